```python
import math
import jax
import jax.numpy as jnp
from jax import lax
import numpy as np


D_MODEL = 1024
BATCH = 2
SEQ = 8192
DEPTH = 2
DEC_BATCH = 32
DEC_SEQ = 8
PAST_LEN = 16384
PAGE_SIZE = 128

N_MIXERS = 2
N_A_LAYERS = (DEPTH + 1) // 2
N_B_LAYERS = DEPTH // 2
DN_ALPHA = (2.0 * DEPTH) ** 0.25
DN_BETA = (8.0 * DEPTH) ** -0.25
LN_EPS = 1e-5
F32 = jnp.float32

GLA_HEADS = 4
GLA_DK = D_MODEL // 2 // GLA_HEADS
GLA_DV = D_MODEL // GLA_HEADS
GLA_GATE_RANK = 16
GLA_TAU = 16.0
GLA_CHUNK = 64
GLA_HK = GLA_HEADS * GLA_DK
GLA_HV = GLA_HEADS * GLA_DV
GLA_IN = 2 * GLA_HK + 2 * GLA_HV + GLA_GATE_RANK

NSA_HEADS = 16
NSA_KV_HEADS = 4
NSA_GROUP = NSA_HEADS // NSA_KV_HEADS
NSA_HD = D_MODEL // NSA_HEADS
NSA_QD = NSA_HEADS * NSA_HD
NSA_KVD = NSA_KV_HEADS * NSA_HD
NSA_IN = NSA_QD + 6 * NSA_KVD + 3 * NSA_HEADS
CMP_LEN = 32
CMP_STRIDE = 16
SEL_BLOCK = 64
N_SEL = 16
WINDOW = 512
Q_BLOCK = 128
FORCE_SCORE = 1e6
ROT_DIM = NSA_HD // 4
ROPE_THETA = 500000.0

PEER_HEADS = 8
PEER_NKEYS = 128
PEER_N_EXPERTS = PEER_NKEYS * PEER_NKEYS
PEER_DKEY = 128
PEER_TOPK = 16
PEER_BLOCK = 128

kernel_name = 'hybrid_gla_nsa_peer_step'


def layer_norm(x, g, b):
    xf = x.astype(F32)
    mu = jnp.mean(xf, -1, keepdims=True)
    var = jnp.mean(jnp.square(xf - mu), -1, keepdims=True)
    return ((xf - mu) * lax.rsqrt(var + LN_EPS) * g.astype(F32) + b.astype(F32)).astype(x.dtype)


def masked_softmax(s, mask):
    s = jnp.where(mask, s.astype(F32), -jnp.inf)
    m = jnp.max(s, -1, keepdims=True)
    m = jnp.where(jnp.isfinite(m), m, 0.0)
    e = jnp.exp(s - m)
    return e / jnp.maximum(jnp.sum(e, -1, keepdims=True), 1e-30)


def partial_rope(x, pos):
    half = ROT_DIM // 2
    inv = ROPE_THETA ** (-jnp.arange(half, dtype=F32) * 2.0 / ROT_DIM)
    ang = pos.astype(F32)[:, None] * inv[None, :]
    cos = jnp.cos(ang)[None, :, None, :]
    sin = jnp.sin(ang)[None, :, None, :]
    xr = x[..., :ROT_DIM].astype(F32)
    x1, x2 = xr[..., :half], xr[..., half:]
    rot = jnp.concatenate([x1 * cos - x2 * sin, x2 * cos + x1 * sin], -1)
    return jnp.concatenate([rot.astype(x.dtype), x[..., ROT_DIM:]], -1)


def ada_modulation(c, w, b):
    mod = jax.nn.silu(c) @ w + b
    return [m[:, None, :] for m in jnp.split(mod, 6, axis=-1)]


def deepnorm_residual(x, sub_out, gate, g, b):
    return layer_norm(DN_ALPHA * x + (1.0 + gate) * sub_out.astype(x.dtype), g, b)


def gla_recurrence(q, k, v, logg, s0):
    B, S, H, DK = q.shape
    DV = v.shape[-1]
    C = GLA_CHUNK if S % GLA_CHUNK == 0 else S
    n = S // C

    def to_chunks(t):
        return t.astype(F32).reshape(B, n, C, H, t.shape[-1]).transpose(1, 0, 2, 3, 4)

    causal = jnp.tril(jnp.ones((C, C), bool))[None, :, :, None, None]

    def step(state, inp):
        qc, kc, vc, gc = inp
        b = jnp.cumsum(gc, axis=1)
        inter = jnp.einsum('bchk,bhkv->bchv', qc * jnp.exp(b), state)
        diff = b[:, :, None] - b[:, None, :]
        decay = jnp.exp(jnp.where(causal, diff, -jnp.inf))
        att = jnp.einsum('bthk,bshk,btshk->bths', qc, kc, decay)
        intra = jnp.einsum('bths,bshv->bthv', att, vc)
        b_last = b[:, -1]
        state = jnp.exp(b_last)[..., None] * state + jnp.einsum(
            'bshk,bshv->bhkv', kc * jnp.exp(b_last[:, None] - b), vc)
        return state, inter + intra

    s_fin, o = lax.scan(step, s0.astype(F32), (to_chunks(q), to_chunks(k), to_chunks(v), to_chunks(logg)))
    return o.transpose(1, 0, 2, 3, 4).reshape(B, S, H, DV), s_fin


def gla_mixer(h, s0, w_in, w_a2, b_a2, gn, w_out):
    B, S, _ = h.shape
    proj = h @ w_in
    q = proj[..., :GLA_HK].reshape(B, S, GLA_HEADS, GLA_DK) * (GLA_DK ** -0.5)
    k = proj[..., GLA_HK:2 * GLA_HK].reshape(B, S, GLA_HEADS, GLA_DK)
    v = proj[..., 2 * GLA_HK:2 * GLA_HK + GLA_HV].reshape(B, S, GLA_HEADS, GLA_DV)
    r = proj[..., 2 * GLA_HK + GLA_HV:2 * GLA_HK + 2 * GLA_HV]
    a = proj[..., 2 * GLA_HK + 2 * GLA_HV:]
    logg = (jax.nn.log_sigmoid((a @ w_a2 + b_a2).astype(F32)) / GLA_TAU).reshape(B, S, GLA_HEADS, GLA_DK)
    o, s_fin = gla_recurrence(q, k, v, logg, s0)
    mu = jnp.mean(o, -1, keepdims=True)
    var = jnp.mean(jnp.square(o - mu), -1, keepdims=True)
    o = (o - mu) * lax.rsqrt(var + LN_EPS) * gn.astype(F32).reshape(GLA_HEADS, GLA_DV)
    y = (o.reshape(B, S, GLA_HV).astype(h.dtype) * jax.nn.silu(r)) @ w_out
    return y, s_fin


def nsa_project(h, pos, w_in):
    B, S, _ = h.shape
    p = h @ w_in
    q = partial_rope(p[..., :NSA_QD].reshape(B, S, NSA_HEADS, NSA_HD), pos) * (NSA_HD ** -0.5)
    kv = p[..., NSA_QD:NSA_QD + 6 * NSA_KVD].reshape(B, S, 6, NSA_KV_HEADS, NSA_HD)
    gates = jax.nn.sigmoid(p[..., NSA_QD + 6 * NSA_KVD:].astype(F32)).reshape(B, S, 3, NSA_HEADS)
    kc = partial_rope(kv[:, :, 0], pos)
    ks = partial_rope(kv[:, :, 2], pos)
    kw = partial_rope(kv[:, :, 4], pos)
    return q, gates, (kc, kv[:, :, 1], ks, kv[:, :, 3], kw, kv[:, :, 5])


def compress_blocks(x, pe):
    B, T = x.shape[:2]
    n_sub = -(-T // CMP_STRIDE)
    x = jnp.pad(x, ((0, 0), (0, n_sub * CMP_STRIDE - T), (0, 0), (0, 0)))
    sub = x.reshape(B, n_sub, CMP_STRIDE, x.shape[2], x.shape[3]).astype(F32)
    w = 1.0 + pe.astype(F32)
    first = jnp.einsum('bnjgd,jgd->bngd', sub, w[:CMP_STRIDE])
    second = jnp.einsum('bnjgd,jgd->bngd', sub, w[CMP_STRIDE:])
    comp = (first[:, :-1] + second[:, 1:]) / CMP_LEN
    ends = CMP_STRIDE * jnp.arange(n_sub - 1, dtype=jnp.int32) + (CMP_LEN - 1)
    return comp, ends


def to_sel_blocks(x):
    B, T = x.shape[:2]
    nsb = -(-T // SEL_BLOCK)
    x = jnp.pad(x, ((0, 0), (0, nsb * SEL_BLOCK - T), (0, 0), (0, 0)))
    return x.reshape(B, nsb, SEL_BLOCK, x.shape[2], x.shape[3])


def nsa_attend(q, gates, pos_q, kcmp, vcmp, cmp_end, ks_blk, vs_blk, kw, vw, pos_w):
    B, Q = q.shape[:2]
    qg = q.reshape(B, Q, NSA_KV_HEADS, NSA_GROUP, NSA_HD).astype(F32)
    s_c = jnp.einsum('bqgrd,bngd->bqgrn', qg, kcmp)
    m_c = (cmp_end[None, :] <= pos_q[:, None])[None, :, None, None, :]
    p_c = masked_softmax(s_c, m_c)
    o_c = jnp.einsum('bqgrn,bngd->bqgrd', p_c, vcmp)
    nsb = ks_blk.shape[1]
    imp = jnp.sum(p_c, axis=3)
    nc = imp.shape[-1]
    n_pair = CMP_LEN // CMP_STRIDE
    ratio = SEL_BLOCK // CMP_STRIDE
    front = n_pair - 1
    imp = jnp.pad(imp, ((0, 0), (0, 0), (0, 0), (front, ratio * nsb - nc)))
    p_slc = sum(imp[..., o:o + ratio * nsb].reshape(B, Q, NSA_KV_HEADS, nsb, ratio).sum(-1)
                for o in range(front + 1))
    jb = jnp.arange(nsb, dtype=jnp.int32)[None, :]
    cur = (pos_q // SEL_BLOCK)[:, None]
    valid = (jb * SEL_BLOCK <= pos_q[:, None])[None, :, None, :]
    forced = ((jb == 0) | (jb == cur) | (jb == cur - 1))[None, :, None, :]
    score = jnp.where(forced, FORCE_SCORE, jnp.where(valid, p_slc, -1.0))
    n_sel = min(N_SEL, nsb)
    _, idx = lax.top_k(score, n_sel)
    bi = jnp.arange(B)[:, None, None, None]
    gi = jnp.arange(NSA_KV_HEADS)[None, None, :, None]
    nk = n_sel * SEL_BLOCK
    k_sel = ks_blk.transpose(0, 3, 1, 2, 4)[bi, gi, idx].reshape(B, Q, NSA_KV_HEADS, nk, NSA_HD)
    v_sel = vs_blk.transpose(0, 3, 1, 2, 4)[bi, gi, idx].reshape(B, Q, NSA_KV_HEADS, nk, NSA_HD)
    kpos = (idx[..., None] * SEL_BLOCK + jnp.arange(SEL_BLOCK, dtype=jnp.int32)).reshape(B, Q, NSA_KV_HEADS, nk)
    m_s = (kpos <= pos_q[None, :, None, None])[:, :, :, None, :]
    s_s = jnp.einsum('bqgrd,bqgkd->bqgrk', qg, k_sel.astype(F32))
    o_s = jnp.einsum('bqgrk,bqgkd->bqgrd', masked_softmax(s_s, m_s), v_sel.astype(F32))
    s_w = jnp.einsum('bqgrd,bkgd->bqgrk', qg, kw.astype(F32))
    dpos = pos_q[:, None] - pos_w[None, :]
    m_w = ((dpos >= 0) & (dpos < WINDOW) & (pos_w[None, :] >= 0))[None, :, None, None, :]
    o_w = jnp.einsum('bqgrk,bkgd->bqgrd', masked_softmax(s_w, m_w), vw.astype(F32))
    gr = gates.reshape(B, Q, 3, NSA_KV_HEADS, NSA_GROUP)[..., None]
    o = gr[:, :, 0] * o_c + gr[:, :, 1] * o_s + gr[:, :, 2] * o_w
    return o.reshape(B, Q, NSA_QD)


def nsa_prompt(h, w_in, pe_k, pe_v, w_out):
    B, S, _ = h.shape
    pos = jnp.arange(S, dtype=jnp.int32)
    q, gates, (kc, vc, ks, vs, kw, vw) = nsa_project(h, pos, w_in)
    kcmp, cmp_end = compress_blocks(kc, pe_k)
    vcmp, _ = compress_blocks(vc, pe_v)
    ks_blk, vs_blk = to_sel_blocks(ks), to_sel_blocks(vs)
    pad = ((0, 0), (WINDOW, 0), (0, 0), (0, 0))
    kw_pad, vw_pad = jnp.pad(kw, pad), jnp.pad(vw, pad)
    band = WINDOW + Q_BLOCK

    def block(i):
        start = i * Q_BLOCK
        q_i = lax.dynamic_slice_in_dim(q, start, Q_BLOCK, 1)
        g_i = lax.dynamic_slice_in_dim(gates, start, Q_BLOCK, 1)
        pos_i = start + jnp.arange(Q_BLOCK, dtype=jnp.int32)
        kw_i = lax.dynamic_slice_in_dim(kw_pad, start, band, 1)
        vw_i = lax.dynamic_slice_in_dim(vw_pad, start, band, 1)
        pos_w = start - WINDOW + jnp.arange(band, dtype=jnp.int32)
        return nsa_attend(q_i, g_i, pos_i, kcmp, vcmp, cmp_end, ks_blk, vs_blk, kw_i, vw_i, pos_w)

    o = lax.map(block, jnp.arange(S // Q_BLOCK, dtype=jnp.int32))
    o = o.transpose(1, 0, 2, 3).reshape(B, S, NSA_QD)
    y = o.astype(h.dtype) @ w_out
    nw = min(WINDOW, S)
    return y, (kc, vc, ks, vs, kw[:, S - nw:], vw[:, S - nw:])


def gather_pages(pool, page_table):
    rows = pool[page_table]
    db, n_pages, page = rows.shape[:3]
    return rows.reshape(db, n_pages * page, rows.shape[3], rows.shape[4])


def nsa_sample(h, ck, cv, sk, sv, wk, wv, page_table, w_in, pe_k, pe_v, w_out):
    B, Q, _ = h.shape
    past = page_table.shape[1] * ck.shape[1]
    pos = past + jnp.arange(Q, dtype=jnp.int32)
    q, gates, (kc, vc, ks, vs, kw, vw) = nsa_project(h, pos, w_in)
    full = lambda pool, new: jnp.concatenate([gather_pages(pool, page_table), new.astype(pool.dtype)], 1)
    kcmp, cmp_end = compress_blocks(full(ck, kc), pe_k)
    vcmp, _ = compress_blocks(full(cv, vc), pe_v)
    ks_blk, vs_blk = to_sel_blocks(full(sk, ks)), to_sel_blocks(full(sv, vs))
    wb = wk.shape[1]
    kw_all = jnp.concatenate([wk, kw.astype(wk.dtype)], 1)
    vw_all = jnp.concatenate([wv, vw.astype(wv.dtype)], 1)
    pos_w = past - wb + jnp.arange(wb + Q, dtype=jnp.int32)
    o = nsa_attend(q, gates, pos, kcmp, vcmp, cmp_end, ks_blk, vs_blk, kw_all, vw_all, pos_w)
    y = o.astype(h.dtype) @ w_out
    return y, (kc, vc, ks, vs, kw_all[:, Q:], vw_all[:, Q:])


def peer_ffn(h, wq, keys, u_tab, v_tab):
    B, S, D = h.shape
    n = B * S
    nb = -(-n // PEER_BLOCK)
    flat = jnp.pad(h.reshape(n, D), ((0, nb * PEER_BLOCK - n), (0, 0))).reshape(nb, PEER_BLOCK, D)

    def block(xb):
        qv = (xb @ wq).reshape(PEER_BLOCK, PEER_HEADS, 2, PEER_DKEY)
        s = jnp.einsum('nhpd,hpkd->nhpk', qv, keys).astype(F32)
        s1, i1 = lax.top_k(s[:, :, 0], PEER_TOPK)
        s2, i2 = lax.top_k(s[:, :, 1], PEER_TOPK)
        cand = (s1[..., :, None] + s2[..., None, :]).reshape(PEER_BLOCK, PEER_HEADS, PEER_TOPK * PEER_TOPK)
        cid = (i1[..., :, None] * PEER_NKEYS + i2[..., None, :]).reshape(PEER_BLOCK, PEER_HEADS, PEER_TOPK * PEER_TOPK)
        top, sel = lax.top_k(cand, PEER_TOPK)
        eid = jnp.take_along_axis(cid, sel, -1)
        g = jax.nn.softmax(top, -1)
        act = jax.nn.gelu(jnp.einsum('nd,nhkd->nhk', xb, u_tab[eid]).astype(F32))
        return jnp.einsum('nhk,nhkd->nd', (g * act).astype(xb.dtype), v_tab[eid])

    out = lax.map(block, flat).reshape(nb * PEER_BLOCK, D)[:n]
    return out.reshape(B, S, D)


def setup_inputs(seed: int = 0) -> dict:
    key = jax.random.key(seed)
    keys = jax.random.split(key, 32)

    def nrm(i, shape, scale):
        return jax.random.normal(keys[i], shape, F32) * scale

    n_pages = PAST_LEN // PAGE_SIZE
    n_used = DEC_BATCH * n_pages
    n_phys = n_used + max(1, n_used // 4)
    win_buf = min(WINDOW, PAST_LEN)
    pool = (N_B_LAYERS, n_phys, PAGE_SIZE, NSA_KV_HEADS, NSA_HD)
    win = (N_B_LAYERS, DEC_BATCH, win_buf, NSA_KV_HEADS, NSA_HD)
    page_table = jax.random.permutation(keys[11], n_phys)[:n_used].reshape(DEC_BATCH, n_pages).astype(jnp.int32)
    d = D_MODEL
    return {
        'x_prompt': nrm(0, (BATCH, SEQ, d), 1.0),
        'x_sample': nrm(1, (DEC_BATCH, DEC_SEQ, d), 1.0),
        'state_gla': nrm(2, (N_A_LAYERS, DEC_BATCH, GLA_HEADS, GLA_DK, GLA_DV), 1.0),
        'cache_cmp_k': nrm(3, pool, 1.0),
        'cache_cmp_v': nrm(4, pool, 1.0),
        'cache_sel_k': nrm(5, pool, 1.0),
        'cache_sel_v': nrm(6, pool, 1.0),
        'cache_win_k': nrm(7, win, 1.0),
        'cache_win_v': nrm(8, win, 1.0),
        'page_table': page_table,
        'c_prompt': nrm(9, (BATCH, d), 1.0),
        'c_sample': nrm(10, (DEC_BATCH, d), 1.0),
        'ada_w': nrm(12, (DEPTH, d, 6 * d), 0.1 * d ** -0.5),
        'ada_b': nrm(13, (DEPTH, 6 * d), 0.02),
        'ln_g': 1.0 + nrm(14, (DEPTH, 2, d), 0.02),
        'ln_b': nrm(15, (DEPTH, 2, d), 0.02),
        'gla_w_in': nrm(16, (N_A_LAYERS, d, GLA_IN), d ** -0.5),
        'gla_w_a2': nrm(17, (N_A_LAYERS, GLA_GATE_RANK, GLA_HK), GLA_GATE_RANK ** -0.5),
        'gla_b_a2': nrm(18, (N_A_LAYERS, GLA_HK), 0.1),
        'gla_gn': 1.0 + nrm(19, (N_A_LAYERS, GLA_HV), 0.02),
        'gla_w_out': nrm(20, (N_A_LAYERS, GLA_HV, d), DN_BETA * GLA_HV ** -0.5),
        'nsa_w_in': nrm(21, (N_B_LAYERS, d, NSA_IN), d ** -0.5),
        'nsa_pe_k': nrm(22, (N_B_LAYERS, CMP_LEN, NSA_KV_HEADS, NSA_HD), 0.1),
        'nsa_pe_v': nrm(23, (N_B_LAYERS, CMP_LEN, NSA_KV_HEADS, NSA_HD), 0.1),
        'nsa_w_out': nrm(24, (N_B_LAYERS, NSA_QD, d), DN_BETA * NSA_QD ** -0.5),
        'peer_wq': nrm(25, (DEPTH, d, PEER_HEADS * 2 * PEER_DKEY), d ** -0.5),
        'peer_keys': nrm(26, (DEPTH, PEER_HEADS, 2, PEER_NKEYS, PEER_DKEY), PEER_DKEY ** -0.5),
        'peer_u': nrm(27, (DEPTH, PEER_N_EXPERTS, d), d ** -0.5),
        'peer_v': nrm(28, (DEPTH, PEER_N_EXPERTS, d), DN_BETA * PEER_HEADS ** -0.5),
    }


def reference(x_prompt, x_sample, state_gla, cache_cmp_k, cache_cmp_v, cache_sel_k, cache_sel_v,
              cache_win_k, cache_win_v, page_table, c_prompt, c_sample, ada_w, ada_b, ln_g, ln_b,
              gla_w_in, gla_w_a2, gla_b_a2, gla_gn, gla_w_out, nsa_w_in, nsa_pe_k, nsa_pe_v, nsa_w_out,
              peer_wq, peer_keys, peer_u, peer_v):
    y_p, y_s = x_prompt, x_sample
    gla_p, gla_s, nsa_p, nsa_s = [], [], [], []
    for i in range(DEPTH):
        mp = ada_modulation(c_prompt, ada_w[i], ada_b[i])
        ms = ada_modulation(c_sample, ada_w[i], ada_b[i])
        h_p = y_p * (1.0 + mp[1]) + mp[0]
        h_s = y_s * (1.0 + ms[1]) + ms[0]
        j = i // N_MIXERS
        if i % N_MIXERS == 0:
            w = (gla_w_in[j], gla_w_a2[j], gla_b_a2[j], gla_gn[j], gla_w_out[j])
            s0 = jnp.zeros((y_p.shape[0], GLA_HEADS, GLA_DK, GLA_DV), F32)
            o_p, st_p = gla_mixer(h_p, s0, *w)
            o_s, st_s = gla_mixer(h_s, state_gla[j], *w)
            gla_p.append(st_p.astype(state_gla.dtype))
            gla_s.append(st_s.astype(state_gla.dtype))
        else:
            o_p, rows_p = nsa_prompt(h_p, nsa_w_in[j], nsa_pe_k[j], nsa_pe_v[j], nsa_w_out[j])
            o_s, rows_s = nsa_sample(h_s, cache_cmp_k[j], cache_cmp_v[j], cache_sel_k[j], cache_sel_v[j],
                                     cache_win_k[j], cache_win_v[j], page_table,
                                     nsa_w_in[j], nsa_pe_k[j], nsa_pe_v[j], nsa_w_out[j])
            nsa_p.append(rows_p)
            nsa_s.append(rows_s)
        y_p = deepnorm_residual(y_p, o_p, mp[2], ln_g[i, 0], ln_b[i, 0])
        y_s = deepnorm_residual(y_s, o_s, ms[2], ln_g[i, 0], ln_b[i, 0])
        h_p = y_p * (1.0 + mp[4]) + mp[3]
        h_s = y_s * (1.0 + ms[4]) + ms[3]
        y_p = deepnorm_residual(y_p, peer_ffn(h_p, peer_wq[i], peer_keys[i], peer_u[i], peer_v[i]), mp[5], ln_g[i, 1], ln_b[i, 1])
        y_s = deepnorm_residual(y_s, peer_ffn(h_s, peer_wq[i], peer_keys[i], peer_u[i], peer_v[i]), ms[5], ln_g[i, 1], ln_b[i, 1])

    gla_state_prompt = jnp.stack(gla_p)
    gla_state_sample = jnp.stack(gla_s)
    cmp_k_prompt = jnp.stack([t[0] for t in nsa_p])
    cmp_v_prompt = jnp.stack([t[1] for t in nsa_p])
    sel_k_prompt = jnp.stack([t[2] for t in nsa_p])
    sel_v_prompt = jnp.stack([t[3] for t in nsa_p])
    win_k_prompt = jnp.stack([t[4] for t in nsa_p])
    win_v_prompt = jnp.stack([t[5] for t in nsa_p])
    cmp_k_sample = jnp.stack([t[0] for t in nsa_s])
    cmp_v_sample = jnp.stack([t[1] for t in nsa_s])
    sel_k_sample = jnp.stack([t[2] for t in nsa_s])
    sel_v_sample = jnp.stack([t[3] for t in nsa_s])
    win_k_sample = jnp.stack([t[4] for t in nsa_s])
    win_v_sample = jnp.stack([t[5] for t in nsa_s])
    return (y_p, y_s, gla_state_prompt, gla_state_sample,
            cmp_k_prompt, cmp_v_prompt, sel_k_prompt, sel_v_prompt, win_k_prompt, win_v_prompt,
            cmp_k_sample, cmp_v_sample, sel_k_sample, sel_v_sample, win_k_sample, win_v_sample)
```

```python
import functools
import math

import jax
import jax.numpy as jnp
import numpy as np
from jax import lax
from jax.experimental import pallas as pl
from jax.experimental.pallas import tpu as pltpu

D_MODEL = 1024
DEPTH = 2
PAGE_SIZE = 128
N_MIXERS = 2
DN_ALPHA = (2.0 * DEPTH) ** 0.25
LN_EPS = 1e-5
F32 = jnp.float32
BF16 = jnp.bfloat16

GLA_HEADS = 4
GLA_DK = D_MODEL // 2 // GLA_HEADS
GLA_DV = D_MODEL // GLA_HEADS
GLA_GATE_RANK = 16
GLA_TAU = 16.0
GLA_CHUNK = 64
GLA_HK = GLA_HEADS * GLA_DK
GLA_HV = GLA_HEADS * GLA_DV

NSA_HEADS = 16
NSA_KV_HEADS = 4
NSA_GROUP = NSA_HEADS // NSA_KV_HEADS
NSA_HD = D_MODEL // NSA_HEADS
NSA_QD = NSA_HEADS * NSA_HD
NSA_KVD = NSA_KV_HEADS * NSA_HD
CMP_LEN = 32
CMP_STRIDE = 16
SEL_BLOCK = 64
N_SEL = 16
WINDOW = 512
Q_BLOCK = 128
FORCE_SCORE = 1e6
ROT_DIM = NSA_HD // 4
ROPE_THETA = 500000.0

PEER_HEADS = 8
PEER_NKEYS = 128
PEER_DKEY = 128
PEER_TOPK = 16
PEER_BLOCK = 128

LANES = 128
SUBLANES = 8
VMEM_LIMIT_BYTES = 56 * 1024 * 1024


def _round_up(n, m):
    return -(-n // m) * m


def _mod_matmul_kernel(x_ref, sc_ref, sh_ref, w_ref, o_ref, h_ref):
    @pl.when(pl.program_id(1) == 0)
    def _():
        h = x_ref[...] * (1.0 + sc_ref[...]) + sh_ref[...]
        h_ref[...] = h.astype(BF16)

    o_ref[...] = jnp.dot(h_ref[...], w_ref[...], preferred_element_type=F32)


def mod_matmul(x, sc, sh, w, *, rows_per_mod, tm, tn):
    t, d = x.shape
    n = w.shape[1]
    assert t % tm == 0 and n % tn == 0
    if rows_per_mod == 1:
        mod_spec = pl.BlockSpec((tm, d), lambda i, j: (i, 0))
    else:
        assert rows_per_mod % tm == 0
        mod_spec = pl.BlockSpec((None, 1, d), lambda i, j: (i * tm // rows_per_mod, 0, 0))
    return pl.pallas_call(
        _mod_matmul_kernel,
        out_shape=jax.ShapeDtypeStruct((t, n), F32),
        grid=(t // tm, n // tn),
        in_specs=[
            pl.BlockSpec((tm, d), lambda i, j: (i, 0)),
            mod_spec,
            mod_spec,
            pl.BlockSpec((d, tn), lambda i, j: (0, j)),
        ],
        out_specs=pl.BlockSpec((tm, tn), lambda i, j: (i, j)),
        scratch_shapes=[pltpu.VMEM((tm, d), BF16)],
        compiler_params=pltpu.CompilerParams(
            dimension_semantics=("parallel", "arbitrary"),
            vmem_limit_bytes=VMEM_LIMIT_BYTES,
        ),
        name="mod_matmul",
    )(x, sc, sh, w)


def _project(x3, scale, shift, w):
    b, s, d = x3.shape
    n = w.shape[1]
    npad = _round_up(n, LANES)
    wp = jnp.pad(w, ((0, 0), (0, npad - n))).astype(BF16)
    tn = npad
    for cand in (1024, 768, 640, 512, 384, 256, 128):
        if npad % cand == 0:
            tn = cand
            break
    x2 = x3.reshape(b * s, d)
    if s % 512 == 0:
        out = mod_matmul(x2, scale, shift, wp, rows_per_mod=s, tm=512, tn=tn)
    else:
        rows = lambda m: jnp.broadcast_to(m, (b, s, d)).reshape(b * s, d)
        out = mod_matmul(x2, rows(scale), rows(shift), wp, rows_per_mod=1, tm=b * s, tn=tn)
    return out[:, :n].reshape(b, s, n)


def layer_norm(x, g, b):
    xf = x.astype(F32)
    mu = jnp.mean(xf, -1, keepdims=True)
    var = jnp.mean(jnp.square(xf - mu), -1, keepdims=True)
    return ((xf - mu) * lax.rsqrt(var + LN_EPS) * g.astype(F32) + b.astype(F32)).astype(x.dtype)


def masked_softmax(s, mask):
    s = jnp.where(mask, s.astype(F32), -jnp.inf)
    m = jnp.max(s, -1, keepdims=True)
    m = jnp.where(jnp.isfinite(m), m, 0.0)
    e = jnp.exp(s - m)
    return e / jnp.maximum(jnp.sum(e, -1, keepdims=True), 1e-30)


def partial_rope(x, pos):
    half = ROT_DIM // 2
    inv = ROPE_THETA ** (-jnp.arange(half, dtype=F32) * 2.0 / ROT_DIM)
    ang = pos.astype(F32)[:, None] * inv[None, :]
    cos = jnp.cos(ang)[None, :, None, :]
    sin = jnp.sin(ang)[None, :, None, :]
    xr = x[..., :ROT_DIM].astype(F32)
    x1, x2 = xr[..., :half], xr[..., half:]
    rot = jnp.concatenate([x1 * cos - x2 * sin, x2 * cos + x1 * sin], -1)
    return jnp.concatenate([rot.astype(x.dtype), x[..., ROT_DIM:]], -1)


def ada_modulation(c, w, b):
    mod = jax.nn.silu(c) @ w + b
    return [m[:, None, :] for m in jnp.split(mod, 6, axis=-1)]


def deepnorm_residual(x, sub_out, gate, g, b):
    return layer_norm(DN_ALPHA * x + (1.0 + gate) * sub_out.astype(x.dtype), g, b)


def gla_recurrence(q, k, v, logg, s0):
    B, S, H, DK = q.shape
    DV = v.shape[-1]
    C = GLA_CHUNK if S % GLA_CHUNK == 0 else S
    n = S // C

    def to_chunks(t):
        return t.astype(F32).reshape(B, n, C, H, t.shape[-1]).transpose(1, 0, 2, 3, 4)

    causal = jnp.tril(jnp.ones((C, C), bool))[None, :, :, None, None]

    def step(state, inp):
        qc, kc, vc, gc = inp
        b = jnp.cumsum(gc, axis=1)
        inter = jnp.einsum('bchk,bhkv->bchv', qc * jnp.exp(b), state)
        diff = b[:, :, None] - b[:, None, :]
        decay = jnp.exp(jnp.where(causal, diff, -jnp.inf))
        att = jnp.einsum('bthk,bshk,btshk->bths', qc, kc, decay)
        intra = jnp.einsum('bths,bshv->bthv', att, vc)
        b_last = b[:, -1]
        state = jnp.exp(b_last)[..., None] * state + jnp.einsum(
            'bshk,bshv->bhkv', kc * jnp.exp(b_last[:, None] - b), vc)
        return state, inter + intra

    s_fin, o = lax.scan(step, s0.astype(F32), (to_chunks(q), to_chunks(k), to_chunks(v), to_chunks(logg)))
    return o.transpose(1, 0, 2, 3, 4).reshape(B, S, H, DV), s_fin


def gla_mixer(x, scale, shift, s0, w_in, w_a2, b_a2, gn, w_out):
    B, S, _ = x.shape
    proj = _project(x, scale, shift, w_in)
    q = proj[..., :GLA_HK].reshape(B, S, GLA_HEADS, GLA_DK) * (GLA_DK ** -0.5)
    k = proj[..., GLA_HK:2 * GLA_HK].reshape(B, S, GLA_HEADS, GLA_DK)
    v = proj[..., 2 * GLA_HK:2 * GLA_HK + GLA_HV].reshape(B, S, GLA_HEADS, GLA_DV)
    r = proj[..., 2 * GLA_HK + GLA_HV:2 * GLA_HK + 2 * GLA_HV]
    a = proj[..., 2 * GLA_HK + 2 * GLA_HV:]
    logg = (jax.nn.log_sigmoid((a @ w_a2 + b_a2).astype(F32)) / GLA_TAU).reshape(B, S, GLA_HEADS, GLA_DK)
    o, s_fin = gla_recurrence(q, k, v, logg, s0)
    mu = jnp.mean(o, -1, keepdims=True)
    var = jnp.mean(jnp.square(o - mu), -1, keepdims=True)
    o = (o - mu) * lax.rsqrt(var + LN_EPS) * gn.astype(F32).reshape(GLA_HEADS, GLA_DV)
    y = (o.reshape(B, S, GLA_HV) * jax.nn.silu(r)) @ w_out
    return y, s_fin


def nsa_project(x, scale, shift, pos, w_in):
    B, S, _ = x.shape
    p = _project(x, scale, shift, w_in)
    q = partial_rope(p[..., :NSA_QD].reshape(B, S, NSA_HEADS, NSA_HD), pos) * (NSA_HD ** -0.5)
    kv = p[..., NSA_QD:NSA_QD + 6 * NSA_KVD].reshape(B, S, 6, NSA_KV_HEADS, NSA_HD)
    gates = jax.nn.sigmoid(p[..., NSA_QD + 6 * NSA_KVD:].astype(F32)).reshape(B, S, 3, NSA_HEADS)
    kc = partial_rope(kv[:, :, 0], pos)
    ks = partial_rope(kv[:, :, 2], pos)
    kw = partial_rope(kv[:, :, 4], pos)
    return q, gates, (kc, kv[:, :, 1], ks, kv[:, :, 3], kw, kv[:, :, 5])


def compress_blocks(x, pe):
    B, T = x.shape[:2]
    n_sub = -(-T // CMP_STRIDE)
    x = jnp.pad(x, ((0, 0), (0, n_sub * CMP_STRIDE - T), (0, 0), (0, 0)))
    sub = x.reshape(B, n_sub, CMP_STRIDE, x.shape[2], x.shape[3]).astype(F32)
    w = 1.0 + pe.astype(F32)
    first = jnp.einsum('bnjgd,jgd->bngd', sub, w[:CMP_STRIDE])
    second = jnp.einsum('bnjgd,jgd->bngd', sub, w[CMP_STRIDE:])
    comp = (first[:, :-1] + second[:, 1:]) / CMP_LEN
    ends = CMP_STRIDE * jnp.arange(n_sub - 1, dtype=jnp.int32) + (CMP_LEN - 1)
    return comp, ends


def to_sel_blocks(x):
    B, T = x.shape[:2]
    nsb = -(-T // SEL_BLOCK)
    x = jnp.pad(x, ((0, 0), (0, nsb * SEL_BLOCK - T), (0, 0), (0, 0)))
    return x.reshape(B, nsb, SEL_BLOCK, x.shape[2], x.shape[3])


def nsa_attend(q, gates, pos_q, kcmp, vcmp, cmp_end, ks_blk, vs_blk, kw, vw, pos_w):
    B, Q = q.shape[:2]
    qg = q.reshape(B, Q, NSA_KV_HEADS, NSA_GROUP, NSA_HD).astype(F32)
    s_c = jnp.einsum('bqgrd,bngd->bqgrn', qg, kcmp)
    m_c = (cmp_end[None, :] <= pos_q[:, None])[None, :, None, None, :]
    p_c = masked_softmax(s_c, m_c)
    o_c = jnp.einsum('bqgrn,bngd->bqgrd', p_c, vcmp)
    nsb = ks_blk.shape[1]
    imp = jnp.sum(p_c, axis=3)
    nc = imp.shape[-1]
    n_pair = CMP_LEN // CMP_STRIDE
    ratio = SEL_BLOCK // CMP_STRIDE
    front = n_pair - 1
    imp = jnp.pad(imp, ((0, 0), (0, 0), (0, 0), (front, ratio * nsb - nc)))
    p_slc = sum(imp[..., o:o + ratio * nsb].reshape(B, Q, NSA_KV_HEADS, nsb, ratio).sum(-1)
                for o in range(front + 1))
    jb = jnp.arange(nsb, dtype=jnp.int32)[None, :]
    cur = (pos_q // SEL_BLOCK)[:, None]
    valid = (jb * SEL_BLOCK <= pos_q[:, None])[None, :, None, :]
    forced = ((jb == 0) | (jb == cur) | (jb == cur - 1))[None, :, None, :]
    score = jnp.where(forced, FORCE_SCORE, jnp.where(valid, p_slc, -1.0))
    n_sel = min(N_SEL, nsb)
    _, idx = lax.top_k(score, n_sel)
    bi = jnp.arange(B)[:, None, None, None]
    gi = jnp.arange(NSA_KV_HEADS)[None, None, :, None]
    nk = n_sel * SEL_BLOCK
    k_sel = ks_blk.transpose(0, 3, 1, 2, 4)[bi, gi, idx].reshape(B, Q, NSA_KV_HEADS, nk, NSA_HD)
    v_sel = vs_blk.transpose(0, 3, 1, 2, 4)[bi, gi, idx].reshape(B, Q, NSA_KV_HEADS, nk, NSA_HD)
    kpos = (idx[..., None] * SEL_BLOCK + jnp.arange(SEL_BLOCK, dtype=jnp.int32)).reshape(B, Q, NSA_KV_HEADS, nk)
    m_s = (kpos <= pos_q[None, :, None, None])[:, :, :, None, :]
    s_s = jnp.einsum('bqgrd,bqgkd->bqgrk', qg, k_sel.astype(F32))
    o_s = jnp.einsum('bqgrk,bqgkd->bqgrd', masked_softmax(s_s, m_s), v_sel.astype(F32))
    s_w = jnp.einsum('bqgrd,bkgd->bqgrk', qg, kw.astype(F32))
    dpos = pos_q[:, None] - pos_w[None, :]
    m_w = ((dpos >= 0) & (dpos < WINDOW) & (pos_w[None, :] >= 0))[None, :, None, None, :]
    o_w = jnp.einsum('bqgrk,bkgd->bqgrd', masked_softmax(s_w, m_w), vw.astype(F32))
    gr = gates.reshape(B, Q, 3, NSA_KV_HEADS, NSA_GROUP)[..., None]
    o = gr[:, :, 0] * o_c + gr[:, :, 1] * o_s + gr[:, :, 2] * o_w
    return o.reshape(B, Q, NSA_QD)


def nsa_prompt(x, scale, shift, w_in, pe_k, pe_v, w_out):
    B, S, _ = x.shape
    pos = jnp.arange(S, dtype=jnp.int32)
    q, gates, (kc, vc, ks, vs, kw, vw) = nsa_project(x, scale, shift, pos, w_in)
    kcmp, cmp_end = compress_blocks(kc, pe_k)
    vcmp, _ = compress_blocks(vc, pe_v)
    ks_blk, vs_blk = to_sel_blocks(ks), to_sel_blocks(vs)
    pad = ((0, 0), (WINDOW, 0), (0, 0), (0, 0))
    kw_pad, vw_pad = jnp.pad(kw, pad), jnp.pad(vw, pad)
    band = WINDOW + Q_BLOCK

    def block(i):
        start = i * Q_BLOCK
        q_i = lax.dynamic_slice_in_dim(q, start, Q_BLOCK, 1)
        g_i = lax.dynamic_slice_in_dim(gates, start, Q_BLOCK, 1)
        pos_i = start + jnp.arange(Q_BLOCK, dtype=jnp.int32)
        kw_i = lax.dynamic_slice_in_dim(kw_pad, start, band, 1)
        vw_i = lax.dynamic_slice_in_dim(vw_pad, start, band, 1)
        pos_w = start - WINDOW + jnp.arange(band, dtype=jnp.int32)
        return nsa_attend(q_i, g_i, pos_i, kcmp, vcmp, cmp_end, ks_blk, vs_blk, kw_i, vw_i, pos_w)

    o = lax.map(block, jnp.arange(S // Q_BLOCK, dtype=jnp.int32))
    o = o.transpose(1, 0, 2, 3).reshape(B, S, NSA_QD)
    y = o @ w_out
    nw = min(WINDOW, S)
    return y, (kc, vc, ks, vs, kw[:, S - nw:], vw[:, S - nw:])


def gather_pages(pool, page_table):
    rows = pool[page_table]
    db, n_pages, page = rows.shape[:3]
    return rows.reshape(db, n_pages * page, rows.shape[3], rows.shape[4])


def nsa_sample(x, scale, shift, ck, cv, sk, sv, wk, wv, page_table, w_in, pe_k, pe_v, w_out):
    B, Q, _ = x.shape
    past = page_table.shape[1] * ck.shape[1]
    pos = past + jnp.arange(Q, dtype=jnp.int32)
    q, gates, (kc, vc, ks, vs, kw, vw) = nsa_project(x, scale, shift, pos, w_in)
    full = lambda pool, new: jnp.concatenate([gather_pages(pool, page_table), new.astype(pool.dtype)], 1)
    kcmp, cmp_end = compress_blocks(full(ck, kc), pe_k)
    vcmp, _ = compress_blocks(full(cv, vc), pe_v)
    ks_blk, vs_blk = to_sel_blocks(full(sk, ks)), to_sel_blocks(full(sv, vs))
    wb = wk.shape[1]
    kw_all = jnp.concatenate([wk, kw.astype(wk.dtype)], 1)
    vw_all = jnp.concatenate([wv, vw.astype(wv.dtype)], 1)
    pos_w = past - wb + jnp.arange(wb + Q, dtype=jnp.int32)
    o = nsa_attend(q, gates, pos, kcmp, vcmp, cmp_end, ks_blk, vs_blk, kw_all, vw_all, pos_w)
    y = o @ w_out
    return y, (kc, vc, ks, vs, kw_all[:, Q:], vw_all[:, Q:])


def peer_ffn(x, scale, shift, wq, keys, u_tab, v_tab):
    B, S, D = x.shape
    n = B * S
    h = x * (1.0 + scale) + shift
    qv_all = _project(x, scale, shift, wq).reshape(n, PEER_HEADS, 2, PEER_DKEY)
    nb = -(-n // PEER_BLOCK)
    flat = jnp.pad(h.reshape(n, D), ((0, nb * PEER_BLOCK - n), (0, 0))).reshape(nb, PEER_BLOCK, D)
    qv_all = jnp.pad(qv_all, ((0, nb * PEER_BLOCK - n), (0, 0), (0, 0), (0, 0))).reshape(
        nb, PEER_BLOCK, PEER_HEADS, 2, PEER_DKEY)

    def block(inp):
        xb, qv = inp
        s = jnp.einsum('nhpd,hpkd->nhpk', qv, keys).astype(F32)
        s1, i1 = lax.top_k(s[:, :, 0], PEER_TOPK)
        s2, i2 = lax.top_k(s[:, :, 1], PEER_TOPK)
        cand = (s1[..., :, None] + s2[..., None, :]).reshape(PEER_BLOCK, PEER_HEADS, PEER_TOPK * PEER_TOPK)
        cid = (i1[..., :, None] * PEER_NKEYS + i2[..., None, :]).reshape(PEER_BLOCK, PEER_HEADS, PEER_TOPK * PEER_TOPK)
        top, sel = lax.top_k(cand, PEER_TOPK)
        eid = jnp.take_along_axis(cid, sel, -1)
        g = jax.nn.softmax(top, -1)
        act = jax.nn.gelu(jnp.einsum('nd,nhkd->nhk', xb, u_tab[eid]).astype(F32))
        return jnp.einsum('nhk,nhkd->nd', (g * act).astype(xb.dtype), v_tab[eid])

    out = lax.map(block, (flat, qv_all)).reshape(nb * PEER_BLOCK, D)[:n]
    return out.reshape(B, S, D)


def kernel(x_prompt, x_sample, state_gla, cache_cmp_k, cache_cmp_v, cache_sel_k, cache_sel_v, cache_win_k, cache_win_v, page_table, c_prompt, c_sample, ada_w, ada_b, ln_g, ln_b, gla_w_in, gla_w_a2, gla_b_a2, gla_gn, gla_w_out, nsa_w_in, nsa_pe_k, nsa_pe_v, nsa_w_out, peer_wq, peer_keys, peer_u, peer_v):
    y_p, y_s = x_prompt, x_sample
    gla_p, gla_s, nsa_p, nsa_s = [], [], [], []
    for i in range(DEPTH):
        mp = ada_modulation(c_prompt, ada_w[i], ada_b[i])
        ms = ada_modulation(c_sample, ada_w[i], ada_b[i])
        j = i // N_MIXERS
        if i % N_MIXERS == 0:
            w = (gla_w_in[j], gla_w_a2[j], gla_b_a2[j], gla_gn[j], gla_w_out[j])
            s0 = jnp.zeros((y_p.shape[0], GLA_HEADS, GLA_DK, GLA_DV), F32)
            o_p, st_p = gla_mixer(y_p, mp[1], mp[0], s0, *w)
            o_s, st_s = gla_mixer(y_s, ms[1], ms[0], state_gla[j], *w)
            gla_p.append(st_p.astype(state_gla.dtype))
            gla_s.append(st_s.astype(state_gla.dtype))
        else:
            o_p, rows_p = nsa_prompt(y_p, mp[1], mp[0], nsa_w_in[j], nsa_pe_k[j], nsa_pe_v[j], nsa_w_out[j])
            o_s, rows_s = nsa_sample(y_s, ms[1], ms[0], cache_cmp_k[j], cache_cmp_v[j], cache_sel_k[j],
                                     cache_sel_v[j], cache_win_k[j], cache_win_v[j], page_table,
                                     nsa_w_in[j], nsa_pe_k[j], nsa_pe_v[j], nsa_w_out[j])
            nsa_p.append(rows_p)
            nsa_s.append(rows_s)
        y_p = deepnorm_residual(y_p, o_p, mp[2], ln_g[i, 0], ln_b[i, 0])
        y_s = deepnorm_residual(y_s, o_s, ms[2], ln_g[i, 0], ln_b[i, 0])
        y_p = deepnorm_residual(y_p, peer_ffn(y_p, mp[4], mp[3], peer_wq[i], peer_keys[i], peer_u[i], peer_v[i]),
                                mp[5], ln_g[i, 1], ln_b[i, 1])
        y_s = deepnorm_residual(y_s, peer_ffn(y_s, ms[4], ms[3], peer_wq[i], peer_keys[i], peer_u[i], peer_v[i]),
                                ms[5], ln_g[i, 1], ln_b[i, 1])

    st = lambda ts, k: jnp.stack([t[k] for t in ts])
    return (y_p, y_s, jnp.stack(gla_p), jnp.stack(gla_s),
            st(nsa_p, 0), st(nsa_p, 1), st(nsa_p, 2), st(nsa_p, 3), st(nsa_p, 4), st(nsa_p, 5),
            st(nsa_s, 0), st(nsa_s, 1), st(nsa_s, 2), st(nsa_s, 3), st(nsa_s, 4), st(nsa_s, 5))
```

```python
import functools
import math

import jax
import jax.numpy as jnp
import numpy as np
from jax import lax
from jax.experimental import pallas as pl
from jax.experimental.pallas import tpu as pltpu

D_MODEL = 1024
DEPTH = 2
PAGE_SIZE = 128
N_MIXERS = 2
DN_ALPHA = (2.0 * DEPTH) ** 0.25
LN_EPS = 1e-5
F32 = jnp.float32
BF16 = jnp.bfloat16

GLA_HEADS = 4
GLA_DK = D_MODEL // 2 // GLA_HEADS
GLA_DV = D_MODEL // GLA_HEADS
GLA_GATE_RANK = 16
GLA_TAU = 16.0
GLA_CHUNK = 64
GLA_HK = GLA_HEADS * GLA_DK
GLA_HV = GLA_HEADS * GLA_DV

NSA_HEADS = 16
NSA_KV_HEADS = 4
NSA_GROUP = NSA_HEADS // NSA_KV_HEADS
NSA_HD = D_MODEL // NSA_HEADS
NSA_QD = NSA_HEADS * NSA_HD
NSA_KVD = NSA_KV_HEADS * NSA_HD
CMP_LEN = 32
CMP_STRIDE = 16
SEL_BLOCK = 64
N_SEL = 16
WINDOW = 512
Q_BLOCK = 128
FORCE_SCORE = 1e6
ROT_DIM = NSA_HD // 4
ROPE_THETA = 500000.0

PEER_HEADS = 8
PEER_NKEYS = 128
PEER_DKEY = 128
PEER_TOPK = 16
PEER_BLOCK = 128

LANES = 128
SUBLANES = 8
VMEM_LIMIT_BYTES = 56 * 1024 * 1024


def _round_up(n, m):
    return -(-n // m) * m


def _mod_matmul_kernel(x_ref, sc_ref, sh_ref, w_ref, o_ref, h_ref):
    @pl.when(pl.program_id(1) == 0)
    def _():
        h = x_ref[...] * (1.0 + sc_ref[...]) + sh_ref[...]
        h_ref[...] = h.astype(BF16)

    o_ref[...] = jnp.dot(h_ref[...], w_ref[...], preferred_element_type=F32)


def mod_matmul(x, sc, sh, w, *, rows_per_mod, tm, tn):
    t, d = x.shape
    n = w.shape[1]
    assert t % tm == 0 and n % tn == 0
    if rows_per_mod == 1:
        mod_spec = pl.BlockSpec((tm, d), lambda i, j: (i, 0))
    else:
        assert rows_per_mod % tm == 0
        mod_spec = pl.BlockSpec((None, 1, d), lambda i, j: (i * tm // rows_per_mod, 0, 0))
    return pl.pallas_call(
        _mod_matmul_kernel,
        out_shape=(jax.ShapeDtypeStruct((t, n), F32), jax.ShapeDtypeStruct((t, d), BF16)),
        grid=(t // tm, n // tn),
        in_specs=[
            pl.BlockSpec((tm, d), lambda i, j: (i, 0)),
            mod_spec,
            mod_spec,
            pl.BlockSpec((d, tn), lambda i, j: (0, j)),
        ],
        out_specs=(pl.BlockSpec((tm, tn), lambda i, j: (i, j)), pl.BlockSpec((tm, d), lambda i, j: (i, 0))),
        compiler_params=pltpu.CompilerParams(
            dimension_semantics=("parallel", "arbitrary"),
            vmem_limit_bytes=VMEM_LIMIT_BYTES,
        ),
        name="mod_matmul",
    )(x, sc, sh, w)


def _project(x3, scale, shift, w, with_h=False, flat_padded=False):
    b, s, d = x3.shape
    n = w.shape[1]
    npad = _round_up(n, LANES)
    wp = jnp.pad(w, ((0, 0), (0, npad - n))).astype(BF16)
    tn = npad
    for cand in (1024, 768, 640, 512, 384, 256, 128):
        if npad % cand == 0:
            tn = cand
            break
    x2 = x3.reshape(b * s, d)
    if s % 512 == 0:
        out, h = mod_matmul(x2, scale, shift, wp, rows_per_mod=s, tm=512, tn=tn)
    else:
        rows = lambda m: jnp.broadcast_to(m, (b, s, d)).reshape(b * s, d)
        out, h = mod_matmul(x2, rows(scale), rows(shift), wp, rows_per_mod=1, tm=b * s, tn=tn)
    if not flat_padded:
        out = out[:, :n].reshape(b, s, n)
    return (out, h) if with_h else out


def layer_norm(x, g, b):
    xf = x.astype(F32)
    mu = jnp.mean(xf, -1, keepdims=True)
    var = jnp.mean(jnp.square(xf - mu), -1, keepdims=True)
    return ((xf - mu) * lax.rsqrt(var + LN_EPS) * g.astype(F32) + b.astype(F32)).astype(x.dtype)


def masked_softmax(s, mask):
    s = jnp.where(mask, s.astype(F32), -jnp.inf)
    m = jnp.max(s, -1, keepdims=True)
    m = jnp.where(jnp.isfinite(m), m, 0.0)
    e = jnp.exp(s - m)
    return e / jnp.maximum(jnp.sum(e, -1, keepdims=True), 1e-30)


def partial_rope(x, pos):
    half = ROT_DIM // 2
    inv = ROPE_THETA ** (-jnp.arange(half, dtype=F32) * 2.0 / ROT_DIM)
    ang = pos.astype(F32)[:, None] * inv[None, :]
    cos = jnp.cos(ang)[None, :, None, :]
    sin = jnp.sin(ang)[None, :, None, :]
    xr = x[..., :ROT_DIM].astype(F32)
    x1, x2 = xr[..., :half], xr[..., half:]
    rot = jnp.concatenate([x1 * cos - x2 * sin, x2 * cos + x1 * sin], -1)
    return jnp.concatenate([rot.astype(x.dtype), x[..., ROT_DIM:]], -1)


def ada_modulation(c, w, b):
    mod = jax.nn.silu(c) @ w + b
    return [m[:, None, :] for m in jnp.split(mod, 6, axis=-1)]


def deepnorm_residual(x, sub_out, gate, g, b):
    return layer_norm(DN_ALPHA * x + (1.0 + gate) * sub_out.astype(x.dtype), g, b)


def gla_recurrence(q, k, v, logg, s0):
    B, S, H, DK = q.shape
    DV = v.shape[-1]
    C = GLA_CHUNK if S % GLA_CHUNK == 0 else S
    n = S // C

    def to_chunks(t):
        return t.astype(F32).reshape(B, n, C, H, t.shape[-1]).transpose(1, 0, 2, 3, 4)

    causal = jnp.tril(jnp.ones((C, C), bool))[None, :, :, None, None]

    def step(state, inp):
        qc, kc, vc, gc = inp
        b = jnp.cumsum(gc, axis=1)
        inter = jnp.einsum('bchk,bhkv->bchv', qc * jnp.exp(b), state)
        diff = b[:, :, None] - b[:, None, :]
        decay = jnp.exp(jnp.where(causal, diff, -jnp.inf))
        att = jnp.einsum('bthk,bshk,btshk->bths', qc, kc, decay)
        intra = jnp.einsum('bths,bshv->bthv', att, vc)
        b_last = b[:, -1]
        state = jnp.exp(b_last)[..., None] * state + jnp.einsum(
            'bshk,bshv->bhkv', kc * jnp.exp(b_last[:, None] - b), vc)
        return state, inter + intra

    s_fin, o = lax.scan(step, s0.astype(F32), (to_chunks(q), to_chunks(k), to_chunks(v), to_chunks(logg)))
    return o.transpose(1, 0, 2, 3, 4).reshape(B, S, H, DV), s_fin


def gla_mixer(x, scale, shift, s0, w_in, w_a2, b_a2, gn, w_out):
    B, S, _ = x.shape
    proj = _project(x, scale, shift, w_in)
    q = proj[..., :GLA_HK].reshape(B, S, GLA_HEADS, GLA_DK) * (GLA_DK ** -0.5)
    k = proj[..., GLA_HK:2 * GLA_HK].reshape(B, S, GLA_HEADS, GLA_DK)
    v = proj[..., 2 * GLA_HK:2 * GLA_HK + GLA_HV].reshape(B, S, GLA_HEADS, GLA_DV)
    r = proj[..., 2 * GLA_HK + GLA_HV:2 * GLA_HK + 2 * GLA_HV]
    a = proj[..., 2 * GLA_HK + 2 * GLA_HV:]
    logg = (jax.nn.log_sigmoid((a @ w_a2 + b_a2).astype(F32)) / GLA_TAU).reshape(B, S, GLA_HEADS, GLA_DK)
    o, s_fin = gla_recurrence(q, k, v, logg, s0)
    mu = jnp.mean(o, -1, keepdims=True)
    var = jnp.mean(jnp.square(o - mu), -1, keepdims=True)
    o = (o - mu) * lax.rsqrt(var + LN_EPS) * gn.astype(F32).reshape(GLA_HEADS, GLA_DV)
    y = (o.reshape(B, S, GLA_HV) * jax.nn.silu(r)) @ w_out
    return y, s_fin


def nsa_project(x, scale, shift, pos, w_in):
    B, S, _ = x.shape
    p = _project(x, scale, shift, w_in)
    q = partial_rope(p[..., :NSA_QD].reshape(B, S, NSA_HEADS, NSA_HD), pos) * (NSA_HD ** -0.5)
    kv = p[..., NSA_QD:NSA_QD + 6 * NSA_KVD].reshape(B, S, 6, NSA_KV_HEADS, NSA_HD)
    gates = jax.nn.sigmoid(p[..., NSA_QD + 6 * NSA_KVD:].astype(F32)).reshape(B, S, 3, NSA_HEADS)
    kc = partial_rope(kv[:, :, 0], pos)
    ks = partial_rope(kv[:, :, 2], pos)
    kw = partial_rope(kv[:, :, 4], pos)
    return q, gates, (kc, kv[:, :, 1], ks, kv[:, :, 3], kw, kv[:, :, 5])


def compress_blocks(x, pe):
    B, T = x.shape[:2]
    n_sub = -(-T // CMP_STRIDE)
    x = jnp.pad(x, ((0, 0), (0, n_sub * CMP_STRIDE - T), (0, 0), (0, 0)))
    sub = x.reshape(B, n_sub, CMP_STRIDE, x.shape[2], x.shape[3]).astype(F32)
    w = 1.0 + pe.astype(F32)
    first = jnp.einsum('bnjgd,jgd->bngd', sub, w[:CMP_STRIDE])
    second = jnp.einsum('bnjgd,jgd->bngd', sub, w[CMP_STRIDE:])
    comp = (first[:, :-1] + second[:, 1:]) / CMP_LEN
    ends = CMP_STRIDE * jnp.arange(n_sub - 1, dtype=jnp.int32) + (CMP_LEN - 1)
    return comp, ends


def to_sel_blocks(x):
    B, T = x.shape[:2]
    nsb = -(-T // SEL_BLOCK)
    x = jnp.pad(x, ((0, 0), (0, nsb * SEL_BLOCK - T), (0, 0), (0, 0)))
    return x.reshape(B, nsb, SEL_BLOCK, x.shape[2], x.shape[3])


def nsa_attend(q, gates, pos_q, kcmp, vcmp, cmp_end, ks_blk, vs_blk, kw, vw, pos_w):
    B, Q = q.shape[:2]
    qg = q.reshape(B, Q, NSA_KV_HEADS, NSA_GROUP, NSA_HD).astype(F32)
    s_c = jnp.einsum('bqgrd,bngd->bqgrn', qg, kcmp)
    m_c = (cmp_end[None, :] <= pos_q[:, None])[None, :, None, None, :]
    p_c = masked_softmax(s_c, m_c)
    o_c = jnp.einsum('bqgrn,bngd->bqgrd', p_c, vcmp)
    nsb = ks_blk.shape[1]
    imp = jnp.sum(p_c, axis=3)
    nc = imp.shape[-1]
    n_pair = CMP_LEN // CMP_STRIDE
    ratio = SEL_BLOCK // CMP_STRIDE
    front = n_pair - 1
    imp = jnp.pad(imp, ((0, 0), (0, 0), (0, 0), (front, ratio * nsb - nc)))
    p_slc = sum(imp[..., o:o + ratio * nsb].reshape(B, Q, NSA_KV_HEADS, nsb, ratio).sum(-1)
                for o in range(front + 1))
    jb = jnp.arange(nsb, dtype=jnp.int32)[None, :]
    cur = (pos_q // SEL_BLOCK)[:, None]
    valid = (jb * SEL_BLOCK <= pos_q[:, None])[None, :, None, :]
    forced = ((jb == 0) | (jb == cur) | (jb == cur - 1))[None, :, None, :]
    score = jnp.where(forced, FORCE_SCORE, jnp.where(valid, p_slc, -1.0))
    n_sel = min(N_SEL, nsb)
    _, idx = lax.top_k(score, n_sel)
    bi = jnp.arange(B)[:, None, None, None]
    gi = jnp.arange(NSA_KV_HEADS)[None, None, :, None]
    nk = n_sel * SEL_BLOCK
    k_sel = ks_blk.transpose(0, 3, 1, 2, 4)[bi, gi, idx].reshape(B, Q, NSA_KV_HEADS, nk, NSA_HD)
    v_sel = vs_blk.transpose(0, 3, 1, 2, 4)[bi, gi, idx].reshape(B, Q, NSA_KV_HEADS, nk, NSA_HD)
    kpos = (idx[..., None] * SEL_BLOCK + jnp.arange(SEL_BLOCK, dtype=jnp.int32)).reshape(B, Q, NSA_KV_HEADS, nk)
    m_s = (kpos <= pos_q[None, :, None, None])[:, :, :, None, :]
    s_s = jnp.einsum('bqgrd,bqgkd->bqgrk', qg, k_sel.astype(F32))
    o_s = jnp.einsum('bqgrk,bqgkd->bqgrd', masked_softmax(s_s, m_s), v_sel.astype(F32))
    s_w = jnp.einsum('bqgrd,bkgd->bqgrk', qg, kw.astype(F32))
    dpos = pos_q[:, None] - pos_w[None, :]
    m_w = ((dpos >= 0) & (dpos < WINDOW) & (pos_w[None, :] >= 0))[None, :, None, None, :]
    o_w = jnp.einsum('bqgrk,bkgd->bqgrd', masked_softmax(s_w, m_w), vw.astype(F32))
    gr = gates.reshape(B, Q, 3, NSA_KV_HEADS, NSA_GROUP)[..., None]
    o = gr[:, :, 0] * o_c + gr[:, :, 1] * o_s + gr[:, :, 2] * o_w
    return o.reshape(B, Q, NSA_QD)


def _nsa_q_perm():
    cols = []
    for m in range(NSA_KV_HEADS // 2):
        for r in range(NSA_GROUP):
            for half in range(2):
                head = (2 * m + half) * NSA_GROUP + r
                cols.append(np.arange(NSA_HD) + head * NSA_HD)
    return np.concatenate(cols)


def _nsa_gate_expand():
    perm = _nsa_q_perm()
    ex = np.zeros((LANES, 3 * NSA_QD), np.float32)
    for br in range(3):
        for col in range(NSA_QD):
            ex[br * NSA_HEADS + perm[col] // NSA_HD, br * NSA_QD + col] = 1.0
    return ex


def _nsa_slc_matrix(nc_pad, nsb_pad):
    ratio = SEL_BLOCK // CMP_STRIDE
    m = np.zeros((nc_pad, nsb_pad), np.float32)
    for j in range(nsb_pad):
        for o in range(CMP_LEN // CMP_STRIDE):
            for i in range(ratio):
                n = ratio * j + i + o - (CMP_LEN // CMP_STRIDE - 1)
                if 0 <= n < nc_pad:
                    m[n, j] += 1.0
    return m


def _rope_tables(pos):
    half = ROT_DIM // 2
    inv = ROPE_THETA ** (-jnp.arange(half, dtype=F32) * 2.0 / ROT_DIM)
    ang = pos.astype(F32)[:, None] * inv[None, :]
    cos, sin = jnp.cos(ang), jnp.sin(ang)
    t = pos.shape[0]
    ones = jnp.ones((t, NSA_HD - ROT_DIM), F32)
    zeros = jnp.zeros((t, NSA_HD - ROT_DIM), F32)
    z8 = jnp.zeros((t, half), F32)
    c = jnp.concatenate([cos, cos, ones], 1)
    up = jnp.concatenate([-sin, z8, zeros], 1)
    dn = jnp.concatenate([z8, sin, zeros], 1)
    two = lambda a: jnp.concatenate([a, a], 1)
    return two(c), two(up), two(dn)


def _nsa_prep_kernel(p_ref, cos_ref, up_ref, dn_ref, q_ref, kc_ref, vc_ref, ks_ref, vs_ref, kw_ref, vw_ref,
                     ksb_ref, vsb_ref, kwb_ref, vwb_ref, g_ref):
    def rope(x):
        reps = x.shape[1] // LANES
        tile = lambda a: jnp.concatenate([a] * reps, axis=1)
        w = x.shape[1]
        return (x * tile(cos_ref[...]) + pltpu.roll(x, w - ROT_DIM // 2, 1) * tile(up_ref[...])
                + pltpu.roll(x, ROT_DIM // 2, 1) * tile(dn_ref[...]))

    q_ref[...] = (rope(p_ref[:, :NSA_QD]) * (NSA_HD ** -0.5)).astype(BF16)
    kv = lambda k: p_ref[:, NSA_QD + k * NSA_KVD:NSA_QD + (k + 1) * NSA_KVD]
    kc_ref[...] = rope(kv(0))
    vc_ref[...] = kv(1)
    ks = rope(kv(2))
    ks_ref[...] = ks
    ksb_ref[...] = ks.astype(BF16)
    vs_ref[...] = kv(3)
    vsb_ref[...] = kv(3).astype(BF16)
    kw = rope(kv(4))
    kw_ref[...] = kw
    kwb_ref[...] = kw.astype(BF16)
    vw_ref[...] = kv(5)
    vwb_ref[...] = kv(5).astype(BF16)
    g0 = NSA_QD + 6 * NSA_KVD
    g_ref[...] = jax.nn.sigmoid(p_ref[:, g0:g0 + LANES])


def nsa_prep(p, cos, up, dn, *, tm):
    t = p.shape[0]
    period = cos.shape[0] // tm
    tab = pl.BlockSpec((tm, LANES), lambda i: (i % period, 0))
    kv32 = jax.ShapeDtypeStruct((t, NSA_KVD), F32)
    kv16 = jax.ShapeDtypeStruct((t, NSA_KVD), BF16)
    kvs = pl.BlockSpec((tm, NSA_KVD), lambda i: (i, 0))
    return pl.pallas_call(
        _nsa_prep_kernel,
        out_shape=(jax.ShapeDtypeStruct((t, NSA_QD), BF16),) + (kv32,) * 6 + (kv16,) * 4
        + (jax.ShapeDtypeStruct((t, LANES), F32),),
        grid=(t // tm,),
        in_specs=[pl.BlockSpec((tm, p.shape[1]), lambda i: (i, 0)), tab, tab, tab],
        out_specs=(pl.BlockSpec((tm, NSA_QD), lambda i: (i, 0)),) + (kvs,) * 10
        + (pl.BlockSpec((tm, LANES), lambda i: (i, 0)),),
        compiler_params=pltpu.CompilerParams(
            dimension_semantics=("parallel",), vmem_limit_bytes=VMEM_LIMIT_BYTES),
        name="nsa_prep",
    )(p, cos, up, dn)


def _nsa_compress_kernel(x_ref, w_ref, o_ref):
    n_sub = x_ref.shape[0] // CMP_STRIDE
    x = x_ref[...].reshape(n_sub, CMP_STRIDE, NSA_KVD)
    w = 1.0 + w_ref[...]
    first = jnp.sum(x * w[None, :CMP_STRIDE], axis=1)
    second = jnp.sum(x * w[None, CMP_STRIDE:], axis=1)
    nxt = pltpu.roll(second, n_sub - 1, 0)
    row = lax.broadcasted_iota(jnp.int32, (n_sub, 1), 0)
    o_ref[...] = jnp.where(row < n_sub - 1, (first + nxt) / CMP_LEN, 0.0).astype(o_ref.dtype)


def nsa_compress(x, pe, b, s):
    n_sub = s // CMP_STRIDE
    return pl.pallas_call(
        _nsa_compress_kernel,
        out_shape=jax.ShapeDtypeStruct((b, n_sub, NSA_KVD), BF16),
        grid=(b,),
        in_specs=[pl.BlockSpec((s, NSA_KVD), lambda i: (i, 0)),
                  pl.BlockSpec((CMP_LEN, NSA_KVD), lambda i: (0, 0))],
        out_specs=pl.BlockSpec((None, n_sub, NSA_KVD), lambda i: (i, 0, 0)),
        compiler_params=pltpu.CompilerParams(
            dimension_semantics=("parallel",), vmem_limit_bytes=VMEM_LIMIT_BYTES),
        name="nsa_compress",
    )(x, pe.reshape(CMP_LEN, NSA_KVD))


NSA_KV_TILE = 512


def _softmax_rows(s, allowed):
    s = jnp.where(allowed, s, -jnp.inf)
    m = jnp.max(s, axis=1, keepdims=True)
    m = jnp.where(m == -jnp.inf, 0.0, m)
    e = jnp.exp(s - m)
    return e / jnp.maximum(jnp.sum(e, axis=1, keepdims=True), 1e-30)


def _select_blocks(score, n_sel):
    nb = score.shape[1]
    lane = lax.broadcasted_iota(jnp.int32, score.shape, 1).astype(F32)
    sel = jnp.zeros(score.shape, F32)
    for _ in range(n_sel):
        m = jnp.max(score, axis=1, keepdims=True)
        first = jnp.min(jnp.where(score == m, lane, float(nb)), axis=1, keepdims=True)
        pick = lane == first
        sel = jnp.where(pick, 1.0, sel)
        score = jnp.where(pick, -jnp.inf, score)
    return sel


def _nsa_attn_kernel(q_ref, g_ref, kcmp_ref, vcmp_ref, ks_ref, vs_ref, kw_ref, vw_ref, slc_ref, ex_ref,
                     o_ref, acc_ref):
    i = pl.program_id(1)
    nt = (((1,), (1,)), ((), ()))
    start = i * Q_BLOCK
    pos_q = start + lax.broadcasted_iota(jnp.int32, (Q_BLOCK, 1), 0)
    lane128 = lax.broadcasted_iota(jnp.int32, (1, LANES), 1)
    n_cmp = kcmp_ref.shape[0]
    cmp_end = CMP_STRIDE * lax.broadcasted_iota(jnp.int32, (1, n_cmp), 1) + (CMP_LEN - 1)
    cmp_ok = cmp_end <= pos_q
    jb = lax.broadcasted_iota(jnp.int32, (1, LANES), 1)
    cur = pos_q // SEL_BLOCK
    valid = jb * SEL_BLOCK <= pos_q
    forced = (jb == 0) | (jb == cur) | (jb == cur - 1)
    gexp = jnp.dot(g_ref[...], ex_ref[...], precision=lax.Precision.HIGHEST,
                   preferred_element_type=F32)
    o_ref[...] = jnp.zeros_like(o_ref)

    win_base = pl.multiple_of(jnp.maximum(start - WINDOW, 0), Q_BLOCK)
    band = WINDOW + Q_BLOCK
    kw_t = kw_ref[pl.ds(win_base, band), :]
    vw_t = vw_ref[pl.ds(win_base, band), :]
    dpos = pos_q - (win_base + lax.broadcasted_iota(jnp.int32, (1, band), 1))
    win_ok = (dpos >= 0) & (dpos < WINDOW)

    for g in range(NSA_KV_HEADS):
        m, half = divmod(g, 2)
        in_half = (lane128 // NSA_HD) == half
        mcols = slice(m * LANES, (m + 1) * LANES)
        qz = []
        for r in range(NSA_GROUP):
            t = m * NSA_GROUP + r
            tile = q_ref[:, t * LANES:(t + 1) * LANES]
            sel = jnp.where(in_half, tile, jnp.zeros_like(tile))
            z = jnp.zeros_like(sel)
            qz.append(jnp.concatenate([sel, z] if m == 0 else [z, sel], axis=1))

        imp = jnp.zeros((Q_BLOCK, n_cmp), F32)
        o_c = []
        for r in range(NSA_GROUP):
            s = lax.dot_general(qz[r], kcmp_ref[...], nt, preferred_element_type=F32)
            p = _softmax_rows(s, cmp_ok)
            imp = imp + p
            o_c.append(jnp.dot(p.astype(BF16), vcmp_ref[...], preferred_element_type=F32)[:, mcols])
        p_slc = jnp.dot(imp, slc_ref[...], precision=lax.Precision.HIGHEST, preferred_element_type=F32)
        score = jnp.where(forced, FORCE_SCORE, jnp.where(valid, p_slc, -1.0))
        sel_bf = _select_blocks(score, N_SEL).astype(BF16)

        acc_ref[...] = jnp.zeros_like(acc_ref)

        def kv_step(t, carry):
            k0 = pl.multiple_of(t * NSA_KV_TILE, NSA_KV_TILE)
            k_t = ks_ref[pl.ds(k0, NSA_KV_TILE), :]
            v_t = vs_ref[pl.ds(k0, NSA_KV_TILE), :]
            key = k0 + lax.broadcasted_iota(jnp.int32, (1, NSA_KV_TILE), 1)
            blk = lax.broadcasted_iota(jnp.int32, (LANES, 1), 0)
            expand = jnp.where(blk == key // SEL_BLOCK, 1.0, 0.0).astype(BF16)
            chosen = jnp.dot(sel_bf, expand, preferred_element_type=F32)
            allowed = (chosen > 0.5) & (key <= pos_q)
            out = []
            for r in range(NSA_GROUP):
                m_old, l_old = carry[2 * r], carry[2 * r + 1]
                s = lax.dot_general(qz[r], k_t, nt, preferred_element_type=F32)
                s = jnp.where(allowed, s, -jnp.inf)
                m_new = jnp.maximum(m_old, jnp.max(s, axis=1, keepdims=True))
                m_use = jnp.where(m_new == -jnp.inf, 0.0, m_new)
                alpha = jnp.exp(m_old - m_use)
                p = jnp.exp(s - m_use)
                l_new = alpha * l_old + jnp.sum(p, axis=1, keepdims=True)
                acc_ref[r] = alpha * acc_ref[r] + jnp.dot(p.astype(BF16), v_t, preferred_element_type=F32)
                out += [m_new, l_new]
            return tuple(out)

        init = (jnp.full((Q_BLOCK, 1), -jnp.inf, F32), jnp.zeros((Q_BLOCK, 1), F32)) * NSA_GROUP
        n_tiles = (start + Q_BLOCK + NSA_KV_TILE - 1) // NSA_KV_TILE
        fin = lax.fori_loop(0, n_tiles, kv_step, init)

        for r in range(NSA_GROUP):
            t = m * NSA_GROUP + r
            cols = slice(t * LANES, (t + 1) * LANES)
            o_s = acc_ref[r][:, mcols] / jnp.maximum(fin[2 * r + 1], 1e-30)
            s = lax.dot_general(qz[r], kw_t, nt, preferred_element_type=F32)
            p = _softmax_rows(s, win_ok)
            o_w = jnp.dot(p.astype(BF16), vw_t, preferred_element_type=F32)[:, mcols]
            comb = (gexp[:, cols] * o_c[r] + gexp[:, NSA_QD + t * LANES:NSA_QD + (t + 1) * LANES] * o_s
                    + gexp[:, 2 * NSA_QD + t * LANES:2 * NSA_QD + (t + 1) * LANES] * o_w)
            o_ref[:, cols] += jnp.where(in_half, comb, 0.0)


def nsa_attn_prompt(q, gates, kcmp, vcmp, ks, vs, kw, vw, b, s):
    nq = s // Q_BLOCK
    slc = jnp.asarray(_nsa_slc_matrix(s // CMP_STRIDE, LANES))
    ex = jnp.asarray(_nsa_gate_expand())
    seq = pl.BlockSpec((s, NSA_KVD), lambda bi, i: (bi, 0))
    cmp_spec = pl.BlockSpec((None, s // CMP_STRIDE, NSA_KVD), lambda bi, i: (bi, 0, 0))
    return pl.pallas_call(
        _nsa_attn_kernel,
        out_shape=jax.ShapeDtypeStruct((b * s, NSA_QD), F32),
        grid=(b, nq),
        in_specs=[
            pl.BlockSpec((Q_BLOCK, NSA_QD), lambda bi, i: (bi * nq + i, 0)),
            pl.BlockSpec((Q_BLOCK, LANES), lambda bi, i: (bi * nq + i, 0)),
            cmp_spec, cmp_spec, seq, seq, seq, seq,
            pl.BlockSpec(slc.shape, lambda bi, i: (0, 0)),
            pl.BlockSpec(ex.shape, lambda bi, i: (0, 0)),
        ],
        out_specs=pl.BlockSpec((Q_BLOCK, NSA_QD), lambda bi, i: (bi * nq + i, 0)),
        scratch_shapes=[pltpu.VMEM((NSA_GROUP, Q_BLOCK, NSA_KVD), F32)],
        compiler_params=pltpu.CompilerParams(
            dimension_semantics=("parallel", "arbitrary"), vmem_limit_bytes=VMEM_LIMIT_BYTES),
        name="nsa_attn_prompt",
    )(q, gates, kcmp, vcmp, ks, vs, kw, vw, slc, ex)


def nsa_prompt(x, scale, shift, w_in, pe_k, pe_v, w_out):
    B, S, _ = x.shape
    perm = _nsa_q_perm()
    w_in_p = jnp.concatenate([w_in[:, perm], w_in[:, NSA_QD:]], axis=1)
    p = _project(x, scale, shift, w_in_p, flat_padded=True)
    cos, up, dn = _rope_tables(jnp.arange(S, dtype=jnp.int32))
    q, kc, vc, ks, vs, kw, vw, ksb, vsb, kwb, vwb, gates = nsa_prep(p, cos, up, dn, tm=512)
    kcmp = nsa_compress(kc, pe_k, B, S)
    vcmp = nsa_compress(vc, pe_v, B, S)
    o = nsa_attn_prompt(q, gates, kcmp, vcmp, ksb, vsb, kwb, vwb, B, S)
    y = (o @ w_out[perm]).reshape(B, S, D_MODEL)
    nw = min(WINDOW, S)
    rs = lambda a: a.reshape(B, S, NSA_KV_HEADS, NSA_HD)
    return y, (rs(kc), rs(vc), rs(ks), rs(vs), rs(kw)[:, S - nw:], rs(vw)[:, S - nw:])


def gather_pages(pool, page_table):
    rows = pool[page_table]
    db, n_pages, page = rows.shape[:3]
    return rows.reshape(db, n_pages * page, rows.shape[3], rows.shape[4])


def nsa_sample(x, scale, shift, ck, cv, sk, sv, wk, wv, page_table, w_in, pe_k, pe_v, w_out):
    B, Q, _ = x.shape
    past = page_table.shape[1] * ck.shape[1]
    pos = past + jnp.arange(Q, dtype=jnp.int32)
    q, gates, (kc, vc, ks, vs, kw, vw) = nsa_project(x, scale, shift, pos, w_in)
    full = lambda pool, new: jnp.concatenate([gather_pages(pool, page_table), new.astype(pool.dtype)], 1)
    kcmp, cmp_end = compress_blocks(full(ck, kc), pe_k)
    vcmp, _ = compress_blocks(full(cv, vc), pe_v)
    ks_blk, vs_blk = to_sel_blocks(full(sk, ks)), to_sel_blocks(full(sv, vs))
    wb = wk.shape[1]
    kw_all = jnp.concatenate([wk, kw.astype(wk.dtype)], 1)
    vw_all = jnp.concatenate([wv, vw.astype(wv.dtype)], 1)
    pos_w = past - wb + jnp.arange(wb + Q, dtype=jnp.int32)
    o = nsa_attend(q, gates, pos, kcmp, vcmp, cmp_end, ks_blk, vs_blk, kw_all, vw_all, pos_w)
    y = o @ w_out
    return y, (kc, vc, ks, vs, kw_all[:, Q:], vw_all[:, Q:])


def _top_values(s, k):
    vals = []
    for _ in range(k):
        m = jnp.max(s, axis=0, keepdims=True)
        vals.append(m)
        s = jnp.where(s >= m, -jnp.inf, s)
    return vals


def _peer_route_kernel(qv_ref, keys_ref, s1_ref, s2_ref, e1_ref, e2_ref, th_ref):
    nt = (((1,), (1,)), ((), ()))
    for h in range(PEER_HEADS):
        q1 = qv_ref[:, (2 * h) * PEER_DKEY:(2 * h + 1) * PEER_DKEY]
        q2 = qv_ref[:, (2 * h + 1) * PEER_DKEY:(2 * h + 2) * PEER_DKEY]
        s1 = lax.dot_general(keys_ref[h, 0], q1, nt, precision=lax.Precision.HIGHEST,
                             preferred_element_type=F32)
        s2 = lax.dot_general(keys_ref[h, 1], q2, nt, precision=lax.Precision.HIGHEST,
                             preferred_element_type=F32)
        top1 = _top_values(s1, PEER_TOPK)
        top2 = _top_values(s2, PEER_TOPK)
        t2 = jnp.concatenate(top2, axis=0)
        cand = jnp.concatenate([r + t2 for r in top1], axis=0)
        best = _top_values(cand, PEER_TOPK)
        z = best[0] * 0.0
        for v in best:
            z = z + jnp.exp(v - best[0])
        s1_ref[h] = s1
        s2_ref[h] = s2
        e1_ref[h] = jnp.exp(s1 - top1[0])
        e2_ref[h] = jnp.exp(s2 - top2[0]) / z
        th_ref[h] = best[PEER_TOPK - 1]


def peer_route(qv, keys, *, tm):
    t = qv.shape[0]
    big = jax.ShapeDtypeStruct((PEER_HEADS, PEER_NKEYS, t), F32)
    big_spec = pl.BlockSpec((PEER_HEADS, PEER_NKEYS, tm), lambda i: (0, 0, i))
    return pl.pallas_call(
        _peer_route_kernel,
        out_shape=(big, big, big, big, jax.ShapeDtypeStruct((PEER_HEADS, 1, t), F32)),
        grid=(t // tm,),
        in_specs=[
            pl.BlockSpec((tm, 2 * PEER_HEADS * PEER_DKEY), lambda i: (i, 0)),
            pl.BlockSpec(keys.shape, lambda i: (0, 0, 0, 0)),
        ],
        out_specs=(big_spec, big_spec, big_spec, big_spec,
                   pl.BlockSpec((PEER_HEADS, 1, tm), lambda i: (0, 0, i))),
        compiler_params=pltpu.CompilerParams(
            dimension_semantics=("parallel",), vmem_limit_bytes=VMEM_LIMIT_BYTES),
        name="peer_route",
    )(qv, keys)


PEER_A_PER_STEP = SUBLANES


def _gelu_tanh(x):
    return 0.5 * x * (1.0 + jnp.tanh(math.sqrt(2.0 / math.pi) * (x + 0.044715 * (x * x * x))))


def _peer_dense_kernel(h_ref, s1_ref, e1_ref, s2_ref, e2_ref, th_ref, u_ref, vt_ref, o_ref,
                       act_ref, g_ref, acc_ref):
    j = pl.program_id(1)
    tm = h_ref.shape[0]

    @pl.when(j == 0)
    def _():
        acc_ref[...] = jnp.zeros_like(acc_ref)

    act_ref[...] = lax.dot_general(u_ref[...], h_ref[...], (((1,), (1,)), ((), ())),
                                   preferred_element_type=F32)
    for a in range(PEER_A_PER_STEP):
        rows = slice(a * PEER_NKEYS, (a + 1) * PEER_NKEYS)
        for c in range(tm // LANES):
            cols = slice(c * LANES, (c + 1) * LANES)
            w = jnp.zeros((PEER_NKEYS, LANES), F32)
            for h in range(PEER_HEADS):
                cand = s2_ref[h, :, cols] + s1_ref[h, a:a + 1, cols]
                w = w + e1_ref[h, a:a + 1, cols] * jnp.where(cand >= th_ref[h, :, cols], e2_ref[h, :, cols], 0.0)
            g_ref[rows, cols] = (w * _gelu_tanh(act_ref[rows, cols])).astype(BF16)
    acc_ref[...] += jnp.dot(vt_ref[...], g_ref[...], preferred_element_type=F32)

    @pl.when(j == pl.num_programs(1) - 1)
    def _():
        o_ref[...] = acc_ref[...].T


def peer_dense(h, s1, s2, e1, e2, th, u, vt, *, tm):
    t, d = h.shape
    ne = PEER_A_PER_STEP * PEER_NKEYS
    tok = pl.BlockSpec((PEER_HEADS, PEER_NKEYS, tm), lambda i, j: (0, 0, i))
    arow = pl.BlockSpec((PEER_HEADS, PEER_A_PER_STEP, tm), lambda i, j: (0, j, i))
    return pl.pallas_call(
        _peer_dense_kernel,
        out_shape=jax.ShapeDtypeStruct((t, d), F32),
        grid=(t // tm, PEER_NKEYS // PEER_A_PER_STEP),
        in_specs=[
            pl.BlockSpec((tm, d), lambda i, j: (i, 0)),
            arow, arow, tok, tok,
            pl.BlockSpec((PEER_HEADS, 1, tm), lambda i, j: (0, 0, i)),
            pl.BlockSpec((ne, d), lambda i, j: (j, 0)),
            pl.BlockSpec((d, ne), lambda i, j: (0, j)),
        ],
        out_specs=pl.BlockSpec((tm, d), lambda i, j: (i, 0)),
        scratch_shapes=[pltpu.VMEM((ne, tm), F32), pltpu.VMEM((ne, tm), BF16), pltpu.VMEM((d, tm), F32)],
        compiler_params=pltpu.CompilerParams(
            dimension_semantics=("parallel", "arbitrary"), vmem_limit_bytes=VMEM_LIMIT_BYTES),
        name="peer_dense",
    )(h, s1, e1, s2, e2, th, u, vt)


def peer_ffn(x, scale, shift, wq, keys, u_bf, vt_bf):
    B, S, D = x.shape
    qv, h = _project(x, scale, shift, wq, with_h=True)
    qv = qv.reshape(B * S, -1)
    tm = 512 if (B * S) % 512 == 0 else B * S
    s1, s2, e1, e2, th = peer_route(qv, keys, tm=min(tm, 256))
    out = peer_dense(h, s1, s2, e1, e2, th, u_bf, vt_bf, tm=tm)
    return out.reshape(B, S, D)


def kernel(x_prompt, x_sample, state_gla, cache_cmp_k, cache_cmp_v, cache_sel_k, cache_sel_v, cache_win_k, cache_win_v, page_table, c_prompt, c_sample, ada_w, ada_b, ln_g, ln_b, gla_w_in, gla_w_a2, gla_b_a2, gla_gn, gla_w_out, nsa_w_in, nsa_pe_k, nsa_pe_v, nsa_w_out, peer_wq, peer_keys, peer_u, peer_v):
    y_p, y_s = x_prompt, x_sample
    gla_p, gla_s, nsa_p, nsa_s = [], [], [], []
    for i in range(DEPTH):
        mp = ada_modulation(c_prompt, ada_w[i], ada_b[i])
        ms = ada_modulation(c_sample, ada_w[i], ada_b[i])
        j = i // N_MIXERS
        if i % N_MIXERS == 0:
            w = (gla_w_in[j], gla_w_a2[j], gla_b_a2[j], gla_gn[j], gla_w_out[j])
            s0 = jnp.zeros((y_p.shape[0], GLA_HEADS, GLA_DK, GLA_DV), F32)
            o_p, st_p = gla_mixer(y_p, mp[1], mp[0], s0, *w)
            o_s, st_s = gla_mixer(y_s, ms[1], ms[0], state_gla[j], *w)
            gla_p.append(st_p.astype(state_gla.dtype))
            gla_s.append(st_s.astype(state_gla.dtype))
        else:
            o_p, rows_p = nsa_prompt(y_p, mp[1], mp[0], nsa_w_in[j], nsa_pe_k[j], nsa_pe_v[j], nsa_w_out[j])
            o_s, rows_s = nsa_sample(y_s, ms[1], ms[0], cache_cmp_k[j], cache_cmp_v[j], cache_sel_k[j],
                                     cache_sel_v[j], cache_win_k[j], cache_win_v[j], page_table,
                                     nsa_w_in[j], nsa_pe_k[j], nsa_pe_v[j], nsa_w_out[j])
            nsa_p.append(rows_p)
            nsa_s.append(rows_s)
        y_p = deepnorm_residual(y_p, o_p, mp[2], ln_g[i, 0], ln_b[i, 0])
        y_s = deepnorm_residual(y_s, o_s, ms[2], ln_g[i, 0], ln_b[i, 0])
        u_bf = peer_u[i].astype(BF16)
        vt_bf = peer_v[i].astype(BF16).T
        y_p = deepnorm_residual(y_p, peer_ffn(y_p, mp[4], mp[3], peer_wq[i], peer_keys[i], u_bf, vt_bf),
                                mp[5], ln_g[i, 1], ln_b[i, 1])
        y_s = deepnorm_residual(y_s, peer_ffn(y_s, ms[4], ms[3], peer_wq[i], peer_keys[i], u_bf, vt_bf),
                                ms[5], ln_g[i, 1], ln_b[i, 1])

    st = lambda ts, k: jnp.stack([t[k] for t in ts])
    return (y_p, y_s, jnp.stack(gla_p), jnp.stack(gla_s),
            st(nsa_p, 0), st(nsa_p, 1), st(nsa_p, 2), st(nsa_p, 3), st(nsa_p, 4), st(nsa_p, 5),
            st(nsa_s, 0), st(nsa_s, 1), st(nsa_s, 2), st(nsa_s, 3), st(nsa_s, 4), st(nsa_s, 5))
```

```python
import functools
import math

import jax
import jax.numpy as jnp
import numpy as np
from jax import lax
from jax.experimental import pallas as pl
from jax.experimental.pallas import tpu as pltpu

D_MODEL = 1024
DEPTH = 2
N_MIXERS = 2
DN_ALPHA = (2.0 * DEPTH) ** 0.25
LN_EPS = 1e-5
F32 = jnp.float32
BF16 = jnp.bfloat16
HIGHEST = lax.Precision.HIGHEST

GLA_HEADS = 4
GLA_DK = D_MODEL // 2 // GLA_HEADS
GLA_DV = D_MODEL // GLA_HEADS
GLA_TAU = 16.0
GLA_CHUNK = 64
GLA_HK = GLA_HEADS * GLA_DK
GLA_HV = GLA_HEADS * GLA_DV

NSA_HEADS = 16
NSA_KV_HEADS = 4
NSA_GROUP = NSA_HEADS // NSA_KV_HEADS
NSA_HD = D_MODEL // NSA_HEADS
NSA_QD = NSA_HEADS * NSA_HD
NSA_KVD = NSA_KV_HEADS * NSA_HD
CMP_LEN = 32
CMP_STRIDE = 16
SEL_BLOCK = 64
N_SEL = 16
WINDOW = 512
Q_BLOCK = 128
FORCE_SCORE = 1e6
ROT_DIM = NSA_HD // 4
ROPE_THETA = 500000.0

PEER_HEADS = 8
PEER_NKEYS = 128
PEER_DKEY = 128
PEER_TOPK = 16

LANES = 128
SUBLANES = 8
VMEM_LIMIT_BYTES = 56 * 1024 * 1024
ROW_TILE = 512

NT = (((1,), (1,)), ((), ()))
TN = (((0,), (0,)), ((), ()))


def _round_up(n, m):
    return -(-n // m) * m


def _params(*sem):
    return pltpu.CompilerParams(dimension_semantics=sem, vmem_limit_bytes=VMEM_LIMIT_BYTES)


def _row_tile(t):
    return ROW_TILE if t % ROW_TILE == 0 else t


def _mod_operand(m, b, s, tm):
    d = m.shape[-1]
    if s % tm == 0:
        return m, pl.BlockSpec((None, 1, d), lambda i, *_: (i * tm // s, 0, 0))
    rows = jnp.broadcast_to(m, (b, s, d)).reshape(b * s, d)
    return rows, pl.BlockSpec((tm, d), lambda i, *_: (i, 0))


def _ada_kernel(c_ref, w_ref, b_ref, o_ref):
    c = c_ref[...]
    act = (c * jax.nn.sigmoid(c)).astype(BF16)
    o_ref[...] = jnp.dot(act, w_ref[...].astype(BF16), preferred_element_type=F32) + b_ref[...]


def ada_mod(c, w, b):
    r, d = c.shape
    nl, _, n = w.shape
    tn = n // 4
    return pl.pallas_call(
        _ada_kernel,
        out_shape=jax.ShapeDtypeStruct((nl, r, n), F32),
        grid=(nl, n // tn),
        in_specs=[pl.BlockSpec((r, d), lambda l, j: (0, 0)),
                  pl.BlockSpec((None, d, tn), lambda l, j: (l, 0, j)),
                  pl.BlockSpec((None, 1, tn), lambda l, j: (l, 0, j))],
        out_specs=pl.BlockSpec((None, r, tn), lambda l, j: (l, 0, j)),
        compiler_params=_params("parallel", "parallel"),
        name="ada_mod",
    )(c, w, b.reshape(nl, 1, n))


def _mod_matmul_kernel(x_ref, sc_ref, sh_ref, w_ref, o_ref, h_ref):
    @pl.when(pl.program_id(1) == 0)
    def _():
        h = x_ref[...] * (1.0 + sc_ref[...]) + sh_ref[...]
        h_ref[...] = h.astype(BF16)

    o_ref[...] = jnp.dot(h_ref[...], w_ref[...], preferred_element_type=F32)


def project(x, scale, shift, w, b, s):
    t, d = x.shape
    npad = _round_up(w.shape[1], LANES)
    wp = jnp.pad(w, ((0, 0), (0, npad - w.shape[1]))).astype(BF16)
    tn = next(c for c in (1024, 768, 640, 512, 384, 256, 128) if npad % c == 0)
    tm = _row_tile(t)
    sc, mod_spec = _mod_operand(scale, b, s, tm)
    sh, _ = _mod_operand(shift, b, s, tm)
    return pl.pallas_call(
        _mod_matmul_kernel,
        out_shape=(jax.ShapeDtypeStruct((t, npad), F32), jax.ShapeDtypeStruct((t, d), BF16)),
        grid=(t // tm, npad // tn),
        in_specs=[pl.BlockSpec((tm, d), lambda i, j: (i, 0)), mod_spec, mod_spec,
                  pl.BlockSpec((d, tn), lambda i, j: (0, j))],
        out_specs=(pl.BlockSpec((tm, tn), lambda i, j: (i, j)), pl.BlockSpec((tm, d), lambda i, j: (i, 0))),
        compiler_params=_params("parallel", "arbitrary"),
        name="mod_matmul",
    )(x, sc, sh, wp)


def _deepnorm(x, sub, gate, g, b):
    y = DN_ALPHA * x + (1.0 + gate) * sub
    mu = jnp.mean(y, axis=-1, keepdims=True)
    var = jnp.mean(jnp.square(y - mu), axis=-1, keepdims=True)
    return (y - mu) * lax.rsqrt(var + LN_EPS) * g + b


def _post_gla_kernel(o_ref, r_ref, x_ref, gate_ref, gn_ref, w_ref, g_ref, b_ref, y_ref):
    parts = []
    for h in range(GLA_HEADS):
        cols = slice(h * GLA_DV, (h + 1) * GLA_DV)
        o = o_ref[:, cols]
        mu = jnp.mean(o, axis=-1, keepdims=True)
        var = jnp.mean(jnp.square(o - mu), axis=-1, keepdims=True)
        parts.append((o - mu) * lax.rsqrt(var + LN_EPS) * gn_ref[:, cols])
    r = r_ref[...]
    f = (jnp.concatenate(parts, axis=1) * (r * jax.nn.sigmoid(r))).astype(BF16)
    sub = jnp.dot(f, w_ref[...], preferred_element_type=F32)
    y_ref[...] = _deepnorm(x_ref[...], sub, gate_ref[...], g_ref[...], b_ref[...])


def _post_matmul_kernel(o_ref, x_ref, gate_ref, w_ref, g_ref, b_ref, y_ref):
    sub = jnp.dot(o_ref[...].astype(BF16), w_ref[...], preferred_element_type=F32)
    y_ref[...] = _deepnorm(x_ref[...], sub, gate_ref[...], g_ref[...], b_ref[...])


def _post_plain_kernel(o_ref, x_ref, gate_ref, g_ref, b_ref, y_ref):
    y_ref[...] = _deepnorm(x_ref[...], o_ref[...], gate_ref[...], g_ref[...], b_ref[...])


def post(x, gate, ln_g, ln_b, b, s, *, sub=None, w_out=None, gla_proj=None, gla_gn=None):
    t, d = x.shape
    tm = _row_tile(t)
    gate_arr, gate_spec = _mod_operand(gate, b, s, tm)
    row = pl.BlockSpec((tm, d), lambda i: (i, 0))
    vec = pl.BlockSpec((1, d), lambda i: (0, 0))
    mat = pl.BlockSpec((d, d), lambda i: (0, 0))
    g2, b2 = ln_g.reshape(1, d), ln_b.reshape(1, d)
    if gla_proj is not None:
        r_spec = pl.BlockSpec((tm, GLA_HV), lambda i: (i, (2 * GLA_HK + GLA_HV) // GLA_HV))
        args = (sub, gla_proj, x, gate_arr, gla_gn.reshape(1, d), w_out.astype(BF16), g2, b2)
        specs = [row, r_spec, row, gate_spec, vec, mat, vec, vec]
        body = _post_gla_kernel
    elif w_out is not None:
        args = (sub, x, gate_arr, w_out.astype(BF16), g2, b2)
        specs = [row, row, gate_spec, mat, vec, vec]
        body = _post_matmul_kernel
    else:
        args = (sub, x, gate_arr, g2, b2)
        specs = [row, row, gate_spec, vec, vec]
        body = _post_plain_kernel
    return pl.pallas_call(
        body,
        out_shape=jax.ShapeDtypeStruct((t, d), F32),
        grid=(t // tm,),
        in_specs=specs,
        out_specs=row,
        compiler_params=_params("parallel"),
        name="post",
    )(*args)


GLA_MIN_ROWS = 64


def _cumsum_rows(x):
    n = x.shape[0]
    row = lax.broadcasted_iota(jnp.int32, (n, 1), 0)
    shift = 1
    while shift < n:
        x = x + jnp.where(row >= shift, pltpu.roll(x, shift, 0), 0.0)
        shift *= 2
    return x


def _gla_kernel(has_s0, q_ref, k_ref, v_ref, a_ref, wa_ref, ba_ref, *rest):
    if has_s0:
        s0_ref, o_ref, st_ref, s_scr = rest
    else:
        o_ref, st_ref, s_scr = rest
    c = pl.program_id(1)
    n_rows = q_ref.shape[0]
    rows = max(n_rows, GLA_MIN_ROWS)

    @pl.when(c == 0)
    def _():
        s_scr[...] = s0_ref[...] if has_s0 else jnp.zeros_like(s_scr)

    def padded(x):
        if rows == n_rows:
            return x
        return jnp.concatenate([x, jnp.zeros((rows - n_rows, x.shape[1]), x.dtype)], axis=0)

    z = jnp.dot(a_ref[...], wa_ref[...], precision=HIGHEST, preferred_element_type=F32) + ba_ref[...]
    logg = (jnp.minimum(z, 0.0) - jnp.log1p(jnp.exp(-jnp.abs(z)))) / GLA_TAU
    b = _cumsum_rows(padded(logg))
    q, k, v = padded(q_ref[...]), padded(k_ref[...]), padded(v_ref[...])
    qe = q * (GLA_DK ** -0.5) * jnp.exp(b)
    ke = k * jnp.exp(-b)
    b_last = b[rows - 1:rows, :]
    kd = k * jnp.exp(b_last - b)
    e_last = jnp.exp(b_last)
    causal = lax.broadcasted_iota(jnp.int32, (rows, 1), 0) >= lax.broadcasted_iota(jnp.int32, (1, rows), 1)
    eye = lax.broadcasted_iota(jnp.int32, (GLA_DK, 1), 0) == lax.broadcasted_iota(jnp.int32, (1, GLA_DK), 1)
    for h in range(GLA_HEADS):
        ck = slice(h * GLA_DK, (h + 1) * GLA_DK)
        cv = slice(h * GLA_DV, (h + 1) * GLA_DV)
        att = lax.dot_general(qe[:, ck], ke[:, ck], NT, precision=HIGHEST, preferred_element_type=F32)
        att = jnp.where(causal, att, 0.0)
        s_h = s_scr[h]
        o = (jnp.dot(att, v[:, cv], precision=HIGHEST, preferred_element_type=F32)
             + jnp.dot(qe[:, ck], s_h, precision=HIGHEST, preferred_element_type=F32))
        o_ref[:, cv] = o[:n_rows]
        e_col = jnp.sum(jnp.where(eye, e_last[:, ck], 0.0), axis=1, keepdims=True)
        s_scr[h] = e_col * s_h + lax.dot_general(kd[:, ck], v[:, cv], TN, precision=HIGHEST,
                                                 preferred_element_type=F32)

    @pl.when(c == pl.num_programs(1) - 1)
    def _():
        st_ref[...] = s_scr[...]


def gla_recurrence(proj, w_a2, b_a2, s0, b, s):
    chunk = GLA_CHUNK if s % GLA_CHUNK == 0 else s
    nc = s // chunk
    wa = jnp.pad(w_a2, ((0, LANES - w_a2.shape[0]), (0, 0)))
    a_block = (2 * GLA_HK + 2 * GLA_HV) // LANES
    row = lambda width, blk: pl.BlockSpec((chunk, width), lambda bi, c: (bi * nc + c, blk))
    st_spec = pl.BlockSpec((None, GLA_HEADS, GLA_DK, GLA_DV), lambda bi, c: (bi, 0, 0, 0))
    in_specs = [row(GLA_HK, 0), row(GLA_HK, 1), row(GLA_HV, 2 * GLA_HK // GLA_HV), row(LANES, a_block),
                pl.BlockSpec(wa.shape, lambda bi, c: (0, 0)), pl.BlockSpec((1, GLA_HK), lambda bi, c: (0, 0))]
    args = [proj, proj, proj, proj, wa, b_a2.reshape(1, GLA_HK)]
    if s0 is not None:
        in_specs.append(st_spec)
        args.append(s0)
    return pl.pallas_call(
        functools.partial(_gla_kernel, s0 is not None),
        out_shape=(jax.ShapeDtypeStruct((b * s, GLA_HV), F32),
                   jax.ShapeDtypeStruct((b, GLA_HEADS, GLA_DK, GLA_DV), F32)),
        grid=(b, nc),
        in_specs=in_specs,
        out_specs=(pl.BlockSpec((chunk, GLA_HV), lambda bi, c: (bi * nc + c, 0)), st_spec),
        scratch_shapes=[pltpu.VMEM((GLA_HEADS, GLA_DK, GLA_DV), F32)],
        compiler_params=_params("parallel", "arbitrary"),
        name="gla_recurrence",
    )(*args)


def _nsa_q_perm():
    cols = []
    for m in range(NSA_KV_HEADS // 2):
        for r in range(NSA_GROUP):
            for half in range(2):
                head = (2 * m + half) * NSA_GROUP + r
                cols.append(np.arange(NSA_HD) + head * NSA_HD)
    return np.concatenate(cols)


def _nsa_gate_expand():
    perm = _nsa_q_perm()
    ex = np.zeros((LANES, 3 * NSA_QD), np.float32)
    for br in range(3):
        for col in range(NSA_QD):
            ex[br * NSA_HEADS + perm[col] // NSA_HD, br * NSA_QD + col] = 1.0
    return ex


def _nsa_slc_matrix(nc_pad, nsb_pad):
    ratio = SEL_BLOCK // CMP_STRIDE
    m = np.zeros((nc_pad, nsb_pad), np.float32)
    for j in range(nsb_pad):
        for o in range(CMP_LEN // CMP_STRIDE):
            for i in range(ratio):
                n = ratio * j + i + o - (CMP_LEN // CMP_STRIDE - 1)
                if 0 <= n < nc_pad:
                    m[n, j] += 1.0
    return m


def _rope_tables(pos):
    half = ROT_DIM // 2
    inv = ROPE_THETA ** (-jnp.arange(half, dtype=F32) * 2.0 / ROT_DIM)
    ang = pos.astype(F32)[:, None] * inv[None, :]
    cos, sin = jnp.cos(ang), jnp.sin(ang)
    t = pos.shape[0]
    ones = jnp.ones((t, NSA_HD - ROT_DIM), F32)
    zeros = jnp.zeros((t, NSA_HD - ROT_DIM), F32)
    z8 = jnp.zeros((t, half), F32)
    c = jnp.concatenate([cos, cos, ones], 1)
    up = jnp.concatenate([-sin, z8, zeros], 1)
    dn = jnp.concatenate([z8, sin, zeros], 1)
    two = lambda a: jnp.concatenate([a, a], 1)
    return two(c), two(up), two(dn)


def _nsa_prep_kernel(p_ref, cos_ref, up_ref, dn_ref, q_ref, kc_ref, vc_ref, ks_ref, vs_ref, kw_ref, vw_ref,
                     ksb_ref, vsb_ref, kwb_ref, vwb_ref, g_ref):
    def rope(x):
        reps = x.shape[1] // LANES
        tile = lambda a: jnp.concatenate([a] * reps, axis=1)
        w = x.shape[1]
        return (x * tile(cos_ref[...]) + pltpu.roll(x, w - ROT_DIM // 2, 1) * tile(up_ref[...])
                + pltpu.roll(x, ROT_DIM // 2, 1) * tile(dn_ref[...]))

    q_ref[...] = (rope(p_ref[:, :NSA_QD]) * (NSA_HD ** -0.5)).astype(q_ref.dtype)
    kv = lambda k: p_ref[:, NSA_QD + k * NSA_KVD:NSA_QD + (k + 1) * NSA_KVD]
    kc_ref[...] = rope(kv(0))
    vc_ref[...] = kv(1)
    ks = rope(kv(2))
    ks_ref[...] = ks
    ksb_ref[...] = ks.astype(BF16)
    vs_ref[...] = kv(3)
    vsb_ref[...] = kv(3).astype(BF16)
    kw = rope(kv(4))
    kw_ref[...] = kw
    kwb_ref[...] = kw.astype(BF16)
    vw_ref[...] = kv(5)
    vwb_ref[...] = kv(5).astype(BF16)
    g0 = NSA_QD + 6 * NSA_KVD
    g_ref[...] = jax.nn.sigmoid(p_ref[:, g0:g0 + LANES])


def nsa_prep(p, cos, up, dn, *, q_dtype):
    t = p.shape[0]
    tm = _row_tile(t)
    period = cos.shape[0] // tm
    tab = pl.BlockSpec((tm, LANES), lambda i: (i % period, 0))
    kv32 = jax.ShapeDtypeStruct((t, NSA_KVD), F32)
    kv16 = jax.ShapeDtypeStruct((t, NSA_KVD), BF16)
    kvs = pl.BlockSpec((tm, NSA_KVD), lambda i: (i, 0))
    return pl.pallas_call(
        _nsa_prep_kernel,
        out_shape=(jax.ShapeDtypeStruct((t, NSA_QD), q_dtype),) + (kv32,) * 6 + (kv16,) * 4
        + (jax.ShapeDtypeStruct((t, LANES), F32),),
        grid=(t // tm,),
        in_specs=[pl.BlockSpec((tm, p.shape[1]), lambda i: (i, 0)), tab, tab, tab],
        out_specs=(pl.BlockSpec((tm, NSA_QD), lambda i: (i, 0)),) + (kvs,) * 10
        + (pl.BlockSpec((tm, LANES), lambda i: (i, 0)),),
        compiler_params=_params("parallel"),
        name="nsa_prep",
    )(p, cos, up, dn)


def _block_sums(x, w):
    n_sub = x.shape[0] // CMP_STRIDE
    x = x.reshape(n_sub, CMP_STRIDE, NSA_KVD)
    w = 1.0 + w
    return jnp.sum(x * w[None, :CMP_STRIDE], axis=1), jnp.sum(x * w[None, CMP_STRIDE:], axis=1)


def _nsa_compress_kernel(x_ref, w_ref, o_ref):
    first, second = _block_sums(x_ref[...], w_ref[...])
    n_sub = first.shape[0]
    nxt = pltpu.roll(second, n_sub - 1, 0)
    row = lax.broadcasted_iota(jnp.int32, (n_sub, 1), 0)
    o_ref[...] = jnp.where(row < n_sub - 1, (first + nxt) / CMP_LEN, 0.0).astype(o_ref.dtype)


def nsa_compress(x, pe, b, s):
    n_sub = s // CMP_STRIDE
    return pl.pallas_call(
        _nsa_compress_kernel,
        out_shape=jax.ShapeDtypeStruct((b, n_sub, NSA_KVD), BF16),
        grid=(b,),
        in_specs=[pl.BlockSpec((s, NSA_KVD), lambda i: (i, 0)),
                  pl.BlockSpec((CMP_LEN, NSA_KVD), lambda i: (0, 0))],
        out_specs=pl.BlockSpec((None, n_sub, NSA_KVD), lambda i: (i, 0, 0)),
        compiler_params=_params("parallel"),
        name="nsa_compress",
    )(x, pe.reshape(CMP_LEN, NSA_KVD))


NSA_KV_TILE = 512


def _softmax_rows(s, allowed):
    s = jnp.where(allowed, s, -jnp.inf)
    m = jnp.max(s, axis=1, keepdims=True)
    m = jnp.where(m == -jnp.inf, 0.0, m)
    e = jnp.exp(s - m)
    return e / jnp.maximum(jnp.sum(e, axis=1, keepdims=True), 1e-30)


def _select_blocks(score, n_sel):
    nb = score.shape[1]
    lane = lax.broadcasted_iota(jnp.int32, score.shape, 1).astype(F32)
    sel = jnp.zeros(score.shape, F32)
    for _ in range(n_sel):
        m = jnp.max(score, axis=1, keepdims=True)
        first = jnp.min(jnp.where(score == m, lane, float(nb)), axis=1, keepdims=True)
        pick = lane == first
        sel = jnp.where(pick, 1.0, sel)
        score = jnp.where(pick, -jnp.inf, score)
    return sel


def _block_scores(p_slc, pos_q):
    jb = lax.broadcasted_iota(jnp.int32, (1, p_slc.shape[1]), 1)
    cur = pos_q // SEL_BLOCK
    valid = jb * SEL_BLOCK <= pos_q
    forced = (jb == 0) | (jb == cur) | (jb == cur - 1)
    return jnp.where(forced, FORCE_SCORE, jnp.where(valid, p_slc, -1.0))


def _online_softmax_step(s, allowed, v, m_old, l_old, acc_old):
    s = jnp.where(allowed, s, -jnp.inf)
    m_new = jnp.maximum(m_old, jnp.max(s, axis=1, keepdims=True))
    m_use = jnp.where(m_new == -jnp.inf, 0.0, m_new)
    alpha = jnp.exp(m_old - m_use)
    p = jnp.exp(s - m_use)
    l_new = alpha * l_old + jnp.sum(p, axis=1, keepdims=True)
    acc_new = alpha * acc_old + jnp.dot(p.astype(BF16), v, preferred_element_type=F32)
    return m_new, l_new, acc_new


def _padded_query(tile, m, in_half):
    sel = jnp.where(in_half, tile, jnp.zeros_like(tile))
    z = jnp.zeros_like(sel)
    return jnp.concatenate([sel, z] if m == 0 else [z, sel], axis=1)


def _nsa_attn_kernel(q_ref, g_ref, kcmp_ref, vcmp_ref, ks_ref, vs_ref, kw_ref, vw_ref, slc_ref, ex_ref,
                     o_ref, acc_ref):
    i = pl.program_id(1)
    start = i * Q_BLOCK
    pos_q = start + lax.broadcasted_iota(jnp.int32, (Q_BLOCK, 1), 0)
    lane128 = lax.broadcasted_iota(jnp.int32, (1, LANES), 1)
    n_cmp = kcmp_ref.shape[0]
    cmp_end = CMP_STRIDE * lax.broadcasted_iota(jnp.int32, (1, n_cmp), 1) + (CMP_LEN - 1)
    cmp_ok = cmp_end <= pos_q
    gexp = jnp.dot(g_ref[...], ex_ref[...], precision=HIGHEST, preferred_element_type=F32)
    o_ref[...] = jnp.zeros_like(o_ref)

    win_base = pl.multiple_of(jnp.maximum(start - WINDOW, 0), Q_BLOCK)
    band = WINDOW + Q_BLOCK
    kw_t = kw_ref[pl.ds(win_base, band), :]
    vw_t = vw_ref[pl.ds(win_base, band), :]
    dpos = pos_q - (win_base + lax.broadcasted_iota(jnp.int32, (1, band), 1))
    win_ok = (dpos >= 0) & (dpos < WINDOW)

    for g in range(NSA_KV_HEADS):
        m, half = divmod(g, 2)
        in_half = (lane128 // NSA_HD) == half
        mcols = slice(m * LANES, (m + 1) * LANES)
        qz = [_padded_query(q_ref[:, (m * NSA_GROUP + r) * LANES:(m * NSA_GROUP + r + 1) * LANES], m, in_half)
              for r in range(NSA_GROUP)]

        imp = jnp.zeros((Q_BLOCK, n_cmp), F32)
        o_c = []
        for r in range(NSA_GROUP):
            s = lax.dot_general(qz[r], kcmp_ref[...], NT, preferred_element_type=F32)
            p = _softmax_rows(s, cmp_ok)
            imp = imp + p
            o_c.append(jnp.dot(p.astype(BF16), vcmp_ref[...], preferred_element_type=F32)[:, mcols])
        p_slc = jnp.dot(imp, slc_ref[...], precision=HIGHEST, preferred_element_type=F32)
        sel_bf = _select_blocks(_block_scores(p_slc, pos_q), N_SEL).astype(BF16)

        acc_ref[...] = jnp.zeros_like(acc_ref)

        def kv_step(t, carry):
            k0 = pl.multiple_of(t * NSA_KV_TILE, NSA_KV_TILE)
            k_t = ks_ref[pl.ds(k0, NSA_KV_TILE), :]
            v_t = vs_ref[pl.ds(k0, NSA_KV_TILE), :]
            key = k0 + lax.broadcasted_iota(jnp.int32, (1, NSA_KV_TILE), 1)
            blk = lax.broadcasted_iota(jnp.int32, (LANES, 1), 0)
            expand = jnp.where(blk == key // SEL_BLOCK, 1.0, 0.0).astype(BF16)
            chosen = jnp.dot(sel_bf, expand, preferred_element_type=F32)
            allowed = (chosen > 0.5) & (key <= pos_q)
            out = []
            for r in range(NSA_GROUP):
                s = lax.dot_general(qz[r], k_t, NT, preferred_element_type=F32)
                m_new, l_new, acc_ref[r] = _online_softmax_step(s, allowed, v_t, carry[2 * r], carry[2 * r + 1],
                                                                acc_ref[r])
                out += [m_new, l_new]
            return tuple(out)

        init = (jnp.full((Q_BLOCK, 1), -jnp.inf, F32), jnp.zeros((Q_BLOCK, 1), F32)) * NSA_GROUP
        n_tiles = (start + Q_BLOCK + NSA_KV_TILE - 1) // NSA_KV_TILE
        fin = lax.fori_loop(0, n_tiles, kv_step, init)

        for r in range(NSA_GROUP):
            t = m * NSA_GROUP + r
            cols = slice(t * LANES, (t + 1) * LANES)
            o_s = acc_ref[r][:, mcols] / jnp.maximum(fin[2 * r + 1], 1e-30)
            s = lax.dot_general(qz[r], kw_t, NT, preferred_element_type=F32)
            p = _softmax_rows(s, win_ok)
            o_w = jnp.dot(p.astype(BF16), vw_t, preferred_element_type=F32)[:, mcols]
            comb = (gexp[:, cols] * o_c[r] + gexp[:, NSA_QD + t * LANES:NSA_QD + (t + 1) * LANES] * o_s
                    + gexp[:, 2 * NSA_QD + t * LANES:2 * NSA_QD + (t + 1) * LANES] * o_w)
            o_ref[:, cols] += jnp.where(in_half, comb, 0.0)


def nsa_attn_prompt(q, gates, kcmp, vcmp, ks, vs, kw, vw, b, s):
    nq = s // Q_BLOCK
    slc = jnp.asarray(_nsa_slc_matrix(s // CMP_STRIDE, LANES))
    ex = jnp.asarray(_nsa_gate_expand())
    seq = pl.BlockSpec((s, NSA_KVD), lambda bi, i: (bi, 0))
    cmp_spec = pl.BlockSpec((None, s // CMP_STRIDE, NSA_KVD), lambda bi, i: (bi, 0, 0))
    return pl.pallas_call(
        _nsa_attn_kernel,
        out_shape=jax.ShapeDtypeStruct((b * s, NSA_QD), F32),
        grid=(b, nq),
        in_specs=[
            pl.BlockSpec((Q_BLOCK, NSA_QD), lambda bi, i: (bi * nq + i, 0)),
            pl.BlockSpec((Q_BLOCK, LANES), lambda bi, i: (bi * nq + i, 0)),
            cmp_spec, cmp_spec, seq, seq, seq, seq,
            pl.BlockSpec(slc.shape, lambda bi, i: (0, 0)),
            pl.BlockSpec(ex.shape, lambda bi, i: (0, 0)),
        ],
        out_specs=pl.BlockSpec((Q_BLOCK, NSA_QD), lambda bi, i: (bi * nq + i, 0)),
        scratch_shapes=[pltpu.VMEM((NSA_GROUP, Q_BLOCK, NSA_KVD), F32)],
        compiler_params=_params("parallel", "arbitrary"),
        name="nsa_attn_prompt",
    )(q, gates, kcmp, vcmp, ks, vs, kw, vw, slc, ex)


PAGES_PER_STEP = 8


def _page_specs(n, page_rows):
    def one(u):
        return pl.BlockSpec((None, page_rows, NSA_KVD),
                            lambda bi, j, pt: (pt[bi, j * PAGES_PER_STEP + u], 0, 0))
    return [one(u) for u in range(n)]


def _nsa_page_sums_kernel(pt_ref, *refs):
    n = PAGES_PER_STEP
    kp, vp = refs[:n], refs[n:2 * n]
    wk_ref, wv_ref, fk_ref, sk_ref, fv_ref, sv_ref = refs[2 * n:]
    per_page = kp[0].shape[0] // CMP_STRIDE
    for u in range(n):
        rows = slice(u * per_page, (u + 1) * per_page)
        fk_ref[rows, :], sk_ref[rows, :] = _block_sums(kp[u][...], wk_ref[...])
        fv_ref[rows, :], sv_ref[rows, :] = _block_sums(vp[u][...], wv_ref[...])


def nsa_page_sums(page_table, pool_k, pool_v, pe_k, pe_v):
    db, n_pages = page_table.shape
    page = pool_k.shape[1]
    per_step = PAGES_PER_STEP * page // CMP_STRIDE
    out = jax.ShapeDtypeStruct((db, n_pages * page // CMP_STRIDE, NSA_KVD), F32)
    out_spec = pl.BlockSpec((None, per_step, NSA_KVD), lambda bi, j, pt: (bi, j, 0))
    pe_spec = pl.BlockSpec((CMP_LEN, NSA_KVD), lambda bi, j, pt: (0, 0))
    flat = lambda pool: pool.reshape(pool.shape[0], page, NSA_KVD)
    return pl.pallas_call(
        _nsa_page_sums_kernel,
        out_shape=(out,) * 4,
        grid_spec=pltpu.PrefetchScalarGridSpec(
            num_scalar_prefetch=1,
            grid=(db, n_pages // PAGES_PER_STEP),
            in_specs=_page_specs(PAGES_PER_STEP, page) * 2 + [pe_spec, pe_spec],
            out_specs=(out_spec,) * 4,
        ),
        compiler_params=_params("parallel", "arbitrary"),
        name="nsa_page_sums",
    )(page_table, *([flat(pool_k)] * PAGES_PER_STEP), *([flat(pool_v)] * PAGES_PER_STEP),
      pe_k.reshape(CMP_LEN, NSA_KVD), pe_v.reshape(CMP_LEN, NSA_KVD))


def _decode_queries(q_ref, g):
    m, half = divmod(g, 2)
    in_half = (lax.broadcasted_iota(jnp.int32, (1, LANES), 1) // NSA_HD) == half
    tiles = [_padded_query(q_ref[:, (m * NSA_GROUP + r) * LANES:(m * NSA_GROUP + r + 1) * LANES], m, in_half)
             for r in range(NSA_GROUP)]
    return jnp.concatenate(tiles, axis=0).astype(BF16)


def _group_rows(x):
    return jnp.concatenate([x] * NSA_GROUP, axis=0)


def _nsa_decode_select_kernel(past, q_ref, fk_ref, sk_ref, fv_ref, sv_ref, kn_ref, vn_ref, wk_ref, wv_ref,
                              slc_ref, sel_ref, oc_ref):
    nq = q_ref.shape[0]
    n_cmp = fk_ref.shape[0]
    row = lax.broadcasted_iota(jnp.int32, (n_cmp, 1), 0)

    def summaries(f_ref, s_ref, new_ref, w_ref):
        w = 1.0 + w_ref[CMP_STRIDE:CMP_STRIDE + nq, :]
        second_new = jnp.sum(new_ref[...] * w, axis=0, keepdims=True)
        nxt = jnp.where(row == n_cmp - 1, second_new, pltpu.roll(s_ref[...], n_cmp - 1, 0))
        return ((f_ref[...] + nxt) / CMP_LEN).astype(BF16)

    kcmp = summaries(fk_ref, sk_ref, kn_ref, wk_ref)
    vcmp = summaries(fv_ref, sv_ref, vn_ref, wv_ref)
    pos_q = past + lax.broadcasted_iota(jnp.int32, (nq, 1), 0)
    cmp_end = CMP_STRIDE * lax.broadcasted_iota(jnp.int32, (1, n_cmp), 1) + (CMP_LEN - 1)
    cmp_ok = _group_rows(cmp_end <= pos_q)
    lane128 = lax.broadcasted_iota(jnp.int32, (1, LANES), 1)
    oc_ref[...] = jnp.zeros_like(oc_ref)
    for g in range(NSA_KV_HEADS):
        m, half = divmod(g, 2)
        in_half = (lane128 // NSA_HD) == half
        s = lax.dot_general(_decode_queries(q_ref, g), kcmp, NT, preferred_element_type=F32)
        p = _softmax_rows(s, cmp_ok)
        o_c = jnp.dot(p.astype(BF16), vcmp, preferred_element_type=F32)[:, m * LANES:(m + 1) * LANES]
        imp = p[0:nq]
        for r in range(1, NSA_GROUP):
            imp = imp + p[r * nq:(r + 1) * nq]
        for r in range(NSA_GROUP):
            cols = slice((m * NSA_GROUP + r) * LANES, (m * NSA_GROUP + r + 1) * LANES)
            oc_ref[:, cols] += jnp.where(in_half, o_c[r * nq:(r + 1) * nq], 0.0)
        p_slc = jnp.dot(imp, slc_ref[...], precision=HIGHEST, preferred_element_type=F32)
        sel_ref[g] = _select_blocks(_block_scores(p_slc, pos_q), N_SEL)


def nsa_decode_select(q, sums, kc_new, vc_new, pe_k, pe_v, db, nq, past):
    n_cmp = sums[0].shape[1]
    nsb_pad = _round_up(-(-(past + nq) // SEL_BLOCK), LANES)
    slc = jnp.asarray(_nsa_slc_matrix(n_cmp, nsb_pad))
    cmp_spec = pl.BlockSpec((None, n_cmp, NSA_KVD), lambda bi: (bi, 0, 0))
    new_spec = pl.BlockSpec((nq, NSA_KVD), lambda bi: (bi, 0))
    pe_spec = pl.BlockSpec((CMP_LEN, NSA_KVD), lambda bi: (0, 0))
    return pl.pallas_call(
        functools.partial(_nsa_decode_select_kernel, past),
        out_shape=(jax.ShapeDtypeStruct((db, NSA_KV_HEADS, nq, nsb_pad), F32),
                   jax.ShapeDtypeStruct((db * nq, NSA_QD), F32)),
        grid=(db,),
        in_specs=[pl.BlockSpec((nq, NSA_QD), lambda bi: (bi, 0)), cmp_spec, cmp_spec, cmp_spec, cmp_spec,
                  new_spec, new_spec, pe_spec, pe_spec, pl.BlockSpec(slc.shape, lambda bi: (0, 0))],
        out_specs=(pl.BlockSpec((None, NSA_KV_HEADS, nq, nsb_pad), lambda bi: (bi, 0, 0, 0)),
                   pl.BlockSpec((nq, NSA_QD), lambda bi: (bi, 0))),
        compiler_params=_params("parallel"),
        name="nsa_decode_select",
    )(q, *sums, kc_new, vc_new, pe_k.reshape(CMP_LEN, NSA_KVD), pe_v.reshape(CMP_LEN, NSA_KVD), slc)


def _nsa_decode_attend_kernel(past, pt_ref, q_ref, sel_ref, g_ref, oc_ref, ksn_ref, vsn_ref, wk_ref, wv_ref,
                              kwn_ref, vwn_ref, ex_ref, *refs):
    n = PAGES_PER_STEP
    kp, vp = refs[:n], refs[n:2 * n]
    o_ref, m_scr, l_scr, acc_scr = refs[2 * n:]
    j = pl.program_id(1)
    nq = q_ref.shape[0]
    page = kp[0].shape[0]
    n_keys = n * page
    pos_q = _group_rows(past + lax.broadcasted_iota(jnp.int32, (nq, 1), 0))
    nsb_pad = sel_ref.shape[-1]
    blk = lax.broadcasted_iota(jnp.int32, (nsb_pad, 1), 0)

    @pl.when(j == 0)
    def _():
        m_scr[...] = jnp.full_like(m_scr, -jnp.inf)
        l_scr[...] = jnp.zeros_like(l_scr)
        acc_scr[...] = jnp.zeros_like(acc_scr)

    def attend(g, k, v, allowed):
        s = lax.dot_general(_decode_queries(q_ref, g), k, NT, preferred_element_type=F32)
        m_scr[g], l_scr[g], acc_scr[g] = _online_softmax_step(s, allowed, v, m_scr[g], l_scr[g], acc_scr[g])

    k_t = jnp.concatenate([r[...].astype(BF16) for r in kp], axis=0)
    v_t = jnp.concatenate([r[...].astype(BF16) for r in vp], axis=0)
    key = j * n_keys + lax.broadcasted_iota(jnp.int32, (1, n_keys), 1)
    expand = jnp.where(blk == key // SEL_BLOCK, 1.0, 0.0).astype(BF16)
    for g in range(NSA_KV_HEADS):
        chosen = jnp.dot(sel_ref[g].astype(BF16), expand, preferred_element_type=F32)
        attend(g, k_t, v_t, _group_rows(chosen > 0.5) & (key <= pos_q))

    @pl.when(j == pl.num_programs(1) - 1)
    def _():
        lane128 = lax.broadcasted_iota(jnp.int32, (1, LANES), 1)
        n_new = ksn_ref.shape[0]
        new_idx = lax.broadcasted_iota(jnp.int32, (1, n_new), 1)
        new_key = past + new_idx
        new_blk = jnp.where(blk == new_key // SEL_BLOCK, 1.0, 0.0).astype(BF16)
        n_win = wk_ref.shape[0]
        k_w = jnp.concatenate([wk_ref[...].astype(BF16), kwn_ref[...]], axis=0)
        v_w = jnp.concatenate([wv_ref[...].astype(BF16), vwn_ref[...]], axis=0)
        win_idx = lax.broadcasted_iota(jnp.int32, (1, n_win + n_new), 1)
        pos_w = jnp.where(win_idx < n_win, past - n_win + win_idx, past + win_idx - n_win)
        dpos = pos_q - pos_w
        win_ok = (dpos >= 0) & (dpos < WINDOW) & (pos_w >= 0) & (win_idx < n_win + nq)
        gexp = jnp.dot(g_ref[...], ex_ref[...], precision=HIGHEST, preferred_element_type=F32)
        o_ref[...] = gexp[:, :NSA_QD] * oc_ref[...]
        for g in range(NSA_KV_HEADS):
            m, half = divmod(g, 2)
            in_half = (lane128 // NSA_HD) == half
            mcols = slice(m * LANES, (m + 1) * LANES)
            chosen = jnp.dot(sel_ref[g].astype(BF16), new_blk, preferred_element_type=F32)
            attend(g, ksn_ref[...], vsn_ref[...],
                   _group_rows(chosen > 0.5) & (new_key <= pos_q) & (new_idx < nq))
            o_s = acc_scr[g][:, mcols] / jnp.maximum(l_scr[g], 1e-30)
            s = lax.dot_general(_decode_queries(q_ref, g), k_w, NT, preferred_element_type=F32)
            o_w = jnp.dot(_softmax_rows(s, win_ok).astype(BF16), v_w, preferred_element_type=F32)[:, mcols]
            for r in range(NSA_GROUP):
                t = m * NSA_GROUP + r
                cols = slice(t * LANES, (t + 1) * LANES)
                rows = slice(r * nq, (r + 1) * nq)
                comb = (gexp[:, NSA_QD + t * LANES:NSA_QD + (t + 1) * LANES] * o_s[rows]
                        + gexp[:, 2 * NSA_QD + t * LANES:2 * NSA_QD + (t + 1) * LANES] * o_w[rows])
                o_ref[:, cols] += jnp.where(in_half, comb, 0.0)


def nsa_decode_attend(page_table, q, sel, gates, o_c, ks_new, vs_new, wk, wv, kw_new, vw_new, pool_k, pool_v,
                      db, nq, past):
    n_pages = page_table.shape[1]
    page = pool_k.shape[1]
    ex = jnp.asarray(_nsa_gate_expand())
    n_win = wk.shape[1]
    rows = NSA_GROUP * nq
    per_b = lambda shape: pl.BlockSpec((None,) + shape, lambda bi, j, pt: (bi,) + (0,) * len(shape))
    q_rows = lambda width: pl.BlockSpec((nq, width), lambda bi, j, pt: (bi, 0))
    flat = lambda pool: pool.reshape(pool.shape[0], page, NSA_KVD)
    new_rows = ks_new.shape[1]
    return pl.pallas_call(
        functools.partial(_nsa_decode_attend_kernel, past),
        out_shape=jax.ShapeDtypeStruct((db * nq, NSA_QD), F32),
        grid_spec=pltpu.PrefetchScalarGridSpec(
            num_scalar_prefetch=1,
            grid=(db, n_pages // PAGES_PER_STEP),
            in_specs=[q_rows(NSA_QD), per_b(sel.shape[1:]), q_rows(LANES), q_rows(NSA_QD),
                      per_b((new_rows, NSA_KVD)), per_b((new_rows, NSA_KVD)),
                      per_b((n_win, NSA_KVD)), per_b((n_win, NSA_KVD)),
                      per_b((new_rows, NSA_KVD)), per_b((new_rows, NSA_KVD)),
                      pl.BlockSpec(ex.shape, lambda bi, j, pt: (0, 0))]
            + _page_specs(PAGES_PER_STEP, page) * 2,
            out_specs=q_rows(NSA_QD),
            scratch_shapes=[pltpu.VMEM((NSA_KV_HEADS, rows, 1), F32), pltpu.VMEM((NSA_KV_HEADS, rows, 1), F32),
                            pltpu.VMEM((NSA_KV_HEADS, rows, NSA_KVD), F32)],
        ),
        compiler_params=_params("parallel", "arbitrary"),
        name="nsa_decode_attend",
    )(page_table, q, sel, gates, o_c, ks_new, vs_new, wk, wv, kw_new, vw_new, ex,
      *([flat(pool_k)] * PAGES_PER_STEP), *([flat(pool_v)] * PAGES_PER_STEP))


def _nsa_weights(w_in, w_out):
    perm = _nsa_q_perm()
    return jnp.concatenate([w_in[:, perm], w_in[:, NSA_QD:]], axis=1), w_out[perm]


def nsa_prompt(x, scale, shift, w_in_p, pe_k, pe_v, b, s):
    p, _ = project(x, scale, shift, w_in_p, b, s)
    cos, up, dn = _rope_tables(jnp.arange(s, dtype=jnp.int32))
    q, kc, vc, ks, vs, kw, vw, ksb, vsb, kwb, vwb, gates = nsa_prep(p, cos, up, dn, q_dtype=BF16)
    kcmp = nsa_compress(kc, pe_k, b, s)
    vcmp = nsa_compress(vc, pe_v, b, s)
    o = nsa_attn_prompt(q, gates, kcmp, vcmp, ksb, vsb, kwb, vwb, b, s)
    nw = min(WINDOW, s)
    rs = lambda a: a.reshape(b, s, NSA_KV_HEADS, NSA_HD)
    return o, (rs(kc), rs(vc), rs(ks), rs(vs), rs(kw)[:, s - nw:], rs(vw)[:, s - nw:])


def nsa_sample(x, scale, shift, ck, cv, sk, sv, wk, wv, page_table, w_in_p, pe_k, pe_v, db, nq):
    past = page_table.shape[1] * ck.shape[1]
    p, _ = project(x, scale, shift, w_in_p, db, nq)
    pos = past + jnp.tile(jnp.arange(nq, dtype=jnp.int32), db)
    cos, up, dn = _rope_tables(pos)
    q, kc, vc, ks, vs, kw, vw, ksb, vsb, kwb, vwb, gates = nsa_prep(p, cos, up, dn, q_dtype=F32)
    sums = nsa_page_sums(page_table, ck, cv, pe_k, pe_v)
    sel, o_c = nsa_decode_select(q, sums, kc, vc, pe_k, pe_v, db, nq, past)
    n_win = wk.shape[1]
    flat_win = lambda a: a.reshape(db, n_win, NSA_KVD)
    new_pad = lambda a: jnp.pad(a.reshape(db, nq, NSA_KVD), ((0, 0), (0, LANES - nq), (0, 0)))
    o = nsa_decode_attend(page_table, q, sel, gates, o_c, new_pad(ksb), new_pad(vsb), flat_win(wk), flat_win(wv),
                          new_pad(kwb), new_pad(vwb), sk, sv, db, nq, past)
    rs = lambda a: a.reshape(db, nq, NSA_KV_HEADS, NSA_HD)
    slide = lambda cache, new: jnp.concatenate([cache, rs(new).astype(cache.dtype)], 1)[:, nq:]
    return o, (rs(kc), rs(vc), rs(ks), rs(vs), slide(wk, kw), slide(wv, vw))


def _top_values(s, k):
    vals = []
    for _ in range(k):
        m = jnp.max(s, axis=0, keepdims=True)
        vals.append(m)
        s = jnp.where(s >= m, -jnp.inf, s)
    return vals


def _peer_route_kernel(qv_ref, keys_ref, s1_ref, s2_ref, e1_ref, e2_ref, th_ref):
    for h in range(PEER_HEADS):
        q1 = qv_ref[:, (2 * h) * PEER_DKEY:(2 * h + 1) * PEER_DKEY]
        q2 = qv_ref[:, (2 * h + 1) * PEER_DKEY:(2 * h + 2) * PEER_DKEY]
        s1 = lax.dot_general(keys_ref[h, 0], q1, NT, precision=HIGHEST, preferred_element_type=F32)
        s2 = lax.dot_general(keys_ref[h, 1], q2, NT, precision=HIGHEST, preferred_element_type=F32)
        top1 = _top_values(s1, PEER_TOPK)
        top2 = _top_values(s2, PEER_TOPK)
        t2 = jnp.concatenate(top2, axis=0)
        cand = jnp.concatenate([r + t2 for r in top1], axis=0)
        best = _top_values(cand, PEER_TOPK)
        z = best[0] * 0.0
        for v in best:
            z = z + jnp.exp(v - best[0])
        s1_ref[h] = s1
        s2_ref[h] = s2
        e1_ref[h] = jnp.exp(s1 - top1[0])
        e2_ref[h] = jnp.exp(s2 - top2[0]) / z
        th_ref[h] = best[PEER_TOPK - 1]


def peer_route(qv, keys, *, tm):
    t = qv.shape[0]
    big = jax.ShapeDtypeStruct((PEER_HEADS, PEER_NKEYS, t), F32)
    big_spec = pl.BlockSpec((PEER_HEADS, PEER_NKEYS, tm), lambda i: (0, 0, i))
    return pl.pallas_call(
        _peer_route_kernel,
        out_shape=(big, big, big, big, jax.ShapeDtypeStruct((PEER_HEADS, 1, t), F32)),
        grid=(t // tm,),
        in_specs=[
            pl.BlockSpec((tm, 2 * PEER_HEADS * PEER_DKEY), lambda i: (i, 0)),
            pl.BlockSpec(keys.shape, lambda i: (0, 0, 0, 0)),
        ],
        out_specs=(big_spec, big_spec, big_spec, big_spec,
                   pl.BlockSpec((PEER_HEADS, 1, tm), lambda i: (0, 0, i))),
        compiler_params=_params("parallel"),
        name="peer_route",
    )(qv, keys)


PEER_A_PER_STEP = SUBLANES
PEER_ROUTE_TILE = 256


def _gelu_tanh(x):
    return 0.5 * x * (1.0 + jnp.tanh(math.sqrt(2.0 / math.pi) * (x + 0.044715 * (x * x * x))))


def _peer_dense_kernel(h_ref, s1_ref, e1_ref, s2_ref, e2_ref, th_ref, u_ref, vt_ref, o_ref,
                       act_ref, g_ref, acc_ref):
    j = pl.program_id(1)
    tm = h_ref.shape[0]

    @pl.when(j == 0)
    def _():
        acc_ref[...] = jnp.zeros_like(acc_ref)

    act_ref[...] = lax.dot_general(u_ref[...], h_ref[...], NT, preferred_element_type=F32)
    for a in range(PEER_A_PER_STEP):
        rows = slice(a * PEER_NKEYS, (a + 1) * PEER_NKEYS)
        for c in range(tm // LANES):
            cols = slice(c * LANES, (c + 1) * LANES)
            w = jnp.zeros((PEER_NKEYS, LANES), F32)
            for h in range(PEER_HEADS):
                cand = s2_ref[h, :, cols] + s1_ref[h, a:a + 1, cols]
                w = w + e1_ref[h, a:a + 1, cols] * jnp.where(cand >= th_ref[h, :, cols], e2_ref[h, :, cols], 0.0)
            g_ref[rows, cols] = (w * _gelu_tanh(act_ref[rows, cols])).astype(BF16)
    acc_ref[...] += jnp.dot(vt_ref[...], g_ref[...], preferred_element_type=F32)

    @pl.when(j == pl.num_programs(1) - 1)
    def _():
        o_ref[...] = acc_ref[...].T


def peer_dense(h, s1, s2, e1, e2, th, u, vt, *, tm):
    t, d = h.shape
    ne = PEER_A_PER_STEP * PEER_NKEYS
    tok = pl.BlockSpec((PEER_HEADS, PEER_NKEYS, tm), lambda i, j: (0, 0, i))
    arow = pl.BlockSpec((PEER_HEADS, PEER_A_PER_STEP, tm), lambda i, j: (0, j, i))
    return pl.pallas_call(
        _peer_dense_kernel,
        out_shape=jax.ShapeDtypeStruct((t, d), F32),
        grid=(t // tm, PEER_NKEYS // PEER_A_PER_STEP),
        in_specs=[
            pl.BlockSpec((tm, d), lambda i, j: (i, 0)),
            arow, arow, tok, tok,
            pl.BlockSpec((PEER_HEADS, 1, tm), lambda i, j: (0, 0, i)),
            pl.BlockSpec((ne, d), lambda i, j: (j, 0)),
            pl.BlockSpec((d, ne), lambda i, j: (0, j)),
        ],
        out_specs=pl.BlockSpec((tm, d), lambda i, j: (i, 0)),
        scratch_shapes=[pltpu.VMEM((ne, tm), F32), pltpu.VMEM((ne, tm), BF16), pltpu.VMEM((d, tm), F32)],
        compiler_params=_params("parallel", "arbitrary"),
        name="peer_dense",
    )(h, s1, e1, s2, e2, th, u, vt)


def peer_ffn(x, scale, shift, wq, keys, u_bf, vt_bf, b, s):
    qv, h = project(x, scale, shift, wq, b, s)
    tm = _row_tile(b * s)
    s1, s2, e1, e2, th = peer_route(qv, keys, tm=min(tm, PEER_ROUTE_TILE))
    return peer_dense(h, s1, s2, e1, e2, th, u_bf, vt_bf, tm=tm)


def kernel(x_prompt, x_sample, state_gla, cache_cmp_k, cache_cmp_v, cache_sel_k, cache_sel_v, cache_win_k, cache_win_v, page_table, c_prompt, c_sample, ada_w, ada_b, ln_g, ln_b, gla_w_in, gla_w_a2, gla_b_a2, gla_gn, gla_w_out, nsa_w_in, nsa_pe_k, nsa_pe_v, nsa_w_out, peer_wq, peer_keys, peer_u, peer_v):
    bp, sp, d = x_prompt.shape
    bs, ss, _ = x_sample.shape
    groups = ((bp, sp), (bs, ss))
    c_all = jnp.concatenate([c_prompt, c_sample], axis=0)
    c_rows = _round_up(bp + bs, SUBLANES)
    mod = ada_mod(jnp.pad(c_all, ((0, c_rows - bp - bs), (0, 0))), ada_w, ada_b)
    ys = [x_prompt.reshape(bp * sp, d), x_sample.reshape(bs * ss, d)]
    gla_states, nsa_rows = ([], []), ([], [])
    for i in range(DEPTH):
        row0 = (0, bp)
        mods = [[mod[i, row0[n]:row0[n] + b, k * d:(k + 1) * d][:, None, :] for k in range(6)]
                for n, (b, _) in enumerate(groups)]
        j = i // N_MIXERS
        if i % N_MIXERS == 0:
            for n, (b, s) in enumerate(groups):
                shift, scale, gate = mods[n][:3]
                proj, _ = project(ys[n], scale, shift, gla_w_in[j], b, s)
                o, st = gla_recurrence(proj, gla_w_a2[j], gla_b_a2[j], None if n == 0 else state_gla[j], b, s)
                gla_states[n].append(st.astype(state_gla.dtype))
                ys[n] = post(ys[n], gate, ln_g[i, 0], ln_b[i, 0], b, s, sub=o, w_out=gla_w_out[j],
                             gla_proj=proj, gla_gn=gla_gn[j])
        else:
            w_in_p, w_out_p = _nsa_weights(nsa_w_in[j], nsa_w_out[j])
            for n, (b, s) in enumerate(groups):
                shift, scale, gate = mods[n][:3]
                if n == 0:
                    o, rows = nsa_prompt(ys[n], scale, shift, w_in_p, nsa_pe_k[j], nsa_pe_v[j], b, s)
                else:
                    o, rows = nsa_sample(ys[n], scale, shift, cache_cmp_k[j], cache_cmp_v[j], cache_sel_k[j],
                                         cache_sel_v[j], cache_win_k[j], cache_win_v[j], page_table,
                                         w_in_p, nsa_pe_k[j], nsa_pe_v[j], b, s)
                nsa_rows[n].append(rows)
                ys[n] = post(ys[n], gate, ln_g[i, 0], ln_b[i, 0], b, s, sub=o, w_out=w_out_p)
        u_bf = peer_u[i].astype(BF16)
        vt_bf = peer_v[i].astype(BF16).T
        for n, (b, s) in enumerate(groups):
            shift, scale, gate = mods[n][3:]
            f = peer_ffn(ys[n], scale, shift, peer_wq[i], peer_keys[i], u_bf, vt_bf, b, s)
            ys[n] = post(ys[n], gate, ln_g[i, 1], ln_b[i, 1], b, s, sub=f)

    st = lambda ts, k: jnp.stack([t[k] for t in ts])
    return (ys[0].reshape(bp, sp, d), ys[1].reshape(bs, ss, d),
            jnp.stack(gla_states[0]), jnp.stack(gla_states[1]),
            *(st(nsa_rows[0], k) for k in range(6)), *(st(nsa_rows[1], k) for k in range(6)))
```

```python
import functools
import math

import jax
import jax.numpy as jnp
import numpy as np
from jax import lax
from jax.experimental import pallas as pl
from jax.experimental.pallas import tpu as pltpu

D_MODEL = 1024
DEPTH = 2
N_MIXERS = 2
DN_ALPHA = (2.0 * DEPTH) ** 0.25
LN_EPS = 1e-5
F32 = jnp.float32
BF16 = jnp.bfloat16
HIGHEST = lax.Precision.HIGHEST

GLA_HEADS = 4
GLA_DK = D_MODEL // 2 // GLA_HEADS
GLA_DV = D_MODEL // GLA_HEADS
GLA_TAU = 16.0
GLA_CHUNK = 64
GLA_HK = GLA_HEADS * GLA_DK
GLA_HV = GLA_HEADS * GLA_DV

NSA_HEADS = 16
NSA_KV_HEADS = 4
NSA_GROUP = NSA_HEADS // NSA_KV_HEADS
NSA_HD = D_MODEL // NSA_HEADS
NSA_QD = NSA_HEADS * NSA_HD
NSA_KVD = NSA_KV_HEADS * NSA_HD
CMP_LEN = 32
CMP_STRIDE = 16
SEL_BLOCK = 64
N_SEL = 16
WINDOW = 512
Q_BLOCK = 128
FORCE_SCORE = 1e6
ROT_DIM = NSA_HD // 4
ROPE_THETA = 500000.0

PEER_HEADS = 8
PEER_NKEYS = 128
PEER_DKEY = 128
PEER_TOPK = 16

LANES = 128
SUBLANES = 8
VMEM_LIMIT_BYTES = 56 * 1024 * 1024
ROW_TILE = 512

NT = (((1,), (1,)), ((), ()))
TN = (((0,), (0,)), ((), ()))


def _round_up(n, m):
    return -(-n // m) * m


def _params(*sem):
    return pltpu.CompilerParams(dimension_semantics=sem, vmem_limit_bytes=VMEM_LIMIT_BYTES)


def _row_tile(t):
    return ROW_TILE if t % ROW_TILE == 0 else t


def _mod_operand(m, b, s, tm):
    d = m.shape[-1]
    if s % tm == 0:
        return m, pl.BlockSpec((None, 1, d), lambda i, *_: (i * tm // s, 0, 0))
    rows = jnp.broadcast_to(m, (b, s, d)).reshape(b * s, d)
    return rows, pl.BlockSpec((tm, d), lambda i, *_: (i, 0))


def _ada_kernel(c_ref, w_ref, b_ref, o_ref):
    c = c_ref[...]
    act = (c * jax.nn.sigmoid(c)).astype(BF16)
    o_ref[...] = jnp.dot(act, w_ref[...].astype(BF16), preferred_element_type=F32) + b_ref[...]


def ada_mod(c, w, b):
    r, d = c.shape
    nl, _, n = w.shape
    tn = n // 4
    return pl.pallas_call(
        _ada_kernel,
        out_shape=jax.ShapeDtypeStruct((nl, r, n), F32),
        grid=(nl, n // tn),
        in_specs=[pl.BlockSpec((r, d), lambda l, j: (0, 0)),
                  pl.BlockSpec((None, d, tn), lambda l, j: (l, 0, j)),
                  pl.BlockSpec((None, 1, tn), lambda l, j: (l, 0, j))],
        out_specs=pl.BlockSpec((None, r, tn), lambda l, j: (l, 0, j)),
        compiler_params=_params("parallel", "parallel"),
        name="ada_mod",
    )(c, w, b.reshape(nl, 1, n))


def _mod_matmul_kernel(x_ref, sc_ref, sh_ref, w_ref, o_ref, h_ref):
    @pl.when(pl.program_id(1) == 0)
    def _():
        h = x_ref[...] * (1.0 + sc_ref[...]) + sh_ref[...]
        h_ref[...] = h.astype(BF16)

    o_ref[...] = jnp.dot(h_ref[...], w_ref[...], preferred_element_type=F32)


def project(x, scale, shift, w, b, s):
    t, d = x.shape
    npad = _round_up(w.shape[1], LANES)
    wp = jnp.pad(w, ((0, 0), (0, npad - w.shape[1]))).astype(BF16)
    tn = next(c for c in (1024, 768, 640, 512, 384, 256, 128) if npad % c == 0)
    tm = _row_tile(t)
    sc, mod_spec = _mod_operand(scale, b, s, tm)
    sh, _ = _mod_operand(shift, b, s, tm)
    return pl.pallas_call(
        _mod_matmul_kernel,
        out_shape=(jax.ShapeDtypeStruct((t, npad), F32), jax.ShapeDtypeStruct((t, d), BF16)),
        grid=(t // tm, npad // tn),
        in_specs=[pl.BlockSpec((tm, d), lambda i, j: (i, 0)), mod_spec, mod_spec,
                  pl.BlockSpec((d, tn), lambda i, j: (0, j))],
        out_specs=(pl.BlockSpec((tm, tn), lambda i, j: (i, j)), pl.BlockSpec((tm, d), lambda i, j: (i, 0))),
        compiler_params=_params("parallel", "arbitrary"),
        name="mod_matmul",
    )(x, sc, sh, wp)


def _deepnorm(x, sub, gate, g, b):
    y = DN_ALPHA * x + (1.0 + gate) * sub
    mu = jnp.mean(y, axis=-1, keepdims=True)
    var = jnp.mean(jnp.square(y - mu), axis=-1, keepdims=True)
    return (y - mu) * lax.rsqrt(var + LN_EPS) * g + b


def _post_gla_kernel(o_ref, r_ref, x_ref, gate_ref, gn_ref, w_ref, g_ref, b_ref, y_ref):
    parts = []
    for h in range(GLA_HEADS):
        cols = slice(h * GLA_DV, (h + 1) * GLA_DV)
        o = o_ref[:, cols]
        mu = jnp.mean(o, axis=-1, keepdims=True)
        var = jnp.mean(jnp.square(o - mu), axis=-1, keepdims=True)
        parts.append((o - mu) * lax.rsqrt(var + LN_EPS) * gn_ref[:, cols])
    r = r_ref[...]
    f = (jnp.concatenate(parts, axis=1) * (r * jax.nn.sigmoid(r))).astype(BF16)
    sub = jnp.dot(f, w_ref[...], preferred_element_type=F32)
    y_ref[...] = _deepnorm(x_ref[...], sub, gate_ref[...], g_ref[...], b_ref[...])


def _post_matmul_kernel(o_ref, x_ref, gate_ref, w_ref, g_ref, b_ref, y_ref):
    sub = jnp.dot(o_ref[...].astype(BF16), w_ref[...], preferred_element_type=F32)
    y_ref[...] = _deepnorm(x_ref[...], sub, gate_ref[...], g_ref[...], b_ref[...])


def _post_plain_kernel(o_ref, x_ref, gate_ref, g_ref, b_ref, y_ref):
    y_ref[...] = _deepnorm(x_ref[...], o_ref[...], gate_ref[...], g_ref[...], b_ref[...])


def post(x, gate, ln_g, ln_b, b, s, *, sub=None, w_out=None, gla_proj=None, gla_gn=None):
    t, d = x.shape
    tm = _row_tile(t)
    gate_arr, gate_spec = _mod_operand(gate, b, s, tm)
    row = pl.BlockSpec((tm, d), lambda i: (i, 0))
    vec = pl.BlockSpec((1, d), lambda i: (0, 0))
    mat = pl.BlockSpec((d, d), lambda i: (0, 0))
    g2, b2 = ln_g.reshape(1, d), ln_b.reshape(1, d)
    if gla_proj is not None:
        r_spec = pl.BlockSpec((tm, GLA_HV), lambda i: (i, (2 * GLA_HK + GLA_HV) // GLA_HV))
        args = (sub, gla_proj, x, gate_arr, gla_gn.reshape(1, d), w_out.astype(BF16), g2, b2)
        specs = [row, r_spec, row, gate_spec, vec, mat, vec, vec]
        body = _post_gla_kernel
    elif w_out is not None:
        args = (sub, x, gate_arr, w_out.astype(BF16), g2, b2)
        specs = [row, row, gate_spec, mat, vec, vec]
        body = _post_matmul_kernel
    else:
        args = (sub, x, gate_arr, g2, b2)
        specs = [row, row, gate_spec, vec, vec]
        body = _post_plain_kernel
    return pl.pallas_call(
        body,
        out_shape=jax.ShapeDtypeStruct((t, d), F32),
        grid=(t // tm,),
        in_specs=specs,
        out_specs=row,
        compiler_params=_params("parallel"),
        name="post",
    )(*args)


GLA_MIN_ROWS = 64


def _cumsum_rows(x):
    n = x.shape[0]
    row = lax.broadcasted_iota(jnp.int32, (n, 1), 0)
    shift = 1
    while shift < n:
        x = x + jnp.where(row >= shift, pltpu.roll(x, shift, 0), 0.0)
        shift *= 2
    return x


def _gla_kernel(has_s0, q_ref, k_ref, v_ref, a_ref, wa_ref, ba_ref, *rest):
    if has_s0:
        s0_ref, o_ref, st_ref, s_scr = rest
    else:
        o_ref, st_ref, s_scr = rest
    c = pl.program_id(1)
    n_rows = q_ref.shape[0]
    rows = max(n_rows, GLA_MIN_ROWS)

    @pl.when(c == 0)
    def _():
        s_scr[...] = s0_ref[...] if has_s0 else jnp.zeros_like(s_scr)

    def padded(x):
        if rows == n_rows:
            return x
        return jnp.concatenate([x, jnp.zeros((rows - n_rows, x.shape[1]), x.dtype)], axis=0)

    z = jnp.dot(a_ref[...], wa_ref[...], precision=HIGHEST, preferred_element_type=F32) + ba_ref[...]
    logg = (jnp.minimum(z, 0.0) - jnp.log1p(jnp.exp(-jnp.abs(z)))) / GLA_TAU
    b = _cumsum_rows(padded(logg))
    q, k, v = padded(q_ref[...]), padded(k_ref[...]), padded(v_ref[...])
    qe = q * (GLA_DK ** -0.5) * jnp.exp(b)
    ke = k * jnp.exp(-b)
    b_last = b[rows - 1:rows, :]
    kd = k * jnp.exp(b_last - b)
    e_last = jnp.exp(b_last)
    causal = lax.broadcasted_iota(jnp.int32, (rows, 1), 0) >= lax.broadcasted_iota(jnp.int32, (1, rows), 1)
    eye = lax.broadcasted_iota(jnp.int32, (GLA_DK, 1), 0) == lax.broadcasted_iota(jnp.int32, (1, GLA_DK), 1)
    for h in range(GLA_HEADS):
        ck = slice(h * GLA_DK, (h + 1) * GLA_DK)
        cv = slice(h * GLA_DV, (h + 1) * GLA_DV)
        att = lax.dot_general(qe[:, ck], ke[:, ck], NT, precision=HIGHEST, preferred_element_type=F32)
        att = jnp.where(causal, att, 0.0)
        s_h = s_scr[h]
        o = (jnp.dot(att, v[:, cv], precision=HIGHEST, preferred_element_type=F32)
             + jnp.dot(qe[:, ck], s_h, precision=HIGHEST, preferred_element_type=F32))
        o_ref[:, cv] = o[:n_rows]
        e_col = jnp.sum(jnp.where(eye, e_last[:, ck], 0.0), axis=1, keepdims=True)
        s_scr[h] = e_col * s_h + lax.dot_general(kd[:, ck], v[:, cv], TN, precision=HIGHEST,
                                                 preferred_element_type=F32)

    @pl.when(c == pl.num_programs(1) - 1)
    def _():
        st_ref[...] = s_scr[...]


def gla_recurrence(proj, w_a2, b_a2, s0, b, s):
    chunk = GLA_CHUNK if s % GLA_CHUNK == 0 else s
    nc = s // chunk
    wa = jnp.pad(w_a2, ((0, LANES - w_a2.shape[0]), (0, 0)))
    a_block = (2 * GLA_HK + 2 * GLA_HV) // LANES
    row = lambda width, blk: pl.BlockSpec((chunk, width), lambda bi, c: (bi * nc + c, blk))
    st_spec = pl.BlockSpec((None, GLA_HEADS, GLA_DK, GLA_DV), lambda bi, c: (bi, 0, 0, 0))
    in_specs = [row(GLA_HK, 0), row(GLA_HK, 1), row(GLA_HV, 2 * GLA_HK // GLA_HV), row(LANES, a_block),
                pl.BlockSpec(wa.shape, lambda bi, c: (0, 0)), pl.BlockSpec((1, GLA_HK), lambda bi, c: (0, 0))]
    args = [proj, proj, proj, proj, wa, b_a2.reshape(1, GLA_HK)]
    if s0 is not None:
        in_specs.append(st_spec)
        args.append(s0)
    return pl.pallas_call(
        functools.partial(_gla_kernel, s0 is not None),
        out_shape=(jax.ShapeDtypeStruct((b * s, GLA_HV), F32),
                   jax.ShapeDtypeStruct((b, GLA_HEADS, GLA_DK, GLA_DV), F32)),
        grid=(b, nc),
        in_specs=in_specs,
        out_specs=(pl.BlockSpec((chunk, GLA_HV), lambda bi, c: (bi * nc + c, 0)), st_spec),
        scratch_shapes=[pltpu.VMEM((GLA_HEADS, GLA_DK, GLA_DV), F32)],
        compiler_params=_params("parallel", "arbitrary"),
        name="gla_recurrence",
    )(*args)


def _nsa_q_perm():
    cols = []
    for m in range(NSA_KV_HEADS // 2):
        for r in range(NSA_GROUP):
            for half in range(2):
                head = (2 * m + half) * NSA_GROUP + r
                cols.append(np.arange(NSA_HD) + head * NSA_HD)
    return np.concatenate(cols)


def _nsa_gate_expand():
    perm = _nsa_q_perm()
    ex = np.zeros((LANES, 3 * NSA_QD), np.float32)
    for br in range(3):
        for col in range(NSA_QD):
            ex[br * NSA_HEADS + perm[col] // NSA_HD, br * NSA_QD + col] = 1.0
    return ex


def _nsa_slc_matrix(nc_pad, nsb_pad):
    ratio = SEL_BLOCK // CMP_STRIDE
    m = np.zeros((nc_pad, nsb_pad), np.float32)
    for j in range(nsb_pad):
        for o in range(CMP_LEN // CMP_STRIDE):
            for i in range(ratio):
                n = ratio * j + i + o - (CMP_LEN // CMP_STRIDE - 1)
                if 0 <= n < nc_pad:
                    m[n, j] += 1.0
    return m


def _rope_tables(pos):
    half = ROT_DIM // 2
    inv = ROPE_THETA ** (-jnp.arange(half, dtype=F32) * 2.0 / ROT_DIM)
    ang = pos.astype(F32)[:, None] * inv[None, :]
    cos, sin = jnp.cos(ang), jnp.sin(ang)
    t = pos.shape[0]
    ones = jnp.ones((t, NSA_HD - ROT_DIM), F32)
    zeros = jnp.zeros((t, NSA_HD - ROT_DIM), F32)
    z8 = jnp.zeros((t, half), F32)
    c = jnp.concatenate([cos, cos, ones], 1)
    up = jnp.concatenate([-sin, z8, zeros], 1)
    dn = jnp.concatenate([z8, sin, zeros], 1)
    two = lambda a: jnp.concatenate([a, a], 1)
    return two(c), two(up), two(dn)


def _nsa_prep_kernel(p_ref, cos_ref, up_ref, dn_ref, q_ref, kc_ref, vc_ref, ks_ref, vs_ref, kw_ref, vw_ref,
                     ksb_ref, vsb_ref, kwb_ref, vwb_ref, g_ref):
    def rope(x):
        reps = x.shape[1] // LANES
        tile = lambda a: jnp.concatenate([a] * reps, axis=1)
        w = x.shape[1]
        return (x * tile(cos_ref[...]) + pltpu.roll(x, w - ROT_DIM // 2, 1) * tile(up_ref[...])
                + pltpu.roll(x, ROT_DIM // 2, 1) * tile(dn_ref[...]))

    q_ref[...] = (rope(p_ref[:, :NSA_QD]) * (NSA_HD ** -0.5)).astype(q_ref.dtype)
    kv = lambda k: p_ref[:, NSA_QD + k * NSA_KVD:NSA_QD + (k + 1) * NSA_KVD]
    kc_ref[...] = rope(kv(0))
    vc_ref[...] = kv(1)
    ks = rope(kv(2))
    ks_ref[...] = ks
    ksb_ref[...] = ks.astype(BF16)
    vs_ref[...] = kv(3)
    vsb_ref[...] = kv(3).astype(BF16)
    kw = rope(kv(4))
    kw_ref[...] = kw
    kwb_ref[...] = kw.astype(BF16)
    vw_ref[...] = kv(5)
    vwb_ref[...] = kv(5).astype(BF16)
    g0 = NSA_QD + 6 * NSA_KVD
    g_ref[...] = jax.nn.sigmoid(p_ref[:, g0:g0 + LANES])


def nsa_prep(p, cos, up, dn, *, q_dtype):
    t = p.shape[0]
    tm = _row_tile(t)
    period = cos.shape[0] // tm
    tab = pl.BlockSpec((tm, LANES), lambda i: (i % period, 0))
    kv32 = jax.ShapeDtypeStruct((t, NSA_KVD), F32)
    kv16 = jax.ShapeDtypeStruct((t, NSA_KVD), BF16)
    kvs = pl.BlockSpec((tm, NSA_KVD), lambda i: (i, 0))
    return pl.pallas_call(
        _nsa_prep_kernel,
        out_shape=(jax.ShapeDtypeStruct((t, NSA_QD), q_dtype),) + (kv32,) * 6 + (kv16,) * 4
        + (jax.ShapeDtypeStruct((t, LANES), F32),),
        grid=(t // tm,),
        in_specs=[pl.BlockSpec((tm, p.shape[1]), lambda i: (i, 0)), tab, tab, tab],
        out_specs=(pl.BlockSpec((tm, NSA_QD), lambda i: (i, 0)),) + (kvs,) * 10
        + (pl.BlockSpec((tm, LANES), lambda i: (i, 0)),),
        compiler_params=_params("parallel"),
        name="nsa_prep",
    )(p, cos, up, dn)


def _block_sums(x, w):
    n_sub = x.shape[0] // CMP_STRIDE
    x = x.reshape(n_sub, CMP_STRIDE, NSA_KVD)
    w = 1.0 + w
    return jnp.sum(x * w[None, :CMP_STRIDE], axis=1), jnp.sum(x * w[None, CMP_STRIDE:], axis=1)


def _nsa_compress_kernel(x_ref, w_ref, o_ref):
    first, second = _block_sums(x_ref[...], w_ref[...])
    n_sub = first.shape[0]
    nxt = pltpu.roll(second, n_sub - 1, 0)
    row = lax.broadcasted_iota(jnp.int32, (n_sub, 1), 0)
    o_ref[...] = jnp.where(row < n_sub - 1, (first + nxt) / CMP_LEN, 0.0).astype(o_ref.dtype)


def nsa_compress(x, pe, b, s):
    n_sub = s // CMP_STRIDE
    return pl.pallas_call(
        _nsa_compress_kernel,
        out_shape=jax.ShapeDtypeStruct((b, n_sub, NSA_KVD), BF16),
        grid=(b,),
        in_specs=[pl.BlockSpec((s, NSA_KVD), lambda i: (i, 0)),
                  pl.BlockSpec((CMP_LEN, NSA_KVD), lambda i: (0, 0))],
        out_specs=pl.BlockSpec((None, n_sub, NSA_KVD), lambda i: (i, 0, 0)),
        compiler_params=_params("parallel"),
        name="nsa_compress",
    )(x, pe.reshape(CMP_LEN, NSA_KVD))


NSA_KV_TILE = 512


def _mask_rows(s, allowed):
    if allowed.shape[0] == s.shape[0]:
        return jnp.where(allowed, s, -jnp.inf)
    reps = s.shape[0] // allowed.shape[0]
    s3 = s.reshape(reps, allowed.shape[0], s.shape[1])
    return jnp.where(allowed[None], s3, -jnp.inf).reshape(s.shape)


def _softmax_rows(s, allowed):
    s = _mask_rows(s, allowed)
    m = jnp.max(s, axis=1, keepdims=True)
    m = jnp.where(m == -jnp.inf, 0.0, m)
    e = jnp.exp(s - m)
    return e / jnp.maximum(jnp.sum(e, axis=1, keepdims=True), 1e-30)


def _select_blocks(score, n_sel):
    nb = score.shape[1]
    lane = lax.broadcasted_iota(jnp.int32, score.shape, 1).astype(F32)
    sel = jnp.zeros(score.shape, F32)
    for _ in range(n_sel):
        m = jnp.max(score, axis=1, keepdims=True)
        first = jnp.min(jnp.where(score == m, lane, float(nb)), axis=1, keepdims=True)
        pick = lane == first
        sel = jnp.where(pick, 1.0, sel)
        score = jnp.where(pick, -jnp.inf, score)
    return sel


def _block_scores(p_slc, pos_q):
    jb = lax.broadcasted_iota(jnp.int32, (1, p_slc.shape[1]), 1)
    cur = pos_q // SEL_BLOCK
    valid = jb * SEL_BLOCK <= pos_q
    forced = (jb == 0) | (jb == cur) | (jb == cur - 1)
    return jnp.where(forced, FORCE_SCORE, jnp.where(valid, p_slc, -1.0))


def _select_blocks_t(score, n_sel):
    st = score.T
    nb = st.shape[0]
    blk = lax.broadcasted_iota(jnp.int32, st.shape, 0).astype(F32)
    sel = jnp.zeros(st.shape, F32)
    for _ in range(n_sel):
        m = jnp.max(st, axis=0, keepdims=True)
        first = jnp.min(jnp.where(st == m, blk, float(nb)), axis=0, keepdims=True)
        pick = blk == first
        sel = jnp.where(pick, 1.0, sel)
        st = jnp.where(pick, -jnp.inf, st)
    return sel.T


def _online_softmax_step(s, allowed, v, m_old, l_old, acc_old, feature_major=False):
    s = _mask_rows(s, allowed)
    m_new = jnp.maximum(m_old, jnp.max(s, axis=1, keepdims=True))
    m_use = jnp.where(m_new == -jnp.inf, 0.0, m_new)
    alpha = jnp.exp(m_old - m_use)
    p = jnp.exp(s - m_use)
    l_new = alpha * l_old + jnp.sum(p, axis=1, keepdims=True)
    if feature_major:
        pv = lax.dot_general(p.astype(BF16), v, NT, preferred_element_type=F32)
    else:
        pv = jnp.dot(p.astype(BF16), v, preferred_element_type=F32)
    return m_new, l_new, alpha * acc_old + pv


def _padded_query(tile, m, in_half):
    sel = jnp.where(in_half, tile, jnp.zeros_like(tile))
    z = jnp.zeros_like(sel)
    return jnp.concatenate([sel, z] if m == 0 else [z, sel], axis=1)


def _nsa_attn_kernel(q_ref, g_ref, kcmp_ref, vcmp_ref, ks_ref, vs_ref, kw_ref, vw_ref, slc_ref, ex_ref,
                     o_ref, qz_ref, sel_ref, oc_ref, m_ref, l_ref, acc_ref):
    i = pl.program_id(1)
    start = i * Q_BLOCK
    pos_q = start + lax.broadcasted_iota(jnp.int32, (Q_BLOCK, 1), 0)
    lane128 = lax.broadcasted_iota(jnp.int32, (1, LANES), 1)
    n_cmp = kcmp_ref.shape[0]
    cmp_end = CMP_STRIDE * lax.broadcasted_iota(jnp.int32, (1, n_cmp), 1) + (CMP_LEN - 1)
    cmp_ok = cmp_end <= pos_q
    gexp = jnp.dot(g_ref[...], ex_ref[...], precision=HIGHEST, preferred_element_type=F32)
    o_ref[...] = jnp.zeros_like(o_ref)

    win_base = pl.multiple_of(jnp.maximum(start - WINDOW, 0), Q_BLOCK)
    band = WINDOW + Q_BLOCK
    kw_t = kw_ref[pl.ds(win_base, band), :]
    vw_t = vw_ref[pl.ds(win_base, band), :]
    dpos = pos_q - (win_base + lax.broadcasted_iota(jnp.int32, (1, band), 1))
    win_ok = (dpos >= 0) & (dpos < WINDOW)

    for g in range(NSA_KV_HEADS):
        m, half = divmod(g, 2)
        in_half = (lane128 // NSA_HD) == half
        qz = jnp.concatenate(
            [_padded_query(q_ref[:, (m * NSA_GROUP + r) * LANES:(m * NSA_GROUP + r + 1) * LANES], m, in_half)
             for r in range(NSA_GROUP)], axis=0)
        qz_ref[g] = qz
        s = lax.dot_general(qz, kcmp_ref[...], NT, preferred_element_type=F32)
        p = _softmax_rows(s, cmp_ok)
        oc_ref[g] = jnp.dot(p.astype(BF16), vcmp_ref[...], preferred_element_type=F32)[:, m * LANES:(m + 1) * LANES]
        imp = p[0:Q_BLOCK]
        for r in range(1, NSA_GROUP):
            imp = imp + p[r * Q_BLOCK:(r + 1) * Q_BLOCK]
        p_slc = jnp.dot(imp, slc_ref[...], precision=HIGHEST, preferred_element_type=F32)
        sel_ref[g] = _select_blocks_t(_block_scores(p_slc, pos_q), N_SEL).astype(BF16)

    m_ref[...] = jnp.full_like(m_ref, -jnp.inf)
    l_ref[...] = jnp.zeros_like(l_ref)
    acc_ref[...] = jnp.zeros_like(acc_ref)

    def kv_step(t, carry):
        k0 = pl.multiple_of(t * NSA_KV_TILE, NSA_KV_TILE)
        k_t = ks_ref[pl.ds(k0, NSA_KV_TILE), :]
        v_t = vs_ref[pl.ds(k0, NSA_KV_TILE), :]
        key = k0 + lax.broadcasted_iota(jnp.int32, (1, NSA_KV_TILE), 1)
        blk = lax.broadcasted_iota(jnp.int32, (LANES, 1), 0)
        expand = jnp.where(blk == key // SEL_BLOCK, 1.0, 0.0).astype(BF16)
        causal = key <= pos_q
        for g in range(NSA_KV_HEADS):
            chosen = jnp.dot(sel_ref[g], expand, preferred_element_type=F32)
            s = lax.dot_general(qz_ref[g], k_t, NT, preferred_element_type=F32)
            m_ref[g], l_ref[g], acc_ref[g] = _online_softmax_step(s, (chosen > 0.5) & causal, v_t,
                                                                  m_ref[g], l_ref[g], acc_ref[g])
        return carry

    lax.fori_loop(0, (start + Q_BLOCK + NSA_KV_TILE - 1) // NSA_KV_TILE, kv_step, 0)

    for g in range(NSA_KV_HEADS):
        m, half = divmod(g, 2)
        in_half = (lane128 // NSA_HD) == half
        mcols = slice(m * LANES, (m + 1) * LANES)
        o_s = acc_ref[g][:, mcols] / jnp.maximum(l_ref[g], 1e-30)
        s = lax.dot_general(qz_ref[g], kw_t, NT, preferred_element_type=F32)
        o_w = jnp.dot(_softmax_rows(s, win_ok).astype(BF16), vw_t, preferred_element_type=F32)[:, mcols]
        o_c = oc_ref[g]
        for r in range(NSA_GROUP):
            t = m * NSA_GROUP + r
            cols = slice(t * LANES, (t + 1) * LANES)
            rows = slice(r * Q_BLOCK, (r + 1) * Q_BLOCK)
            comb = (gexp[:, cols] * o_c[rows] + gexp[:, NSA_QD + t * LANES:NSA_QD + (t + 1) * LANES] * o_s[rows]
                    + gexp[:, 2 * NSA_QD + t * LANES:2 * NSA_QD + (t + 1) * LANES] * o_w[rows])
            o_ref[:, cols] += jnp.where(in_half, comb, 0.0)


def nsa_attn_prompt(q, gates, kcmp, vcmp, ks, vs, kw, vw, b, s):
    nq = s // Q_BLOCK
    rows = NSA_GROUP * Q_BLOCK
    slc = jnp.asarray(_nsa_slc_matrix(s // CMP_STRIDE, LANES))
    ex = jnp.asarray(_nsa_gate_expand())
    seq = pl.BlockSpec((s, NSA_KVD), lambda bi, i: (bi, 0))
    cmp_spec = pl.BlockSpec((None, s // CMP_STRIDE, NSA_KVD), lambda bi, i: (bi, 0, 0))
    return pl.pallas_call(
        _nsa_attn_kernel,
        out_shape=jax.ShapeDtypeStruct((b * s, NSA_QD), F32),
        grid=(b, nq),
        in_specs=[
            pl.BlockSpec((Q_BLOCK, NSA_QD), lambda bi, i: (bi * nq + i, 0)),
            pl.BlockSpec((Q_BLOCK, LANES), lambda bi, i: (bi * nq + i, 0)),
            cmp_spec, cmp_spec, seq, seq, seq, seq,
            pl.BlockSpec(slc.shape, lambda bi, i: (0, 0)),
            pl.BlockSpec(ex.shape, lambda bi, i: (0, 0)),
        ],
        out_specs=pl.BlockSpec((Q_BLOCK, NSA_QD), lambda bi, i: (bi * nq + i, 0)),
        scratch_shapes=[pltpu.VMEM((NSA_KV_HEADS, rows, NSA_KVD), BF16),
                        pltpu.VMEM((NSA_KV_HEADS, Q_BLOCK, LANES), BF16),
                        pltpu.VMEM((NSA_KV_HEADS, rows, LANES), F32),
                        pltpu.VMEM((NSA_KV_HEADS, rows, 1), F32),
                        pltpu.VMEM((NSA_KV_HEADS, rows, 1), F32),
                        pltpu.VMEM((NSA_KV_HEADS, rows, NSA_KVD), F32)],
        compiler_params=_params("parallel", "arbitrary"),
        name="nsa_attn_prompt",
    )(q, gates, kcmp, vcmp, ks, vs, kw, vw, slc, ex)


PAGES_PER_STEP = 8
SUM_PAGES_PER_STEP = 16


def _feature_major(cache):
    n, tokens = cache.shape[:2]
    return cache.transpose(0, 2, 3, 1).reshape(n, NSA_KVD, tokens)


def _page_specs(n, page):
    def one(u):
        return pl.BlockSpec((None, NSA_KVD, page), lambda bi, j, pt: (pt[bi, j * n + u], 0, 0))
    return [one(u) for u in range(n)]


def _nsa_page_sums_kernel(pt_ref, *refs):
    n = SUM_PAGES_PER_STEP
    kp, vp = refs[:n], refs[n:2 * n]
    wk1_ref, wk2_ref, wv1_ref, wv2_ref, grp_ref, fk_ref, sk_ref, fv_ref, sv_ref = refs[2 * n:]

    def sums(pages, w_ref, o_ref):
        xw = jnp.concatenate([(p[...] * w_ref[...]).astype(BF16) for p in pages], axis=1)
        o_ref[...] = jnp.dot(xw, grp_ref[...], preferred_element_type=F32)

    sums(kp, wk1_ref, fk_ref)
    sums(kp, wk2_ref, sk_ref)
    sums(vp, wv1_ref, fv_ref)
    sums(vp, wv2_ref, sv_ref)


def nsa_page_sums(page_table, pool_k, pool_v, pe_k, pe_v):
    db, n_pages = page_table.shape
    page = pool_k.shape[2]
    n = SUM_PAGES_PER_STEP
    per_page = page // CMP_STRIDE
    assert n * per_page == LANES
    out = jax.ShapeDtypeStruct((db, NSA_KVD, n_pages * per_page), F32)
    out_spec = pl.BlockSpec((None, NSA_KVD, LANES), lambda bi, j, pt: (bi, 0, j))
    w_spec = pl.BlockSpec((NSA_KVD, page), lambda bi, j, pt: (0, 0))
    halves = lambda pe: [jnp.tile((1.0 + pe.reshape(CMP_LEN, NSA_KVD)[o:o + CMP_STRIDE]).T, (1, per_page))
                         for o in (0, CMP_STRIDE)]
    grp = np.zeros((n * page, LANES), np.float32)
    tok = np.arange(n * page)
    grp[tok, tok // CMP_STRIDE] = 1.0
    grp = jnp.asarray(grp, BF16)
    return pl.pallas_call(
        _nsa_page_sums_kernel,
        out_shape=(out,) * 4,
        grid_spec=pltpu.PrefetchScalarGridSpec(
            num_scalar_prefetch=1,
            grid=(db, n_pages // n),
            in_specs=_page_specs(n, page) * 2 + [w_spec] * 4 + [pl.BlockSpec(grp.shape, lambda bi, j, pt: (0, 0))],
            out_specs=(out_spec,) * 4,
        ),
        compiler_params=_params("parallel", "arbitrary"),
        name="nsa_page_sums",
    )(page_table, *([pool_k] * n), *([pool_v] * n), *halves(pe_k), *halves(pe_v), grp)


def _decode_queries(q_ref, g):
    m, half = divmod(g, 2)
    in_half = (lax.broadcasted_iota(jnp.int32, (1, LANES), 1) // NSA_HD) == half
    tiles = [_padded_query(q_ref[:, (m * NSA_GROUP + r) * LANES:(m * NSA_GROUP + r + 1) * LANES], m, in_half)
             for r in range(NSA_GROUP)]
    return jnp.concatenate(tiles, axis=0).astype(BF16)


def _group_rows(x):
    return jnp.concatenate([x] * NSA_GROUP, axis=0)


def _nsa_decode_select_kernel(past, q_ref, fk_ref, sk_ref, fv_ref, sv_ref, kn_ref, vn_ref, wk_ref, wv_ref,
                              slc_ref, sel_ref, oc_ref):
    nq = q_ref.shape[0]
    n_cmp = fk_ref.shape[1]
    col = lax.broadcasted_iota(jnp.int32, (1, n_cmp), 1)
    eye = (lax.broadcasted_iota(jnp.int32, (NSA_KVD, 1), 0)
           == lax.broadcasted_iota(jnp.int32, (1, NSA_KVD), 1))

    def summaries(f_ref, s_ref, new_ref, w_ref):
        w = 1.0 + w_ref[CMP_STRIDE:CMP_STRIDE + nq, :]
        second_new = jnp.sum(new_ref[...] * w, axis=0, keepdims=True)
        new_col = jnp.sum(jnp.where(eye, second_new, 0.0), axis=1, keepdims=True)
        nxt = jnp.where(col == n_cmp - 1, new_col, pltpu.roll(s_ref[...], n_cmp - 1, 1))
        return ((f_ref[...] + nxt) / CMP_LEN).astype(BF16)

    kcmp_t = summaries(fk_ref, sk_ref, kn_ref, wk_ref)
    vcmp_t = summaries(fv_ref, sv_ref, vn_ref, wv_ref)
    pos_q = past + lax.broadcasted_iota(jnp.int32, (nq, 1), 0)
    cmp_end = CMP_STRIDE * lax.broadcasted_iota(jnp.int32, (1, n_cmp), 1) + (CMP_LEN - 1)
    cmp_ok = _group_rows(cmp_end <= pos_q)
    lane128 = lax.broadcasted_iota(jnp.int32, (1, LANES), 1)
    oc_ref[...] = jnp.zeros_like(oc_ref)
    for g in range(NSA_KV_HEADS):
        m, half = divmod(g, 2)
        in_half = (lane128 // NSA_HD) == half
        s = jnp.dot(_decode_queries(q_ref, g), kcmp_t, preferred_element_type=F32)
        p = _softmax_rows(s, cmp_ok)
        o_c = lax.dot_general(p.astype(BF16), vcmp_t, NT,
                              preferred_element_type=F32)[:, m * LANES:(m + 1) * LANES]
        imp = p[0:nq]
        for r in range(1, NSA_GROUP):
            imp = imp + p[r * nq:(r + 1) * nq]
        for r in range(NSA_GROUP):
            cols = slice((m * NSA_GROUP + r) * LANES, (m * NSA_GROUP + r + 1) * LANES)
            oc_ref[:, cols] += jnp.where(in_half, o_c[r * nq:(r + 1) * nq], 0.0)
        p_slc = jnp.dot(imp, slc_ref[...], precision=HIGHEST, preferred_element_type=F32)
        sel_ref[g] = _select_blocks(_block_scores(p_slc, pos_q), N_SEL)


def nsa_decode_select(q, sums, kc_new, vc_new, pe_k, pe_v, db, nq, past):
    n_cmp = sums[0].shape[2]
    nsb_pad = _round_up(-(-(past + nq) // SEL_BLOCK), LANES)
    slc = jnp.asarray(_nsa_slc_matrix(n_cmp, nsb_pad))
    cmp_spec = pl.BlockSpec((None, NSA_KVD, n_cmp), lambda bi: (bi, 0, 0))
    new_spec = pl.BlockSpec((nq, NSA_KVD), lambda bi: (bi, 0))
    pe_spec = pl.BlockSpec((CMP_LEN, NSA_KVD), lambda bi: (0, 0))
    return pl.pallas_call(
        functools.partial(_nsa_decode_select_kernel, past),
        out_shape=(jax.ShapeDtypeStruct((db, NSA_KV_HEADS, nq, nsb_pad), F32),
                   jax.ShapeDtypeStruct((db * nq, NSA_QD), F32)),
        grid=(db,),
        in_specs=[pl.BlockSpec((nq, NSA_QD), lambda bi: (bi, 0)), cmp_spec, cmp_spec, cmp_spec, cmp_spec,
                  new_spec, new_spec, pe_spec, pe_spec, pl.BlockSpec(slc.shape, lambda bi: (0, 0))],
        out_specs=(pl.BlockSpec((None, NSA_KV_HEADS, nq, nsb_pad), lambda bi: (bi, 0, 0, 0)),
                   pl.BlockSpec((nq, NSA_QD), lambda bi: (bi, 0))),
        compiler_params=_params("parallel"),
        name="nsa_decode_select",
    )(q, *sums, kc_new, vc_new, pe_k.reshape(CMP_LEN, NSA_KVD), pe_v.reshape(CMP_LEN, NSA_KVD), slc)


def _nsa_decode_attend_kernel(past, pt_ref, q_ref, sel_ref, g_ref, oc_ref, ksn_ref, vsn_ref, wk_ref, wv_ref,
                              kwn_ref, vwn_ref, ex_ref, *refs):
    n = PAGES_PER_STEP
    kp, vp = refs[:n], refs[n:2 * n]
    o_ref, m_scr, l_scr, acc_scr = refs[2 * n:]
    j = pl.program_id(1)
    nq = q_ref.shape[0]
    page = kp[0].shape[1]
    n_keys = n * page
    pos_q = _group_rows(past + lax.broadcasted_iota(jnp.int32, (nq, 1), 0))
    nsb_pad = sel_ref.shape[-1]
    blk = lax.broadcasted_iota(jnp.int32, (nsb_pad, 1), 0)

    @pl.when(j == 0)
    def _():
        m_scr[...] = jnp.full_like(m_scr, -jnp.inf)
        l_scr[...] = jnp.zeros_like(l_scr)
        acc_scr[...] = jnp.zeros_like(acc_scr)

    def scores(g, k, feature_major):
        q = _decode_queries(q_ref, g)
        if feature_major:
            return jnp.dot(q, k, preferred_element_type=F32)
        return lax.dot_general(q, k, NT, preferred_element_type=F32)

    def attend(g, k, v, allowed, feature_major):
        m_scr[g], l_scr[g], acc_scr[g] = _online_softmax_step(
            scores(g, k, feature_major), allowed, v, m_scr[g], l_scr[g], acc_scr[g], feature_major)

    k_t = jnp.concatenate([r[...].astype(BF16) for r in kp], axis=1)
    v_t = jnp.concatenate([r[...].astype(BF16) for r in vp], axis=1)
    key = j * n_keys + lax.broadcasted_iota(jnp.int32, (1, n_keys), 1)
    expand = jnp.where(blk == key // SEL_BLOCK, 1.0, 0.0).astype(BF16)
    for g in range(NSA_KV_HEADS):
        chosen = jnp.dot(sel_ref[g].astype(BF16), expand, preferred_element_type=F32)
        attend(g, k_t, v_t, _group_rows(chosen > 0.5) & (key <= pos_q), True)

    @pl.when(j == pl.num_programs(1) - 1)
    def _():
        lane128 = lax.broadcasted_iota(jnp.int32, (1, LANES), 1)
        n_new = ksn_ref.shape[0]
        new_idx = lax.broadcasted_iota(jnp.int32, (1, n_new), 1)
        new_key = past + new_idx
        new_blk = jnp.where(blk == new_key // SEL_BLOCK, 1.0, 0.0).astype(BF16)
        n_win = wk_ref.shape[1]
        k_wc, v_wc = wk_ref[...].astype(BF16), wv_ref[...].astype(BF16)
        pos_wc = past - n_win + lax.broadcasted_iota(jnp.int32, (1, n_win), 1)
        ok_wc = (pos_q - pos_wc >= 0) & (pos_q - pos_wc < WINDOW) & (pos_wc >= 0)
        ok_wn = (pos_q - new_key >= 0) & (pos_q - new_key < WINDOW) & (new_idx < nq)
        rows_all = NSA_GROUP * nq
        gexp = jnp.dot(g_ref[...], ex_ref[...], precision=HIGHEST, preferred_element_type=F32)
        o_ref[...] = gexp[:, :NSA_QD] * oc_ref[...]
        for g in range(NSA_KV_HEADS):
            m, half = divmod(g, 2)
            in_half = (lane128 // NSA_HD) == half
            mcols = slice(m * LANES, (m + 1) * LANES)
            chosen = jnp.dot(sel_ref[g].astype(BF16), new_blk, preferred_element_type=F32)
            attend(g, ksn_ref[...], vsn_ref[...],
                   _group_rows(chosen > 0.5) & (new_key <= pos_q) & (new_idx < nq), False)
            o_s = acc_scr[g][:, mcols] / jnp.maximum(l_scr[g], 1e-30)
            win = (jnp.full((rows_all, 1), -jnp.inf, F32), jnp.zeros((rows_all, 1), F32),
                   jnp.zeros((rows_all, NSA_KVD), F32))
            win = _online_softmax_step(scores(g, k_wc, True), ok_wc, v_wc, *win, True)
            win = _online_softmax_step(scores(g, kwn_ref[...], False), ok_wn, vwn_ref[...], *win, False)
            o_w = win[2][:, mcols] / jnp.maximum(win[1], 1e-30)
            for r in range(NSA_GROUP):
                t = m * NSA_GROUP + r
                cols = slice(t * LANES, (t + 1) * LANES)
                rows = slice(r * nq, (r + 1) * nq)
                comb = (gexp[:, NSA_QD + t * LANES:NSA_QD + (t + 1) * LANES] * o_s[rows]
                        + gexp[:, 2 * NSA_QD + t * LANES:2 * NSA_QD + (t + 1) * LANES] * o_w[rows])
                o_ref[:, cols] += jnp.where(in_half, comb, 0.0)


def nsa_decode_attend(page_table, q, sel, gates, o_c, ks_new, vs_new, wk, wv, kw_new, vw_new, pool_k, pool_v,
                      db, nq, past):
    n_pages = page_table.shape[1]
    page = pool_k.shape[2]
    ex = jnp.asarray(_nsa_gate_expand())
    n_win = wk.shape[2]
    rows = NSA_GROUP * nq
    per_b = lambda shape: pl.BlockSpec((None,) + shape, lambda bi, j, pt: (bi,) + (0,) * len(shape))
    q_rows = lambda width: pl.BlockSpec((nq, width), lambda bi, j, pt: (bi, 0))
    new_rows = ks_new.shape[1]
    return pl.pallas_call(
        functools.partial(_nsa_decode_attend_kernel, past),
        out_shape=jax.ShapeDtypeStruct((db * nq, NSA_QD), F32),
        grid_spec=pltpu.PrefetchScalarGridSpec(
            num_scalar_prefetch=1,
            grid=(db, n_pages // PAGES_PER_STEP),
            in_specs=[q_rows(NSA_QD), per_b(sel.shape[1:]), q_rows(LANES), q_rows(NSA_QD),
                      per_b((new_rows, NSA_KVD)), per_b((new_rows, NSA_KVD)),
                      per_b((NSA_KVD, n_win)), per_b((NSA_KVD, n_win)),
                      per_b((new_rows, NSA_KVD)), per_b((new_rows, NSA_KVD)),
                      pl.BlockSpec(ex.shape, lambda bi, j, pt: (0, 0))]
            + _page_specs(PAGES_PER_STEP, page) * 2,
            out_specs=q_rows(NSA_QD),
            scratch_shapes=[pltpu.VMEM((NSA_KV_HEADS, rows, 1), F32), pltpu.VMEM((NSA_KV_HEADS, rows, 1), F32),
                            pltpu.VMEM((NSA_KV_HEADS, rows, NSA_KVD), F32)],
        ),
        compiler_params=_params("parallel", "arbitrary"),
        name="nsa_decode_attend",
    )(page_table, q, sel, gates, o_c, ks_new, vs_new, wk, wv, kw_new, vw_new, ex,
      *([pool_k] * PAGES_PER_STEP), *([pool_v] * PAGES_PER_STEP))


def _nsa_weights(w_in, w_out):
    perm = _nsa_q_perm()
    return jnp.concatenate([w_in[:, perm], w_in[:, NSA_QD:]], axis=1), w_out[perm]


def nsa_prompt(x, scale, shift, w_in_p, pe_k, pe_v, b, s):
    p, _ = project(x, scale, shift, w_in_p, b, s)
    cos, up, dn = _rope_tables(jnp.arange(s, dtype=jnp.int32))
    q, kc, vc, ks, vs, kw, vw, ksb, vsb, kwb, vwb, gates = nsa_prep(p, cos, up, dn, q_dtype=BF16)
    kcmp = nsa_compress(kc, pe_k, b, s)
    vcmp = nsa_compress(vc, pe_v, b, s)
    o = nsa_attn_prompt(q, gates, kcmp, vcmp, ksb, vsb, kwb, vwb, b, s)
    nw = min(WINDOW, s)
    rs = lambda a: a.reshape(b, s, NSA_KV_HEADS, NSA_HD)
    return o, (rs(kc), rs(vc), rs(ks), rs(vs), rs(kw)[:, s - nw:], rs(vw)[:, s - nw:])


def nsa_sample(x, scale, shift, ck, cv, sk, sv, wk, wv, page_table, w_in_p, pe_k, pe_v, db, nq):
    past = page_table.shape[1] * ck.shape[1]
    p, _ = project(x, scale, shift, w_in_p, db, nq)
    pos = past + jnp.tile(jnp.arange(nq, dtype=jnp.int32), db)
    cos, up, dn = _rope_tables(pos)
    q, kc, vc, ks, vs, kw, vw, ksb, vsb, kwb, vwb, gates = nsa_prep(p, cos, up, dn, q_dtype=F32)
    sums = nsa_page_sums(page_table, _feature_major(ck), _feature_major(cv), pe_k, pe_v)
    sel, o_c = nsa_decode_select(q, sums, kc, vc, pe_k, pe_v, db, nq, past)
    new_pad = lambda a: jnp.pad(a.reshape(db, nq, NSA_KVD), ((0, 0), (0, LANES - nq), (0, 0)))
    o = nsa_decode_attend(page_table, q, sel, gates, o_c, new_pad(ksb), new_pad(vsb),
                          _feature_major(wk), _feature_major(wv), new_pad(kwb), new_pad(vwb),
                          _feature_major(sk), _feature_major(sv), db, nq, past)
    rs = lambda a: a.reshape(db, nq, NSA_KV_HEADS, NSA_HD)
    slide = lambda cache, new: jnp.concatenate([cache, rs(new).astype(cache.dtype)], 1)[:, nq:]
    return o, (rs(kc), rs(vc), rs(ks), rs(vs), slide(wk, kw), slide(wv, vw))


def _top_values(s, k):
    vals = []
    for _ in range(k):
        m = jnp.max(s, axis=0, keepdims=True)
        vals.append(m)
        s = jnp.where(s >= m, -jnp.inf, s)
    return vals


def _peer_route_kernel(qv_ref, keys_ref, s1_ref, s2_ref, e1_ref, e2_ref, th_ref):
    for h in range(PEER_HEADS):
        q1 = qv_ref[:, (2 * h) * PEER_DKEY:(2 * h + 1) * PEER_DKEY]
        q2 = qv_ref[:, (2 * h + 1) * PEER_DKEY:(2 * h + 2) * PEER_DKEY]
        s1 = lax.dot_general(keys_ref[h, 0], q1, NT, precision=HIGHEST, preferred_element_type=F32)
        s2 = lax.dot_general(keys_ref[h, 1], q2, NT, precision=HIGHEST, preferred_element_type=F32)
        top1 = _top_values(s1, PEER_TOPK)
        top2 = _top_values(s2, PEER_TOPK)
        t1 = jnp.concatenate(top1, axis=0)
        t2 = jnp.concatenate(top2, axis=0)
        row8 = lax.broadcasted_iota(jnp.int32, (SUBLANES, 1), 0)
        pairs = [top1[0] + t2, top1[1] + t2[:SUBLANES], t1[SUBLANES:] + top2[0]]
        for i in range(2, SUBLANES):
            pairs.append(jnp.where(row8 < PEER_TOPK // (i + 1), top1[i] + t2[:SUBLANES], -jnp.inf))
        best = _top_values(jnp.concatenate(pairs, axis=0), PEER_TOPK)
        z = best[0] * 0.0
        for v in best:
            z = z + jnp.exp(v - best[0])
        e1 = jnp.exp(s1 - top1[0])
        e2 = jnp.exp(s2 - top2[0]) / z
        for c in range(qv_ref.shape[0] // LANES):
            cols = slice(c * LANES, (c + 1) * LANES)
            s1_ref[h, c] = s1[:, cols]
            s2_ref[h, c] = s2[:, cols]
            e1_ref[h, c] = e1[:, cols]
            e2_ref[h, c] = e2[:, cols]
            th_ref[h, c] = best[PEER_TOPK - 1][:, cols]


def peer_route(qv, keys, *, tm):
    t = qv.shape[0]
    nc = tm // LANES
    big = jax.ShapeDtypeStruct((PEER_HEADS, t // LANES, PEER_NKEYS, LANES), F32)
    big_spec = pl.BlockSpec((PEER_HEADS, nc, PEER_NKEYS, LANES), lambda i: (0, i, 0, 0))
    return pl.pallas_call(
        _peer_route_kernel,
        out_shape=(big, big, big, big, jax.ShapeDtypeStruct((PEER_HEADS, t // LANES, 1, LANES), F32)),
        grid=(t // tm,),
        in_specs=[
            pl.BlockSpec((tm, 2 * PEER_HEADS * PEER_DKEY), lambda i: (i, 0)),
            pl.BlockSpec(keys.shape, lambda i: (0, 0, 0, 0)),
        ],
        out_specs=(big_spec, big_spec, big_spec, big_spec,
                   pl.BlockSpec((PEER_HEADS, nc, 1, LANES), lambda i: (0, i, 0, 0))),
        compiler_params=_params("parallel"),
        name="peer_route",
    )(qv, keys)


PEER_A_PER_STEP = SUBLANES
PEER_ROUTE_TILE = 256


def _gelu_tanh(x):
    return 0.5 * x * (1.0 + jnp.tanh(math.sqrt(2.0 / math.pi) * (x + 0.044715 * (x * x * x))))


def _peer_dense_kernel(h_ref, s1_ref, e1_ref, s2_ref, e2_ref, th_ref, u_ref, vt_ref, o_ref,
                       act_ref, g_ref, acc_ref):
    j = pl.program_id(1)
    tm = h_ref.shape[0]

    @pl.when(j == 0)
    def _():
        acc_ref[...] = jnp.zeros_like(acc_ref)

    nc = tm // LANES
    act = lax.dot_general(u_ref[...], h_ref[...], NT, preferred_element_type=F32)
    for c in range(nc):
        act_ref[c] = act[:, c * LANES:(c + 1) * LANES]

    def tile(idx, carry):
        a, c = idx // nc, idx % nc
        rows = pl.ds(pl.multiple_of(a * PEER_NKEYS, PEER_NKEYS), PEER_NKEYS)
        w = jnp.zeros((PEER_NKEYS, LANES), F32)
        for h in range(PEER_HEADS):
            cand = s2_ref[h, c] + s1_ref[h, c, pl.ds(a, 1), :]
            w = w + e1_ref[h, c, pl.ds(a, 1), :] * jnp.where(cand >= th_ref[h, c], e2_ref[h, c], 0.0)
        g_ref[c, rows, :] = (w * _gelu_tanh(act_ref[c, rows, :])).astype(BF16)
        return carry

    lax.fori_loop(0, PEER_A_PER_STEP * nc, tile, 0)
    g = jnp.concatenate([g_ref[c] for c in range(nc)], axis=1)
    acc_ref[...] += jnp.dot(vt_ref[...], g, preferred_element_type=F32)

    @pl.when(j == pl.num_programs(1) - 1)
    def _():
        o_ref[...] = acc_ref[...].T


def peer_dense(h, s1, s2, e1, e2, th, u, vt, *, tm):
    t, d = h.shape
    ne = PEER_A_PER_STEP * PEER_NKEYS
    nc = tm // LANES
    tok = pl.BlockSpec((PEER_HEADS, nc, PEER_NKEYS, LANES), lambda i, j: (0, i, 0, 0))
    arow = pl.BlockSpec((PEER_HEADS, nc, PEER_A_PER_STEP, LANES), lambda i, j: (0, i, j, 0))
    return pl.pallas_call(
        _peer_dense_kernel,
        out_shape=jax.ShapeDtypeStruct((t, d), F32),
        grid=(t // tm, PEER_NKEYS // PEER_A_PER_STEP),
        in_specs=[
            pl.BlockSpec((tm, d), lambda i, j: (i, 0)),
            arow, arow, tok, tok,
            pl.BlockSpec((PEER_HEADS, nc, 1, LANES), lambda i, j: (0, i, 0, 0)),
            pl.BlockSpec((ne, d), lambda i, j: (j, 0)),
            pl.BlockSpec((d, ne), lambda i, j: (0, j)),
        ],
        out_specs=pl.BlockSpec((tm, d), lambda i, j: (i, 0)),
        scratch_shapes=[pltpu.VMEM((nc, ne, LANES), F32), pltpu.VMEM((nc, ne, LANES), BF16),
                        pltpu.VMEM((d, tm), F32)],
        compiler_params=_params("parallel", "arbitrary"),
        name="peer_dense",
    )(h, s1, e1, s2, e2, th, u, vt)


def peer_ffn(x, scale, shift, wq, keys, u_bf, vt_bf, b, s):
    qv, h = project(x, scale, shift, wq, b, s)
    tm = _row_tile(b * s)
    s1, s2, e1, e2, th = peer_route(qv, keys, tm=min(tm, PEER_ROUTE_TILE))
    return peer_dense(h, s1, s2, e1, e2, th, u_bf, vt_bf, tm=tm)


def kernel(x_prompt, x_sample, state_gla, cache_cmp_k, cache_cmp_v, cache_sel_k, cache_sel_v, cache_win_k, cache_win_v, page_table, c_prompt, c_sample, ada_w, ada_b, ln_g, ln_b, gla_w_in, gla_w_a2, gla_b_a2, gla_gn, gla_w_out, nsa_w_in, nsa_pe_k, nsa_pe_v, nsa_w_out, peer_wq, peer_keys, peer_u, peer_v):
    bp, sp, d = x_prompt.shape
    bs, ss, _ = x_sample.shape
    groups = ((bp, sp), (bs, ss))
    c_all = jnp.concatenate([c_prompt, c_sample], axis=0)
    c_rows = _round_up(bp + bs, SUBLANES)
    mod = ada_mod(jnp.pad(c_all, ((0, c_rows - bp - bs), (0, 0))), ada_w, ada_b)
    ys = [x_prompt.reshape(bp * sp, d), x_sample.reshape(bs * ss, d)]
    gla_states, nsa_rows = ([], []), ([], [])
    for i in range(DEPTH):
        row0 = (0, bp)
        mods = [[mod[i, row0[n]:row0[n] + b, k * d:(k + 1) * d][:, None, :] for k in range(6)]
                for n, (b, _) in enumerate(groups)]
        j = i // N_MIXERS
        if i % N_MIXERS == 0:
            for n, (b, s) in enumerate(groups):
                shift, scale, gate = mods[n][:3]
                proj, _ = project(ys[n], scale, shift, gla_w_in[j], b, s)
                o, st = gla_recurrence(proj, gla_w_a2[j], gla_b_a2[j], None if n == 0 else state_gla[j], b, s)
                gla_states[n].append(st.astype(state_gla.dtype))
                ys[n] = post(ys[n], gate, ln_g[i, 0], ln_b[i, 0], b, s, sub=o, w_out=gla_w_out[j],
                             gla_proj=proj, gla_gn=gla_gn[j])
        else:
            w_in_p, w_out_p = _nsa_weights(nsa_w_in[j], nsa_w_out[j])
            for n, (b, s) in enumerate(groups):
                shift, scale, gate = mods[n][:3]
                if n == 0:
                    o, rows = nsa_prompt(ys[n], scale, shift, w_in_p, nsa_pe_k[j], nsa_pe_v[j], b, s)
                else:
                    o, rows = nsa_sample(ys[n], scale, shift, cache_cmp_k[j], cache_cmp_v[j], cache_sel_k[j],
                                         cache_sel_v[j], cache_win_k[j], cache_win_v[j], page_table,
                                         w_in_p, nsa_pe_k[j], nsa_pe_v[j], b, s)
                nsa_rows[n].append(rows)
                ys[n] = post(ys[n], gate, ln_g[i, 0], ln_b[i, 0], b, s, sub=o, w_out=w_out_p)
        u_bf = peer_u[i].astype(BF16)
        vt_bf = peer_v[i].astype(BF16).T
        for n, (b, s) in enumerate(groups):
            shift, scale, gate = mods[n][3:]
            f = peer_ffn(ys[n], scale, shift, peer_wq[i], peer_keys[i], u_bf, vt_bf, b, s)
            ys[n] = post(ys[n], gate, ln_g[i, 1], ln_b[i, 1], b, s, sub=f)

    st = lambda ts, k: jnp.stack([t[k] for t in ts])
    return (ys[0].reshape(bp, sp, d), ys[1].reshape(bs, ss, d),
            jnp.stack(gla_states[0]), jnp.stack(gla_states[1]),
            *(st(nsa_rows[0], k) for k in range(6)), *(st(nsa_rows[1], k) for k in range(6)))
```

```python
import functools
import math

import jax
import jax.numpy as jnp
import numpy as np
from jax import lax
from jax.experimental import pallas as pl
from jax.experimental.pallas import tpu as pltpu

D_MODEL = 1024
DEPTH = 2
N_MIXERS = 2
DN_ALPHA = (2.0 * DEPTH) ** 0.25
LN_EPS = 1e-5
F32 = jnp.float32
BF16 = jnp.bfloat16
HIGHEST = lax.Precision.HIGHEST

GLA_HEADS = 4
GLA_DK = D_MODEL // 2 // GLA_HEADS
GLA_DV = D_MODEL // GLA_HEADS
GLA_TAU = 16.0
GLA_CHUNK = 64
GLA_HK = GLA_HEADS * GLA_DK
GLA_HV = GLA_HEADS * GLA_DV

NSA_HEADS = 16
NSA_KV_HEADS = 4
NSA_GROUP = NSA_HEADS // NSA_KV_HEADS
NSA_HD = D_MODEL // NSA_HEADS
NSA_QD = NSA_HEADS * NSA_HD
NSA_KVD = NSA_KV_HEADS * NSA_HD
CMP_LEN = 32
CMP_STRIDE = 16
SEL_BLOCK = 64
N_SEL = 16
WINDOW = 512
Q_BLOCK = 128
FORCE_SCORE = 1e6
ROT_DIM = NSA_HD // 4
ROPE_THETA = 500000.0

PEER_HEADS = 8
PEER_NKEYS = 128
PEER_DKEY = 128
PEER_TOPK = 16

LANES = 128
SUBLANES = 8
VMEM_LIMIT_BYTES = 56 * 1024 * 1024
ROW_TILE = 512

NT = (((1,), (1,)), ((), ()))
TN = (((0,), (0,)), ((), ()))


def _round_up(n, m):
    return -(-n // m) * m


def _params(*sem):
    return pltpu.CompilerParams(dimension_semantics=sem, vmem_limit_bytes=VMEM_LIMIT_BYTES)


def _row_tile(t):
    return ROW_TILE if t % ROW_TILE == 0 else t


def _mod_operand(m, b, s, tm):
    d = m.shape[-1]
    if s % tm == 0:
        return m, pl.BlockSpec((None, 1, d), lambda i, *_: (i * tm // s, 0, 0))
    rows = jnp.broadcast_to(m, (b, s, d)).reshape(b * s, d)
    return rows, pl.BlockSpec((tm, d), lambda i, *_: (i, 0))


def _ada_kernel(c_ref, w_ref, b_ref, o_ref):
    c = c_ref[...]
    act = (c * jax.nn.sigmoid(c)).astype(BF16)
    o_ref[...] = jnp.dot(act, w_ref[...].astype(BF16), preferred_element_type=F32) + b_ref[...]


def ada_mod(c, w, b):
    r, d = c.shape
    nl, _, n = w.shape
    tn = n // 4
    return pl.pallas_call(
        _ada_kernel,
        out_shape=jax.ShapeDtypeStruct((nl, r, n), F32),
        grid=(nl, n // tn),
        in_specs=[pl.BlockSpec((r, d), lambda l, j: (0, 0)),
                  pl.BlockSpec((None, d, tn), lambda l, j: (l, 0, j)),
                  pl.BlockSpec((None, 1, tn), lambda l, j: (l, 0, j))],
        out_specs=pl.BlockSpec((None, r, tn), lambda l, j: (l, 0, j)),
        compiler_params=_params("parallel", "parallel"),
        name="ada_mod",
    )(c, w, b.reshape(nl, 1, n))


def _mod_matmul_kernel(x_ref, sc_ref, sh_ref, w_ref, o_ref, h_ref):
    @pl.when(pl.program_id(1) == 0)
    def _():
        h = x_ref[...] * (1.0 + sc_ref[...]) + sh_ref[...]
        h_ref[...] = h.astype(BF16)

    o_ref[...] = jnp.dot(h_ref[...], w_ref[...], preferred_element_type=F32)


def project(x, scale, shift, w, b, s):
    t, d = x.shape
    npad = _round_up(w.shape[1], LANES)
    wp = jnp.pad(w, ((0, 0), (0, npad - w.shape[1]))).astype(BF16)
    tn = next(c for c in (1024, 768, 640, 512, 384, 256, 128) if npad % c == 0)
    tm = _row_tile(t)
    sc, mod_spec = _mod_operand(scale, b, s, tm)
    sh, _ = _mod_operand(shift, b, s, tm)
    return pl.pallas_call(
        _mod_matmul_kernel,
        out_shape=(jax.ShapeDtypeStruct((t, npad), F32), jax.ShapeDtypeStruct((t, d), BF16)),
        grid=(t // tm, npad // tn),
        in_specs=[pl.BlockSpec((tm, d), lambda i, j: (i, 0)), mod_spec, mod_spec,
                  pl.BlockSpec((d, tn), lambda i, j: (0, j))],
        out_specs=(pl.BlockSpec((tm, tn), lambda i, j: (i, j)), pl.BlockSpec((tm, d), lambda i, j: (i, 0))),
        compiler_params=_params("parallel", "arbitrary"),
        name="mod_matmul",
    )(x, sc, sh, wp)


def _deepnorm(x, sub, gate, g, b):
    y = DN_ALPHA * x + (1.0 + gate) * sub
    mu = jnp.mean(y, axis=-1, keepdims=True)
    var = jnp.mean(jnp.square(y - mu), axis=-1, keepdims=True)
    return (y - mu) * lax.rsqrt(var + LN_EPS) * g + b


def _post_gla_kernel(o_ref, r_ref, x_ref, gate_ref, gn_ref, w_ref, g_ref, b_ref, y_ref):
    parts = []
    for h in range(GLA_HEADS):
        cols = slice(h * GLA_DV, (h + 1) * GLA_DV)
        o = o_ref[:, cols]
        mu = jnp.mean(o, axis=-1, keepdims=True)
        var = jnp.mean(jnp.square(o - mu), axis=-1, keepdims=True)
        parts.append((o - mu) * lax.rsqrt(var + LN_EPS) * gn_ref[:, cols])
    r = r_ref[...]
    f = (jnp.concatenate(parts, axis=1) * (r * jax.nn.sigmoid(r))).astype(BF16)
    sub = jnp.dot(f, w_ref[...], preferred_element_type=F32)
    y_ref[...] = _deepnorm(x_ref[...], sub, gate_ref[...], g_ref[...], b_ref[...])


def _post_matmul_kernel(o_ref, x_ref, gate_ref, w_ref, g_ref, b_ref, y_ref):
    sub = jnp.dot(o_ref[...].astype(BF16), w_ref[...], preferred_element_type=F32)
    y_ref[...] = _deepnorm(x_ref[...], sub, gate_ref[...], g_ref[...], b_ref[...])


def _post_plain_kernel(o_ref, x_ref, gate_ref, g_ref, b_ref, y_ref):
    y_ref[...] = _deepnorm(x_ref[...], o_ref[...], gate_ref[...], g_ref[...], b_ref[...])


def post(x, gate, ln_g, ln_b, b, s, *, sub=None, w_out=None, gla_proj=None, gla_gn=None):
    t, d = x.shape
    tm = _row_tile(t)
    gate_arr, gate_spec = _mod_operand(gate, b, s, tm)
    row = pl.BlockSpec((tm, d), lambda i: (i, 0))
    vec = pl.BlockSpec((1, d), lambda i: (0, 0))
    mat = pl.BlockSpec((d, d), lambda i: (0, 0))
    g2, b2 = ln_g.reshape(1, d), ln_b.reshape(1, d)
    if gla_proj is not None:
        r_spec = pl.BlockSpec((tm, GLA_HV), lambda i: (i, (2 * GLA_HK + GLA_HV) // GLA_HV))
        args = (sub, gla_proj, x, gate_arr, gla_gn.reshape(1, d), w_out.astype(BF16), g2, b2)
        specs = [row, r_spec, row, gate_spec, vec, mat, vec, vec]
        body = _post_gla_kernel
    elif w_out is not None:
        args = (sub, x, gate_arr, w_out.astype(BF16), g2, b2)
        specs = [row, row, gate_spec, mat, vec, vec]
        body = _post_matmul_kernel
    else:
        args = (sub, x, gate_arr, g2, b2)
        specs = [row, row, gate_spec, vec, vec]
        body = _post_plain_kernel
    return pl.pallas_call(
        body,
        out_shape=jax.ShapeDtypeStruct((t, d), F32),
        grid=(t // tm,),
        in_specs=specs,
        out_specs=row,
        compiler_params=_params("parallel"),
        name="post",
    )(*args)


GLA_MIN_ROWS = 64


def _cumsum_rows(x):
    n = x.shape[0]
    row = lax.broadcasted_iota(jnp.int32, (n, 1), 0)
    shift = 1
    while shift < n:
        x = x + jnp.where(row >= shift, pltpu.roll(x, shift, 0), 0.0)
        shift *= 2
    return x


def _gla_kernel(has_s0, q_ref, k_ref, v_ref, a_ref, wa_ref, ba_ref, *rest):
    if has_s0:
        s0_ref, o_ref, st_ref, s_scr = rest
    else:
        o_ref, st_ref, s_scr = rest
    c = pl.program_id(1)
    n_rows = q_ref.shape[0]
    rows = max(n_rows, GLA_MIN_ROWS)

    @pl.when(c == 0)
    def _():
        s_scr[...] = s0_ref[...] if has_s0 else jnp.zeros_like(s_scr)

    def padded(x):
        if rows == n_rows:
            return x
        return jnp.concatenate([x, jnp.zeros((rows - n_rows, x.shape[1]), x.dtype)], axis=0)

    z = jnp.dot(a_ref[...], wa_ref[...], precision=HIGHEST, preferred_element_type=F32) + ba_ref[...]
    logg = (jnp.minimum(z, 0.0) - jnp.log1p(jnp.exp(-jnp.abs(z)))) / GLA_TAU
    b = _cumsum_rows(padded(logg))
    q, k, v = padded(q_ref[...]), padded(k_ref[...]), padded(v_ref[...])
    qe = q * (GLA_DK ** -0.5) * jnp.exp(b)
    ke = k * jnp.exp(-b)
    b_last = b[rows - 1:rows, :]
    kd = k * jnp.exp(b_last - b)
    e_last = jnp.exp(b_last)
    causal = lax.broadcasted_iota(jnp.int32, (rows, 1), 0) >= lax.broadcasted_iota(jnp.int32, (1, rows), 1)
    eye = lax.broadcasted_iota(jnp.int32, (GLA_DK, 1), 0) == lax.broadcasted_iota(jnp.int32, (1, GLA_DK), 1)
    for h in range(GLA_HEADS):
        ck = slice(h * GLA_DK, (h + 1) * GLA_DK)
        cv = slice(h * GLA_DV, (h + 1) * GLA_DV)
        att = lax.dot_general(qe[:, ck], ke[:, ck], NT, precision=HIGHEST, preferred_element_type=F32)
        att = jnp.where(causal, att, 0.0)
        s_h = s_scr[h]
        o = (jnp.dot(att, v[:, cv], precision=HIGHEST, preferred_element_type=F32)
             + jnp.dot(qe[:, ck], s_h, precision=HIGHEST, preferred_element_type=F32))
        o_ref[:, cv] = o[:n_rows]
        e_col = jnp.sum(jnp.where(eye, e_last[:, ck], 0.0), axis=1, keepdims=True)
        s_scr[h] = e_col * s_h + lax.dot_general(kd[:, ck], v[:, cv], TN, precision=HIGHEST,
                                                 preferred_element_type=F32)

    @pl.when(c == pl.num_programs(1) - 1)
    def _():
        st_ref[...] = s_scr[...]


def gla_recurrence(proj, w_a2, b_a2, s0, b, s):
    chunk = GLA_CHUNK if s % GLA_CHUNK == 0 else s
    nc = s // chunk
    wa = jnp.pad(w_a2, ((0, LANES - w_a2.shape[0]), (0, 0)))
    a_block = (2 * GLA_HK + 2 * GLA_HV) // LANES
    row = lambda width, blk: pl.BlockSpec((chunk, width), lambda bi, c: (bi * nc + c, blk))
    st_spec = pl.BlockSpec((None, GLA_HEADS, GLA_DK, GLA_DV), lambda bi, c: (bi, 0, 0, 0))
    in_specs = [row(GLA_HK, 0), row(GLA_HK, 1), row(GLA_HV, 2 * GLA_HK // GLA_HV), row(LANES, a_block),
                pl.BlockSpec(wa.shape, lambda bi, c: (0, 0)), pl.BlockSpec((1, GLA_HK), lambda bi, c: (0, 0))]
    args = [proj, proj, proj, proj, wa, b_a2.reshape(1, GLA_HK)]
    if s0 is not None:
        in_specs.append(st_spec)
        args.append(s0)
    return pl.pallas_call(
        functools.partial(_gla_kernel, s0 is not None),
        out_shape=(jax.ShapeDtypeStruct((b * s, GLA_HV), F32),
                   jax.ShapeDtypeStruct((b, GLA_HEADS, GLA_DK, GLA_DV), F32)),
        grid=(b, nc),
        in_specs=in_specs,
        out_specs=(pl.BlockSpec((chunk, GLA_HV), lambda bi, c: (bi * nc + c, 0)), st_spec),
        scratch_shapes=[pltpu.VMEM((GLA_HEADS, GLA_DK, GLA_DV), F32)],
        compiler_params=_params("parallel", "arbitrary"),
        name="gla_recurrence",
    )(*args)


def _nsa_q_perm():
    cols = []
    for m in range(NSA_KV_HEADS // 2):
        for r in range(NSA_GROUP):
            for half in range(2):
                head = (2 * m + half) * NSA_GROUP + r
                cols.append(np.arange(NSA_HD) + head * NSA_HD)
    return np.concatenate(cols)


def _nsa_gate_expand():
    perm = _nsa_q_perm()
    ex = np.zeros((LANES, 3 * NSA_QD), np.float32)
    for br in range(3):
        for col in range(NSA_QD):
            ex[br * NSA_HEADS + perm[col] // NSA_HD, br * NSA_QD + col] = 1.0
    return ex


def _nsa_slc_matrix(nc_pad, nsb_pad):
    ratio = SEL_BLOCK // CMP_STRIDE
    m = np.zeros((nc_pad, nsb_pad), np.float32)
    for j in range(nsb_pad):
        for o in range(CMP_LEN // CMP_STRIDE):
            for i in range(ratio):
                n = ratio * j + i + o - (CMP_LEN // CMP_STRIDE - 1)
                if 0 <= n < nc_pad:
                    m[n, j] += 1.0
    return m


def _rope_tables(pos):
    half = ROT_DIM // 2
    inv = ROPE_THETA ** (-jnp.arange(half, dtype=F32) * 2.0 / ROT_DIM)
    ang = pos.astype(F32)[:, None] * inv[None, :]
    cos, sin = jnp.cos(ang), jnp.sin(ang)
    t = pos.shape[0]
    ones = jnp.ones((t, NSA_HD - ROT_DIM), F32)
    zeros = jnp.zeros((t, NSA_HD - ROT_DIM), F32)
    z8 = jnp.zeros((t, half), F32)
    c = jnp.concatenate([cos, cos, ones], 1)
    up = jnp.concatenate([-sin, z8, zeros], 1)
    dn = jnp.concatenate([z8, sin, zeros], 1)
    two = lambda a: jnp.concatenate([a, a], 1)
    return two(c), two(up), two(dn)


def _nsa_prep_kernel(p_ref, cos_ref, up_ref, dn_ref, q_ref, kc_ref, vc_ref, ks_ref, vs_ref, kw_ref, vw_ref,
                     ksb_ref, vsb_ref, kwb_ref, vwb_ref, g_ref):
    def rope(x):
        reps = x.shape[1] // LANES
        tile = lambda a: jnp.concatenate([a] * reps, axis=1)
        w = x.shape[1]
        return (x * tile(cos_ref[...]) + pltpu.roll(x, w - ROT_DIM // 2, 1) * tile(up_ref[...])
                + pltpu.roll(x, ROT_DIM // 2, 1) * tile(dn_ref[...]))

    q_ref[...] = (rope(p_ref[:, :NSA_QD]) * (NSA_HD ** -0.5)).astype(q_ref.dtype)
    kv = lambda k: p_ref[:, NSA_QD + k * NSA_KVD:NSA_QD + (k + 1) * NSA_KVD]
    kc_ref[...] = rope(kv(0))
    vc_ref[...] = kv(1)
    ks = rope(kv(2))
    ks_ref[...] = ks
    ksb_ref[...] = ks.astype(BF16)
    vs_ref[...] = kv(3)
    vsb_ref[...] = kv(3).astype(BF16)
    kw = rope(kv(4))
    kw_ref[...] = kw
    kwb_ref[...] = kw.astype(BF16)
    vw_ref[...] = kv(5)
    vwb_ref[...] = kv(5).astype(BF16)
    g0 = NSA_QD + 6 * NSA_KVD
    g_ref[...] = jax.nn.sigmoid(p_ref[:, g0:g0 + LANES])


def nsa_prep(p, cos, up, dn, *, q_dtype):
    t = p.shape[0]
    tm = _row_tile(t)
    period = cos.shape[0] // tm
    tab = pl.BlockSpec((tm, LANES), lambda i: (i % period, 0))
    kv32 = jax.ShapeDtypeStruct((t, NSA_KVD), F32)
    kv16 = jax.ShapeDtypeStruct((t, NSA_KVD), BF16)
    kvs = pl.BlockSpec((tm, NSA_KVD), lambda i: (i, 0))
    return pl.pallas_call(
        _nsa_prep_kernel,
        out_shape=(jax.ShapeDtypeStruct((t, NSA_QD), q_dtype),) + (kv32,) * 6 + (kv16,) * 4
        + (jax.ShapeDtypeStruct((t, LANES), F32),),
        grid=(t // tm,),
        in_specs=[pl.BlockSpec((tm, p.shape[1]), lambda i: (i, 0)), tab, tab, tab],
        out_specs=(pl.BlockSpec((tm, NSA_QD), lambda i: (i, 0)),) + (kvs,) * 10
        + (pl.BlockSpec((tm, LANES), lambda i: (i, 0)),),
        compiler_params=_params("parallel"),
        name="nsa_prep",
    )(p, cos, up, dn)


def _block_sums(x, w):
    n_sub = x.shape[0] // CMP_STRIDE
    x = x.reshape(n_sub, CMP_STRIDE, NSA_KVD)
    w = 1.0 + w
    return jnp.sum(x * w[None, :CMP_STRIDE], axis=1), jnp.sum(x * w[None, CMP_STRIDE:], axis=1)


def _nsa_compress_kernel(x_ref, w_ref, o_ref):
    first, second = _block_sums(x_ref[...], w_ref[...])
    n_sub = first.shape[0]
    nxt = pltpu.roll(second, n_sub - 1, 0)
    row = lax.broadcasted_iota(jnp.int32, (n_sub, 1), 0)
    o_ref[...] = jnp.where(row < n_sub - 1, (first + nxt) / CMP_LEN, 0.0).astype(o_ref.dtype)


def nsa_compress(x, pe, b, s):
    n_sub = s // CMP_STRIDE
    return pl.pallas_call(
        _nsa_compress_kernel,
        out_shape=jax.ShapeDtypeStruct((b, n_sub, NSA_KVD), BF16),
        grid=(b,),
        in_specs=[pl.BlockSpec((s, NSA_KVD), lambda i: (i, 0)),
                  pl.BlockSpec((CMP_LEN, NSA_KVD), lambda i: (0, 0))],
        out_specs=pl.BlockSpec((None, n_sub, NSA_KVD), lambda i: (i, 0, 0)),
        compiler_params=_params("parallel"),
        name="nsa_compress",
    )(x, pe.reshape(CMP_LEN, NSA_KVD))


NSA_KV_TILE = 512


def _mask_rows(s, allowed):
    if allowed.shape[0] == s.shape[0]:
        return jnp.where(allowed, s, -jnp.inf)
    reps = s.shape[0] // allowed.shape[0]
    s3 = s.reshape(reps, allowed.shape[0], s.shape[1])
    return jnp.where(allowed[None], s3, -jnp.inf).reshape(s.shape)


def _softmax_rows(s, allowed):
    s = _mask_rows(s, allowed)
    m = jnp.max(s, axis=1, keepdims=True)
    m = jnp.where(m == -jnp.inf, 0.0, m)
    e = jnp.exp(s - m)
    return e / jnp.maximum(jnp.sum(e, axis=1, keepdims=True), 1e-30)


def _select_blocks(score, n_sel):
    nb = score.shape[1]
    lane = lax.broadcasted_iota(jnp.int32, score.shape, 1).astype(F32)
    sel = jnp.zeros(score.shape, F32)
    for _ in range(n_sel):
        m = jnp.max(score, axis=1, keepdims=True)
        first = jnp.min(jnp.where(score == m, lane, float(nb)), axis=1, keepdims=True)
        pick = lane == first
        sel = jnp.where(pick, 1.0, sel)
        score = jnp.where(pick, -jnp.inf, score)
    return sel


def _block_scores(p_slc, pos_q):
    jb = lax.broadcasted_iota(jnp.int32, (1, p_slc.shape[1]), 1)
    cur = pos_q // SEL_BLOCK
    valid = jb * SEL_BLOCK <= pos_q
    forced = (jb == 0) | (jb == cur) | (jb == cur - 1)
    return jnp.where(forced, FORCE_SCORE, jnp.where(valid, p_slc, -1.0))


def _select_blocks_t(score, n_sel):
    st = score.T
    nb = st.shape[0]
    blk = lax.broadcasted_iota(jnp.int32, st.shape, 0).astype(F32)
    sel = jnp.zeros(st.shape, F32)
    for _ in range(n_sel):
        m = jnp.max(st, axis=0, keepdims=True)
        first = jnp.min(jnp.where(st == m, blk, float(nb)), axis=0, keepdims=True)
        pick = blk == first
        sel = jnp.where(pick, 1.0, sel)
        st = jnp.where(pick, -jnp.inf, st)
    return sel.T


def _online_softmax_step(s, allowed, v, m_old, l_old, acc_old, feature_major=False):
    s = _mask_rows(s, allowed)
    m_new = jnp.maximum(m_old, jnp.max(s, axis=1, keepdims=True))
    m_use = jnp.where(m_new == -jnp.inf, 0.0, m_new)
    alpha = jnp.exp(m_old - m_use)
    p = jnp.exp(s - m_use)
    l_new = alpha * l_old + jnp.sum(p, axis=1, keepdims=True)
    if feature_major:
        pv = lax.dot_general(p.astype(BF16), v, NT, preferred_element_type=F32)
    else:
        pv = jnp.dot(p.astype(BF16), v, preferred_element_type=F32)
    return m_new, l_new, alpha * acc_old + pv


def _padded_query(tile, m, in_half):
    sel = jnp.where(in_half, tile, jnp.zeros_like(tile))
    z = jnp.zeros_like(sel)
    return jnp.concatenate([sel, z] if m == 0 else [z, sel], axis=1)


def _nsa_attn_kernel(q_ref, g_ref, kcmp_ref, vcmp_ref, ks_ref, vs_ref, kw_ref, vw_ref, slc_ref, ex_ref,
                     o_ref, qz_ref, sel_ref, oc_ref, m_ref, l_ref, acc_ref):
    i = pl.program_id(1)
    start = i * Q_BLOCK
    pos_q = start + lax.broadcasted_iota(jnp.int32, (Q_BLOCK, 1), 0)
    lane128 = lax.broadcasted_iota(jnp.int32, (1, LANES), 1)
    n_cmp = kcmp_ref.shape[0]
    cmp_end = CMP_STRIDE * lax.broadcasted_iota(jnp.int32, (1, n_cmp), 1) + (CMP_LEN - 1)
    cmp_ok = cmp_end <= pos_q
    gexp = jnp.dot(g_ref[...], ex_ref[...], precision=HIGHEST, preferred_element_type=F32)
    o_ref[...] = jnp.zeros_like(o_ref)

    win_base = pl.multiple_of(jnp.maximum(start - WINDOW, 0), Q_BLOCK)
    band = WINDOW + Q_BLOCK
    kw_t = kw_ref[pl.ds(win_base, band), :]
    vw_t = vw_ref[pl.ds(win_base, band), :]
    dpos = pos_q - (win_base + lax.broadcasted_iota(jnp.int32, (1, band), 1))
    win_ok = (dpos >= 0) & (dpos < WINDOW)

    for g in range(NSA_KV_HEADS):
        m, half = divmod(g, 2)
        in_half = (lane128 // NSA_HD) == half
        qz = jnp.concatenate(
            [_padded_query(q_ref[:, (m * NSA_GROUP + r) * LANES:(m * NSA_GROUP + r + 1) * LANES], m, in_half)
             for r in range(NSA_GROUP)], axis=0)
        qz_ref[g] = qz
        s = lax.dot_general(qz, kcmp_ref[...], NT, preferred_element_type=F32)
        p = _softmax_rows(s, cmp_ok)
        oc_ref[g] = jnp.dot(p.astype(BF16), vcmp_ref[...], preferred_element_type=F32)[:, m * LANES:(m + 1) * LANES]
        imp = p[0:Q_BLOCK]
        for r in range(1, NSA_GROUP):
            imp = imp + p[r * Q_BLOCK:(r + 1) * Q_BLOCK]
        p_slc = jnp.dot(imp, slc_ref[...], precision=HIGHEST, preferred_element_type=F32)
        sel_ref[g] = _select_blocks_t(_block_scores(p_slc, pos_q), N_SEL).astype(BF16)

    m_ref[...] = jnp.full_like(m_ref, -jnp.inf)
    l_ref[...] = jnp.zeros_like(l_ref)
    acc_ref[...] = jnp.zeros_like(acc_ref)

    def kv_step(t, carry):
        k0 = pl.multiple_of(t * NSA_KV_TILE, NSA_KV_TILE)
        k_t = ks_ref[pl.ds(k0, NSA_KV_TILE), :]
        v_t = vs_ref[pl.ds(k0, NSA_KV_TILE), :]
        key = k0 + lax.broadcasted_iota(jnp.int32, (1, NSA_KV_TILE), 1)
        blk = lax.broadcasted_iota(jnp.int32, (LANES, 1), 0)
        expand = jnp.where(blk == key // SEL_BLOCK, 1.0, 0.0).astype(BF16)
        causal = key <= pos_q
        for g in range(NSA_KV_HEADS):
            chosen = jnp.dot(sel_ref[g], expand, preferred_element_type=F32)
            s = lax.dot_general(qz_ref[g], k_t, NT, preferred_element_type=F32)
            m_ref[g], l_ref[g], acc_ref[g] = _online_softmax_step(s, (chosen > 0.5) & causal, v_t,
                                                                  m_ref[g], l_ref[g], acc_ref[g])
        return carry

    lax.fori_loop(0, (start + Q_BLOCK + NSA_KV_TILE - 1) // NSA_KV_TILE, kv_step, 0)

    for g in range(NSA_KV_HEADS):
        m, half = divmod(g, 2)
        in_half = (lane128 // NSA_HD) == half
        mcols = slice(m * LANES, (m + 1) * LANES)
        o_s = acc_ref[g][:, mcols] / jnp.maximum(l_ref[g], 1e-30)
        s = lax.dot_general(qz_ref[g], kw_t, NT, preferred_element_type=F32)
        o_w = jnp.dot(_softmax_rows(s, win_ok).astype(BF16), vw_t, preferred_element_type=F32)[:, mcols]
        o_c = oc_ref[g]
        for r in range(NSA_GROUP):
            t = m * NSA_GROUP + r
            cols = slice(t * LANES, (t + 1) * LANES)
            rows = slice(r * Q_BLOCK, (r + 1) * Q_BLOCK)
            comb = (gexp[:, cols] * o_c[rows] + gexp[:, NSA_QD + t * LANES:NSA_QD + (t + 1) * LANES] * o_s[rows]
                    + gexp[:, 2 * NSA_QD + t * LANES:2 * NSA_QD + (t + 1) * LANES] * o_w[rows])
            o_ref[:, cols] += jnp.where(in_half, comb, 0.0)


def nsa_attn_prompt(q, gates, kcmp, vcmp, ks, vs, kw, vw, b, s):
    nq = s // Q_BLOCK
    rows = NSA_GROUP * Q_BLOCK
    slc = jnp.asarray(_nsa_slc_matrix(s // CMP_STRIDE, LANES))
    ex = jnp.asarray(_nsa_gate_expand())
    seq = pl.BlockSpec((s, NSA_KVD), lambda bi, i: (bi, 0))
    cmp_spec = pl.BlockSpec((None, s // CMP_STRIDE, NSA_KVD), lambda bi, i: (bi, 0, 0))
    return pl.pallas_call(
        _nsa_attn_kernel,
        out_shape=jax.ShapeDtypeStruct((b * s, NSA_QD), F32),
        grid=(b, nq),
        in_specs=[
            pl.BlockSpec((Q_BLOCK, NSA_QD), lambda bi, i: (bi * nq + i, 0)),
            pl.BlockSpec((Q_BLOCK, LANES), lambda bi, i: (bi * nq + i, 0)),
            cmp_spec, cmp_spec, seq, seq, seq, seq,
            pl.BlockSpec(slc.shape, lambda bi, i: (0, 0)),
            pl.BlockSpec(ex.shape, lambda bi, i: (0, 0)),
        ],
        out_specs=pl.BlockSpec((Q_BLOCK, NSA_QD), lambda bi, i: (bi * nq + i, 0)),
        scratch_shapes=[pltpu.VMEM((NSA_KV_HEADS, rows, NSA_KVD), BF16),
                        pltpu.VMEM((NSA_KV_HEADS, Q_BLOCK, LANES), BF16),
                        pltpu.VMEM((NSA_KV_HEADS, rows, LANES), F32),
                        pltpu.VMEM((NSA_KV_HEADS, rows, 1), F32),
                        pltpu.VMEM((NSA_KV_HEADS, rows, 1), F32),
                        pltpu.VMEM((NSA_KV_HEADS, rows, NSA_KVD), F32)],
        compiler_params=_params("parallel", "arbitrary"),
        name="nsa_attn_prompt",
    )(q, gates, kcmp, vcmp, ks, vs, kw, vw, slc, ex)


PAGES_PER_STEP = 8
SUM_PAGES_PER_STEP = 16


def _feature_major(cache):
    n, tokens = cache.shape[:2]
    return cache.transpose(0, 2, 3, 1).reshape(n, NSA_KVD, tokens)


def _page_specs(n, page):
    def one(u):
        return pl.BlockSpec((None, NSA_KVD, page), lambda bi, j, pt: (pt[bi, j * n + u], 0, 0))
    return [one(u) for u in range(n)]


def _nsa_page_sums_kernel(pt_ref, *refs):
    n = SUM_PAGES_PER_STEP
    kp, vp = refs[:n], refs[n:2 * n]
    wk1_ref, wk2_ref, wv1_ref, wv2_ref, grp_ref, fk_ref, sk_ref, fv_ref, sv_ref = refs[2 * n:]

    def sums(pages, w_ref, o_ref):
        xw = jnp.concatenate([(p[...] * w_ref[...]).astype(BF16) for p in pages], axis=1)
        o_ref[...] = jnp.dot(xw, grp_ref[...], preferred_element_type=F32)

    sums(kp, wk1_ref, fk_ref)
    sums(kp, wk2_ref, sk_ref)
    sums(vp, wv1_ref, fv_ref)
    sums(vp, wv2_ref, sv_ref)


def nsa_page_sums(page_table, pool_k, pool_v, pe_k, pe_v):
    db, n_pages = page_table.shape
    page = pool_k.shape[2]
    n = SUM_PAGES_PER_STEP
    per_page = page // CMP_STRIDE
    assert n * per_page == LANES
    out = jax.ShapeDtypeStruct((db, NSA_KVD, n_pages * per_page), F32)
    out_spec = pl.BlockSpec((None, NSA_KVD, LANES), lambda bi, j, pt: (bi, 0, j))
    w_spec = pl.BlockSpec((NSA_KVD, page), lambda bi, j, pt: (0, 0))
    halves = lambda pe: [jnp.tile((1.0 + pe.reshape(CMP_LEN, NSA_KVD)[o:o + CMP_STRIDE]).T, (1, per_page))
                         for o in (0, CMP_STRIDE)]
    grp = np.zeros((n * page, LANES), np.float32)
    tok = np.arange(n * page)
    grp[tok, tok // CMP_STRIDE] = 1.0
    grp = jnp.asarray(grp, BF16)
    return pl.pallas_call(
        _nsa_page_sums_kernel,
        out_shape=(out,) * 4,
        grid_spec=pltpu.PrefetchScalarGridSpec(
            num_scalar_prefetch=1,
            grid=(db, n_pages // n),
            in_specs=_page_specs(n, page) * 2 + [w_spec] * 4 + [pl.BlockSpec(grp.shape, lambda bi, j, pt: (0, 0))],
            out_specs=(out_spec,) * 4,
        ),
        compiler_params=_params("parallel", "arbitrary"),
        name="nsa_page_sums",
    )(page_table, *([pool_k] * n), *([pool_v] * n), *halves(pe_k), *halves(pe_v), grp)


def _decode_queries(q_ref, g):
    m, half = divmod(g, 2)
    in_half = (lax.broadcasted_iota(jnp.int32, (1, LANES), 1) // NSA_HD) == half
    tiles = [_padded_query(q_ref[:, (m * NSA_GROUP + r) * LANES:(m * NSA_GROUP + r + 1) * LANES], m, in_half)
             for r in range(NSA_GROUP)]
    return jnp.concatenate(tiles, axis=0).astype(BF16)


def _group_rows(x):
    return jnp.concatenate([x] * NSA_GROUP, axis=0)


def _nsa_decode_select_kernel(past, q_ref, fk_ref, sk_ref, fv_ref, sv_ref, kn_ref, vn_ref, wk_ref, wv_ref,
                              slc_ref, sel_ref, oc_ref):
    nq = q_ref.shape[0]
    n_cmp = fk_ref.shape[1]
    col = lax.broadcasted_iota(jnp.int32, (1, n_cmp), 1)
    eye = (lax.broadcasted_iota(jnp.int32, (NSA_KVD, 1), 0)
           == lax.broadcasted_iota(jnp.int32, (1, NSA_KVD), 1))

    def summaries(f_ref, s_ref, new_ref, w_ref):
        w = 1.0 + w_ref[CMP_STRIDE:CMP_STRIDE + nq, :]
        second_new = jnp.sum(new_ref[...] * w, axis=0, keepdims=True)
        new_col = jnp.sum(jnp.where(eye, second_new, 0.0), axis=1, keepdims=True)
        nxt = jnp.where(col == n_cmp - 1, new_col, pltpu.roll(s_ref[...], n_cmp - 1, 1))
        return ((f_ref[...] + nxt) / CMP_LEN).astype(BF16)

    kcmp_t = summaries(fk_ref, sk_ref, kn_ref, wk_ref)
    vcmp_t = summaries(fv_ref, sv_ref, vn_ref, wv_ref)
    pos_q = past + lax.broadcasted_iota(jnp.int32, (nq, 1), 0)
    cmp_end = CMP_STRIDE * lax.broadcasted_iota(jnp.int32, (1, n_cmp), 1) + (CMP_LEN - 1)
    cmp_ok = _group_rows(cmp_end <= pos_q)
    lane128 = lax.broadcasted_iota(jnp.int32, (1, LANES), 1)
    oc_ref[...] = jnp.zeros_like(oc_ref)
    for g in range(NSA_KV_HEADS):
        m, half = divmod(g, 2)
        in_half = (lane128 // NSA_HD) == half
        s = jnp.dot(_decode_queries(q_ref, g), kcmp_t, preferred_element_type=F32)
        p = _softmax_rows(s, cmp_ok)
        o_c = lax.dot_general(p.astype(BF16), vcmp_t, NT,
                              preferred_element_type=F32)[:, m * LANES:(m + 1) * LANES]
        imp = p[0:nq]
        for r in range(1, NSA_GROUP):
            imp = imp + p[r * nq:(r + 1) * nq]
        for r in range(NSA_GROUP):
            cols = slice((m * NSA_GROUP + r) * LANES, (m * NSA_GROUP + r + 1) * LANES)
            oc_ref[:, cols] += jnp.where(in_half, o_c[r * nq:(r + 1) * nq], 0.0)
        p_slc = jnp.dot(imp, slc_ref[...], precision=HIGHEST, preferred_element_type=F32)
        sel = _group_rows(_select_blocks(_block_scores(p_slc, pos_q), N_SEL)).astype(BF16)
        rows = slice(g * NSA_GROUP * nq, (g + 1) * NSA_GROUP * nq)
        for lt in range(sel_ref.shape[0]):
            sel_ref[lt, rows, :] = sel[:, lt * LANES:(lt + 1) * LANES]


def nsa_decode_select(q, sums, kc_new, vc_new, pe_k, pe_v, db, nq, past):
    n_cmp = sums[0].shape[2]
    nsb_pad = _round_up(-(-(past + nq) // SEL_BLOCK), LANES)
    sel_rows = NSA_KV_HEADS * NSA_GROUP * nq
    slc = jnp.asarray(_nsa_slc_matrix(n_cmp, nsb_pad))
    cmp_spec = pl.BlockSpec((None, NSA_KVD, n_cmp), lambda bi: (bi, 0, 0))
    new_spec = pl.BlockSpec((nq, NSA_KVD), lambda bi: (bi, 0))
    pe_spec = pl.BlockSpec((CMP_LEN, NSA_KVD), lambda bi: (0, 0))
    return pl.pallas_call(
        functools.partial(_nsa_decode_select_kernel, past),
        out_shape=(jax.ShapeDtypeStruct((db, nsb_pad // LANES, sel_rows, LANES), BF16),
                   jax.ShapeDtypeStruct((db * nq, NSA_QD), F32)),
        grid=(db,),
        in_specs=[pl.BlockSpec((nq, NSA_QD), lambda bi: (bi, 0)), cmp_spec, cmp_spec, cmp_spec, cmp_spec,
                  new_spec, new_spec, pe_spec, pe_spec, pl.BlockSpec(slc.shape, lambda bi: (0, 0))],
        out_specs=(pl.BlockSpec((None, nsb_pad // LANES, sel_rows, LANES), lambda bi: (bi, 0, 0, 0)),
                   pl.BlockSpec((nq, NSA_QD), lambda bi: (bi, 0))),
        compiler_params=_params("parallel"),
        name="nsa_decode_select",
    )(q, *sums, kc_new, vc_new, pe_k.reshape(CMP_LEN, NSA_KVD), pe_v.reshape(CMP_LEN, NSA_KVD), slc)


def _nsa_decode_attend_kernel(past, pt_ref, q_ref, sel_ref, g_ref, oc_ref, ksn_ref, vsn_ref, wk_ref, wv_ref,
                              kwn_ref, vwn_ref, ex_ref, xp_ref, *refs):
    n = PAGES_PER_STEP
    kp, vp = refs[:n], refs[n:2 * n]
    o_ref, qz_scr, m_scr, l_scr, acc_scr = refs[2 * n:]
    j = pl.program_id(1)
    nq = q_ref.shape[0]
    n_keys = n * kp[0].shape[1]
    steps_per_tile = xp_ref.shape[0]
    pos_q = jnp.concatenate([past + lax.broadcasted_iota(jnp.int32, (nq, 1), 0)] * (NSA_KV_HEADS * NSA_GROUP),
                            axis=0)

    @pl.when(j == 0)
    def _():
        m_scr[...] = jnp.full_like(m_scr, -jnp.inf)
        l_scr[...] = jnp.zeros_like(l_scr)
        acc_scr[...] = jnp.zeros_like(acc_scr)
        qz_scr[...] = jnp.concatenate([_decode_queries(q_ref, g) for g in range(NSA_KV_HEADS)], axis=0)

    def attend(s, allowed, v, feature_major):
        m_scr[...], l_scr[...], acc_scr[...] = _online_softmax_step(
            s, allowed, v, m_scr[...], l_scr[...], acc_scr[...], feature_major)

    k_t = jnp.concatenate([r[...].astype(BF16) for r in kp], axis=1)
    v_t = jnp.concatenate([r[...].astype(BF16) for r in vp], axis=1)
    key = j * n_keys + lax.broadcasted_iota(jnp.int32, (1, n_keys), 1)
    chosen = jnp.dot(sel_ref[j // steps_per_tile], xp_ref[j % steps_per_tile], preferred_element_type=F32)
    attend(jnp.dot(qz_scr[...], k_t, preferred_element_type=F32), (chosen > 0.5) & (key <= pos_q), v_t, True)

    @pl.when(j == pl.num_programs(1) - 1)
    def _():
        qz = qz_scr[...]
        lane128 = lax.broadcasted_iota(jnp.int32, (1, LANES), 1)
        n_new = ksn_ref.shape[0]
        new_idx = lax.broadcasted_iota(jnp.int32, (1, n_new), 1)
        new_key = past + new_idx
        new_blk = past // SEL_BLOCK
        chosen_new = sel_ref[new_blk // LANES][:, new_blk % LANES:new_blk % LANES + 1].astype(F32) > 0.5
        attend(lax.dot_general(qz, ksn_ref[...], NT, preferred_element_type=F32),
               chosen_new & (new_key <= pos_q) & (new_idx < nq), vsn_ref[...], False)
        o_s = acc_scr[...] / jnp.maximum(l_scr[...], 1e-30)
        n_win = wk_ref.shape[1]
        k_wc, v_wc = wk_ref[...].astype(BF16), wv_ref[...].astype(BF16)
        pos_wc = past - n_win + lax.broadcasted_iota(jnp.int32, (1, n_win), 1)
        ok_wc = (pos_q - pos_wc >= 0) & (pos_q - pos_wc < WINDOW) & (pos_wc >= 0)
        ok_wn = (pos_q - new_key >= 0) & (pos_q - new_key < WINDOW) & (new_idx < nq)
        win = (jnp.full(m_scr.shape, -jnp.inf, F32), jnp.zeros(l_scr.shape, F32), jnp.zeros(acc_scr.shape, F32))
        win = _online_softmax_step(jnp.dot(qz, k_wc, preferred_element_type=F32), ok_wc, v_wc, *win, True)
        win = _online_softmax_step(lax.dot_general(qz, kwn_ref[...], NT, preferred_element_type=F32), ok_wn,
                                   vwn_ref[...], *win, False)
        o_w = win[2] / jnp.maximum(win[1], 1e-30)
        gexp = jnp.dot(g_ref[...], ex_ref[...], precision=HIGHEST, preferred_element_type=F32)
        o_ref[...] = gexp[:, :NSA_QD] * oc_ref[...]
        for g in range(NSA_KV_HEADS):
            m, half = divmod(g, 2)
            in_half = (lane128 // NSA_HD) == half
            mcols = slice(m * LANES, (m + 1) * LANES)
            for r in range(NSA_GROUP):
                t = m * NSA_GROUP + r
                cols = slice(t * LANES, (t + 1) * LANES)
                rows = slice((g * NSA_GROUP + r) * nq, (g * NSA_GROUP + r + 1) * nq)
                comb = (gexp[:, NSA_QD + t * LANES:NSA_QD + (t + 1) * LANES] * o_s[rows, mcols]
                        + gexp[:, 2 * NSA_QD + t * LANES:2 * NSA_QD + (t + 1) * LANES] * o_w[rows, mcols])
                o_ref[:, cols] += jnp.where(in_half, comb, 0.0)


def nsa_decode_attend(page_table, q, sel, gates, o_c, ks_new, vs_new, wk, wv, kw_new, vw_new, pool_k, pool_v,
                      db, nq, past):
    n_pages = page_table.shape[1]
    page = pool_k.shape[2]
    ex = jnp.asarray(_nsa_gate_expand())
    n_win = wk.shape[2]
    rows = NSA_KV_HEADS * NSA_GROUP * nq
    n_keys = PAGES_PER_STEP * page
    assert past % SEL_BLOCK + nq <= SEL_BLOCK and (LANES * SEL_BLOCK) % n_keys == 0
    steps_per_tile = LANES * SEL_BLOCK // n_keys
    xp = np.zeros((steps_per_tile, LANES, n_keys), np.float32)
    keys = np.arange(n_keys)
    for u in range(steps_per_tile):
        xp[u, u * (n_keys // SEL_BLOCK) + keys // SEL_BLOCK, keys] = 1.0
    xp = jnp.asarray(xp, BF16)
    per_b = lambda shape: pl.BlockSpec((None,) + shape, lambda bi, j, pt: (bi,) + (0,) * len(shape))
    q_rows = lambda width: pl.BlockSpec((nq, width), lambda bi, j, pt: (bi, 0))
    new_rows = ks_new.shape[1]
    return pl.pallas_call(
        functools.partial(_nsa_decode_attend_kernel, past),
        out_shape=jax.ShapeDtypeStruct((db * nq, NSA_QD), F32),
        grid_spec=pltpu.PrefetchScalarGridSpec(
            num_scalar_prefetch=1,
            grid=(db, n_pages // PAGES_PER_STEP),
            in_specs=[q_rows(NSA_QD), per_b(sel.shape[1:]), q_rows(LANES), q_rows(NSA_QD),
                      per_b((new_rows, NSA_KVD)), per_b((new_rows, NSA_KVD)),
                      per_b((NSA_KVD, n_win)), per_b((NSA_KVD, n_win)),
                      per_b((new_rows, NSA_KVD)), per_b((new_rows, NSA_KVD)),
                      pl.BlockSpec(ex.shape, lambda bi, j, pt: (0, 0)),
                      pl.BlockSpec(xp.shape, lambda bi, j, pt: (0, 0, 0))]
            + _page_specs(PAGES_PER_STEP, page) * 2,
            out_specs=q_rows(NSA_QD),
            scratch_shapes=[pltpu.VMEM((rows, NSA_KVD), BF16), pltpu.VMEM((rows, 1), F32),
                            pltpu.VMEM((rows, 1), F32), pltpu.VMEM((rows, NSA_KVD), F32)],
        ),
        compiler_params=_params("parallel", "arbitrary"),
        name="nsa_decode_attend",
    )(page_table, q, sel, gates, o_c, ks_new, vs_new, wk, wv, kw_new, vw_new, ex, xp,
      *([pool_k] * PAGES_PER_STEP), *([pool_v] * PAGES_PER_STEP))


def _nsa_weights(w_in, w_out):
    perm = _nsa_q_perm()
    return jnp.concatenate([w_in[:, perm], w_in[:, NSA_QD:]], axis=1), w_out[perm]


def nsa_prompt(x, scale, shift, w_in_p, pe_k, pe_v, b, s):
    p, _ = project(x, scale, shift, w_in_p, b, s)
    cos, up, dn = _rope_tables(jnp.arange(s, dtype=jnp.int32))
    q, kc, vc, ks, vs, kw, vw, ksb, vsb, kwb, vwb, gates = nsa_prep(p, cos, up, dn, q_dtype=BF16)
    kcmp = nsa_compress(kc, pe_k, b, s)
    vcmp = nsa_compress(vc, pe_v, b, s)
    o = nsa_attn_prompt(q, gates, kcmp, vcmp, ksb, vsb, kwb, vwb, b, s)
    nw = min(WINDOW, s)
    rs = lambda a: a.reshape(b, s, NSA_KV_HEADS, NSA_HD)
    return o, (rs(kc), rs(vc), rs(ks), rs(vs), rs(kw)[:, s - nw:], rs(vw)[:, s - nw:])


def nsa_sample(x, scale, shift, ck, cv, sk, sv, wk, wv, page_table, w_in_p, pe_k, pe_v, db, nq):
    past = page_table.shape[1] * ck.shape[1]
    p, _ = project(x, scale, shift, w_in_p, db, nq)
    pos = past + jnp.tile(jnp.arange(nq, dtype=jnp.int32), db)
    cos, up, dn = _rope_tables(pos)
    q, kc, vc, ks, vs, kw, vw, ksb, vsb, kwb, vwb, gates = nsa_prep(p, cos, up, dn, q_dtype=F32)
    sums = nsa_page_sums(page_table, _feature_major(ck), _feature_major(cv), pe_k, pe_v)
    sel, o_c = nsa_decode_select(q, sums, kc, vc, pe_k, pe_v, db, nq, past)
    new_pad = lambda a: jnp.pad(a.reshape(db, nq, NSA_KVD), ((0, 0), (0, LANES - nq), (0, 0)))
    o = nsa_decode_attend(page_table, q, sel, gates, o_c, new_pad(ksb), new_pad(vsb),
                          _feature_major(wk), _feature_major(wv), new_pad(kwb), new_pad(vwb),
                          _feature_major(sk), _feature_major(sv), db, nq, past)
    rs = lambda a: a.reshape(db, nq, NSA_KV_HEADS, NSA_HD)
    slide = lambda cache, new: jnp.concatenate([cache, rs(new).astype(cache.dtype)], 1)[:, nq:]
    return o, (rs(kc), rs(vc), rs(ks), rs(vs), slide(wk, kw), slide(wv, vw))


def _top_values(s, k):
    vals = []
    for _ in range(k):
        m = jnp.max(s, axis=0, keepdims=True)
        vals.append(m)
        s = jnp.where(s >= m, -jnp.inf, s)
    return vals


def _peer_route_kernel(qv_ref, keys_ref, s1_ref, s2_ref, e1_ref, e2_ref, th_ref):
    for h in range(PEER_HEADS):
        q1 = qv_ref[:, (2 * h) * PEER_DKEY:(2 * h + 1) * PEER_DKEY]
        q2 = qv_ref[:, (2 * h + 1) * PEER_DKEY:(2 * h + 2) * PEER_DKEY]
        s1 = lax.dot_general(keys_ref[h, 0], q1, NT, precision=HIGHEST, preferred_element_type=F32)
        s2 = lax.dot_general(keys_ref[h, 1], q2, NT, precision=HIGHEST, preferred_element_type=F32)
        top1 = _top_values(s1, PEER_TOPK)
        top2 = _top_values(s2, PEER_TOPK)
        t1 = jnp.concatenate(top1, axis=0)
        t2 = jnp.concatenate(top2, axis=0)
        row8 = lax.broadcasted_iota(jnp.int32, (SUBLANES, 1), 0)
        pairs = [top1[0] + t2, top1[1] + t2[:SUBLANES], t1[SUBLANES:] + top2[0]]
        for i in range(2, SUBLANES):
            pairs.append(jnp.where(row8 < PEER_TOPK // (i + 1), top1[i] + t2[:SUBLANES], -jnp.inf))
        best = _top_values(jnp.concatenate(pairs, axis=0), PEER_TOPK)
        z = best[0] * 0.0
        for v in best:
            z = z + jnp.exp(v - best[0])
        e1 = jnp.exp(s1 - top1[0])
        e2 = jnp.exp(s2 - top2[0]) / z
        for c in range(qv_ref.shape[0] // LANES):
            cols = slice(c * LANES, (c + 1) * LANES)
            s1_ref[h, c] = s1[:, cols]
            s2_ref[h, c] = s2[:, cols]
            e1_ref[h, c] = e1[:, cols]
            e2_ref[h, c] = e2[:, cols]
            th_ref[h, c] = best[PEER_TOPK - 1][:, cols]


def peer_route(qv, keys, *, tm):
    t = qv.shape[0]
    nc = tm // LANES
    big = jax.ShapeDtypeStruct((PEER_HEADS, t // LANES, PEER_NKEYS, LANES), F32)
    big_spec = pl.BlockSpec((PEER_HEADS, nc, PEER_NKEYS, LANES), lambda i: (0, i, 0, 0))
    return pl.pallas_call(
        _peer_route_kernel,
        out_shape=(big, big, big, big, jax.ShapeDtypeStruct((PEER_HEADS, t // LANES, 1, LANES), F32)),
        grid=(t // tm,),
        in_specs=[
            pl.BlockSpec((tm, 2 * PEER_HEADS * PEER_DKEY), lambda i: (i, 0)),
            pl.BlockSpec(keys.shape, lambda i: (0, 0, 0, 0)),
        ],
        out_specs=(big_spec, big_spec, big_spec, big_spec,
                   pl.BlockSpec((PEER_HEADS, nc, 1, LANES), lambda i: (0, i, 0, 0))),
        compiler_params=_params("parallel"),
        name="peer_route",
    )(qv, keys)


PEER_A_PER_STEP = SUBLANES
PEER_ROUTE_TILE = 256


def _gelu_tanh(x):
    return 0.5 * x * (1.0 + jnp.tanh(math.sqrt(2.0 / math.pi) * (x + 0.044715 * (x * x * x))))


PEER_K_CHUNK = 256


def _peer_dense_kernel(h_ref, s1_ref, e1_ref, s2_ref, e2_ref, th_ref, u_ref, vt_ref, o_ref,
                       act0_ref, act1_ref, g0_ref, g1_ref, acc_ref):
    j = pl.program_id(1)
    n_groups = pl.num_programs(1) - 2
    tm = h_ref.shape[1]
    nc = tm // LANES
    ne, d = u_ref.shape[1], vt_ref.shape[1]
    w_valid = (j >= 1) & (j <= n_groups)

    @pl.when(j == 0)
    def _():
        acc_ref[...] = jnp.zeros_like(acc_ref)
        g0_ref[...] = jnp.zeros_like(g0_ref)
        g1_ref[...] = jnp.zeros_like(g1_ref)

    def run(act_w, act_r, g_w, g_r):
        def step(k, carry):
            kk, half = k // 2, k % 2
            rows_e = pl.ds(pl.multiple_of(half * (ne // 2), ne // 2), ne // 2)
            part = lax.dot_general(u_ref[kk, rows_e, :], h_ref[kk], NT, preferred_element_type=F32)
            for c in range(nc):
                prev = jnp.where(kk > 0, act_w[c, rows_e, :], 0.0)
                act_w[c, rows_e, :] = part[:, c * LANES:(c + 1) * LANES] + prev
            rows_a = pl.ds(pl.multiple_of(k * PEER_NKEYS, PEER_NKEYS), PEER_NKEYS)
            for c in range(nc):
                w = jnp.zeros((PEER_NKEYS, LANES), F32)
                for h in range(PEER_HEADS):
                    cand = s2_ref[h, c] + s1_ref[h, c, pl.ds(k, 1), :]
                    w = w + e1_ref[h, c, pl.ds(k, 1), :] * jnp.where(cand >= th_ref[h, c], e2_ref[h, c], 0.0)
                val = w * _gelu_tanh(act_r[c, rows_a, :])
                g_w[c, rows_a, :] = jnp.where(w_valid, val, 0.0).astype(BF16)
            rows_g = pl.ds(pl.multiple_of(kk * PEER_K_CHUNK, PEER_K_CHUNK), PEER_K_CHUNK)
            g = jnp.concatenate([g_r[c, rows_g, :] for c in range(nc)], axis=1)
            rows_d = pl.ds(pl.multiple_of(half * (d // 2), d // 2), d // 2)
            acc_ref[rows_d, :] += jnp.dot(vt_ref[kk, rows_d, :], g, preferred_element_type=F32)
            return carry

        lax.fori_loop(0, PEER_A_PER_STEP, step, 0)

    @pl.when(j % 2 == 0)
    def _():
        run(act0_ref, act1_ref, g1_ref, g0_ref)

    @pl.when(j % 2 == 1)
    def _():
        run(act1_ref, act0_ref, g0_ref, g1_ref)

    @pl.when(j == pl.num_programs(1) - 1)
    def _():
        o_ref[...] = acc_ref[...].T


def peer_dense(h4, s1, s2, e1, e2, th, u4, vt4, *, tm):
    n_k, t, _ = h4.shape
    d = vt4.shape[1]
    ne = PEER_A_PER_STEP * PEER_NKEYS
    assert ne == n_k * PEER_K_CHUNK and 2 * n_k == PEER_A_PER_STEP
    n_groups = u4.shape[1] // ne
    nc = tm // LANES
    clamp = lambda g: jnp.minimum(jnp.maximum(g, 0), n_groups - 1)
    tok = pl.BlockSpec((PEER_HEADS, nc, PEER_NKEYS, LANES), lambda i, j: (0, i, 0, 0))
    arow = pl.BlockSpec((PEER_HEADS, nc, PEER_A_PER_STEP, LANES), lambda i, j: (0, i, clamp(j - 1), 0))
    return pl.pallas_call(
        _peer_dense_kernel,
        out_shape=jax.ShapeDtypeStruct((t, d), F32),
        grid=(t // tm, n_groups + 2),
        in_specs=[
            pl.BlockSpec((n_k, tm, PEER_K_CHUNK), lambda i, j: (0, i, 0)),
            arow, arow, tok, tok,
            pl.BlockSpec((PEER_HEADS, nc, 1, LANES), lambda i, j: (0, i, 0, 0)),
            pl.BlockSpec((n_k, ne, PEER_K_CHUNK), lambda i, j: (0, clamp(j), 0)),
            pl.BlockSpec((n_k, d, PEER_K_CHUNK), lambda i, j: (clamp(j - 2), 0, 0)),
        ],
        out_specs=pl.BlockSpec((tm, d), lambda i, j: (i, 0)),
        scratch_shapes=[pltpu.VMEM((nc, ne, LANES), F32), pltpu.VMEM((nc, ne, LANES), F32),
                        pltpu.VMEM((nc, ne, LANES), BF16), pltpu.VMEM((nc, ne, LANES), BF16),
                        pltpu.VMEM((d, tm), F32)],
        compiler_params=_params("parallel", "arbitrary"),
        name="peer_dense",
    )(h4, s1, e1, s2, e2, th, u4, vt4)


def _peer_tables(u, v):
    e, d = u.shape
    u4 = u.astype(BF16).reshape(e, d // PEER_K_CHUNK, PEER_K_CHUNK).transpose(1, 0, 2)
    vt4 = v.astype(BF16).reshape(e // PEER_K_CHUNK, PEER_K_CHUNK, d).transpose(0, 2, 1)
    return u4, vt4


def peer_ffn(x, scale, shift, wq, keys, u4, vt4, b, s):
    qv, h = project(x, scale, shift, wq, b, s)
    t, d = h.shape
    tm = _row_tile(t)
    s1, s2, e1, e2, th = peer_route(qv, keys, tm=min(tm, PEER_ROUTE_TILE))
    h4 = h.reshape(t, d // PEER_K_CHUNK, PEER_K_CHUNK).transpose(1, 0, 2)
    return peer_dense(h4, s1, s2, e1, e2, th, u4, vt4, tm=tm)


def kernel(x_prompt, x_sample, state_gla, cache_cmp_k, cache_cmp_v, cache_sel_k, cache_sel_v, cache_win_k, cache_win_v, page_table, c_prompt, c_sample, ada_w, ada_b, ln_g, ln_b, gla_w_in, gla_w_a2, gla_b_a2, gla_gn, gla_w_out, nsa_w_in, nsa_pe_k, nsa_pe_v, nsa_w_out, peer_wq, peer_keys, peer_u, peer_v):
    bp, sp, d = x_prompt.shape
    bs, ss, _ = x_sample.shape
    groups = ((bp, sp), (bs, ss))
    c_all = jnp.concatenate([c_prompt, c_sample], axis=0)
    c_rows = _round_up(bp + bs, SUBLANES)
    mod = ada_mod(jnp.pad(c_all, ((0, c_rows - bp - bs), (0, 0))), ada_w, ada_b)
    ys = [x_prompt.reshape(bp * sp, d), x_sample.reshape(bs * ss, d)]
    gla_states, nsa_rows = ([], []), ([], [])
    for i in range(DEPTH):
        row0 = (0, bp)
        mods = [[mod[i, row0[n]:row0[n] + b, k * d:(k + 1) * d][:, None, :] for k in range(6)]
                for n, (b, _) in enumerate(groups)]
        j = i // N_MIXERS
        if i % N_MIXERS == 0:
            for n, (b, s) in enumerate(groups):
                shift, scale, gate = mods[n][:3]
                proj, _ = project(ys[n], scale, shift, gla_w_in[j], b, s)
                o, st = gla_recurrence(proj, gla_w_a2[j], gla_b_a2[j], None if n == 0 else state_gla[j], b, s)
                gla_states[n].append(st.astype(state_gla.dtype))
                ys[n] = post(ys[n], gate, ln_g[i, 0], ln_b[i, 0], b, s, sub=o, w_out=gla_w_out[j],
                             gla_proj=proj, gla_gn=gla_gn[j])
        else:
            w_in_p, w_out_p = _nsa_weights(nsa_w_in[j], nsa_w_out[j])
            for n, (b, s) in enumerate(groups):
                shift, scale, gate = mods[n][:3]
                if n == 0:
                    o, rows = nsa_prompt(ys[n], scale, shift, w_in_p, nsa_pe_k[j], nsa_pe_v[j], b, s)
                else:
                    o, rows = nsa_sample(ys[n], scale, shift, cache_cmp_k[j], cache_cmp_v[j], cache_sel_k[j],
                                         cache_sel_v[j], cache_win_k[j], cache_win_v[j], page_table,
                                         w_in_p, nsa_pe_k[j], nsa_pe_v[j], b, s)
                nsa_rows[n].append(rows)
                ys[n] = post(ys[n], gate, ln_g[i, 0], ln_b[i, 0], b, s, sub=o, w_out=w_out_p)
        u4, vt4 = _peer_tables(peer_u[i], peer_v[i])
        for n, (b, s) in enumerate(groups):
            shift, scale, gate = mods[n][3:]
            f = peer_ffn(ys[n], scale, shift, peer_wq[i], peer_keys[i], u4, vt4, b, s)
            ys[n] = post(ys[n], gate, ln_g[i, 1], ln_b[i, 1], b, s, sub=f)

    st = lambda ts, k: jnp.stack([t[k] for t in ts])
    return (ys[0].reshape(bp, sp, d), ys[1].reshape(bs, ss, d),
            jnp.stack(gla_states[0]), jnp.stack(gla_states[1]),
            *(st(nsa_rows[0], k) for k in range(6)), *(st(nsa_rows[1], k) for k in range(6)))
```

```python
import functools
import math

import jax
import jax.numpy as jnp
import numpy as np
from jax import lax
from jax.experimental import pallas as pl
from jax.experimental.pallas import tpu as pltpu

D_MODEL = 1024
DEPTH = 2
N_MIXERS = 2
DN_ALPHA = (2.0 * DEPTH) ** 0.25
LN_EPS = 1e-5
F32 = jnp.float32
BF16 = jnp.bfloat16
HIGHEST = lax.Precision.HIGHEST

GLA_HEADS = 4
GLA_DK = D_MODEL // 2 // GLA_HEADS
GLA_DV = D_MODEL // GLA_HEADS
GLA_TAU = 16.0
GLA_CHUNK = 64
GLA_HK = GLA_HEADS * GLA_DK
GLA_HV = GLA_HEADS * GLA_DV

NSA_HEADS = 16
NSA_KV_HEADS = 4
NSA_GROUP = NSA_HEADS // NSA_KV_HEADS
NSA_HD = D_MODEL // NSA_HEADS
NSA_QD = NSA_HEADS * NSA_HD
NSA_KVD = NSA_KV_HEADS * NSA_HD
CMP_LEN = 32
CMP_STRIDE = 16
SEL_BLOCK = 64
N_SEL = 16
WINDOW = 512
Q_BLOCK = 128
FORCE_SCORE = 1e6
ROT_DIM = NSA_HD // 4
ROPE_THETA = 500000.0

PEER_HEADS = 8
PEER_NKEYS = 128
PEER_DKEY = 128
PEER_TOPK = 16

LANES = 128
SUBLANES = 8
VMEM_LIMIT_BYTES = 56 * 1024 * 1024
ROW_TILE = 512

NT = (((1,), (1,)), ((), ()))
TN = (((0,), (0,)), ((), ()))


def _round_up(n, m):
    return -(-n // m) * m


def _params(*sem):
    return pltpu.CompilerParams(dimension_semantics=sem, vmem_limit_bytes=VMEM_LIMIT_BYTES)


def _row_tile(t):
    return ROW_TILE if t % ROW_TILE == 0 else t


def _mod_operand(m, b, s, tm):
    d = m.shape[-1]
    if s % tm == 0:
        return m, pl.BlockSpec((None, 1, d), lambda i, *_: (i * tm // s, 0, 0))
    rows = jnp.broadcast_to(m, (b, s, d)).reshape(b * s, d)
    return rows, pl.BlockSpec((tm, d), lambda i, *_: (i, 0))


def _ada_kernel(c_ref, w_ref, b_ref, o_ref):
    c = c_ref[...]
    act = (c * jax.nn.sigmoid(c)).astype(BF16)
    o_ref[...] = jnp.dot(act, w_ref[...].astype(BF16), preferred_element_type=F32) + b_ref[...]


def ada_mod(c, w, b):
    r, d = c.shape
    nl, _, n = w.shape
    tn = n // 4
    return pl.pallas_call(
        _ada_kernel,
        out_shape=jax.ShapeDtypeStruct((nl, r, n), F32),
        grid=(nl, n // tn),
        in_specs=[pl.BlockSpec((r, d), lambda l, j: (0, 0)),
                  pl.BlockSpec((None, d, tn), lambda l, j: (l, 0, j)),
                  pl.BlockSpec((None, 1, tn), lambda l, j: (l, 0, j))],
        out_specs=pl.BlockSpec((None, r, tn), lambda l, j: (l, 0, j)),
        compiler_params=_params("parallel", "parallel"),
        name="ada_mod",
    )(c, w, b.reshape(nl, 1, n))


def _mod_matmul_kernel(x_ref, sc_ref, sh_ref, w_ref, o_ref, h_ref):
    @pl.when(pl.program_id(1) == 0)
    def _():
        h = x_ref[...] * (1.0 + sc_ref[...]) + sh_ref[...]
        h_ref[...] = h.astype(BF16)

    o_ref[...] = jnp.dot(h_ref[...], w_ref[...], preferred_element_type=F32)


def project(x, scale, shift, w, b, s):
    t, d = x.shape
    npad = _round_up(w.shape[1], LANES)
    wp = jnp.pad(w, ((0, 0), (0, npad - w.shape[1]))).astype(BF16)
    tn = next(c for c in (1024, 768, 640, 512, 384, 256, 128) if npad % c == 0)
    tm = _row_tile(t)
    sc, mod_spec = _mod_operand(scale, b, s, tm)
    sh, _ = _mod_operand(shift, b, s, tm)
    return pl.pallas_call(
        _mod_matmul_kernel,
        out_shape=(jax.ShapeDtypeStruct((t, npad), F32), jax.ShapeDtypeStruct((t, d), BF16)),
        grid=(t // tm, npad // tn),
        in_specs=[pl.BlockSpec((tm, d), lambda i, j: (i, 0)), mod_spec, mod_spec,
                  pl.BlockSpec((d, tn), lambda i, j: (0, j))],
        out_specs=(pl.BlockSpec((tm, tn), lambda i, j: (i, j)), pl.BlockSpec((tm, d), lambda i, j: (i, 0))),
        compiler_params=_params("parallel", "arbitrary"),
        name="mod_matmul",
    )(x, sc, sh, wp)


def _deepnorm(x, sub, gate, g, b):
    y = DN_ALPHA * x + (1.0 + gate) * sub
    mu = jnp.mean(y, axis=-1, keepdims=True)
    var = jnp.mean(jnp.square(y - mu), axis=-1, keepdims=True)
    return (y - mu) * lax.rsqrt(var + LN_EPS) * g + b


def _post_gla_kernel(o_ref, r_ref, x_ref, gate_ref, gn_ref, w_ref, g_ref, b_ref, y_ref):
    parts = []
    for h in range(GLA_HEADS):
        cols = slice(h * GLA_DV, (h + 1) * GLA_DV)
        o = o_ref[:, cols]
        mu = jnp.mean(o, axis=-1, keepdims=True)
        var = jnp.mean(jnp.square(o - mu), axis=-1, keepdims=True)
        parts.append((o - mu) * lax.rsqrt(var + LN_EPS) * gn_ref[:, cols])
    r = r_ref[...]
    f = (jnp.concatenate(parts, axis=1) * (r * jax.nn.sigmoid(r))).astype(BF16)
    sub = jnp.dot(f, w_ref[...], preferred_element_type=F32)
    y_ref[...] = _deepnorm(x_ref[...], sub, gate_ref[...], g_ref[...], b_ref[...])


def _post_matmul_kernel(o_ref, x_ref, gate_ref, w_ref, g_ref, b_ref, y_ref):
    sub = jnp.dot(o_ref[...].astype(BF16), w_ref[...], preferred_element_type=F32)
    y_ref[...] = _deepnorm(x_ref[...], sub, gate_ref[...], g_ref[...], b_ref[...])


def _post_plain_kernel(o_ref, x_ref, gate_ref, g_ref, b_ref, y_ref):
    y_ref[...] = _deepnorm(x_ref[...], o_ref[...], gate_ref[...], g_ref[...], b_ref[...])


def post(x, gate, ln_g, ln_b, b, s, *, sub=None, w_out=None, gla_proj=None, gla_gn=None):
    t, d = x.shape
    tm = _row_tile(t)
    gate_arr, gate_spec = _mod_operand(gate, b, s, tm)
    row = pl.BlockSpec((tm, d), lambda i: (i, 0))
    vec = pl.BlockSpec((1, d), lambda i: (0, 0))
    mat = pl.BlockSpec((d, d), lambda i: (0, 0))
    g2, b2 = ln_g.reshape(1, d), ln_b.reshape(1, d)
    if gla_proj is not None:
        r_spec = pl.BlockSpec((tm, GLA_HV), lambda i: (i, (2 * GLA_HK + GLA_HV) // GLA_HV))
        args = (sub, gla_proj, x, gate_arr, gla_gn.reshape(1, d), w_out.astype(BF16), g2, b2)
        specs = [row, r_spec, row, gate_spec, vec, mat, vec, vec]
        body = _post_gla_kernel
    elif w_out is not None:
        args = (sub, x, gate_arr, w_out.astype(BF16), g2, b2)
        specs = [row, row, gate_spec, mat, vec, vec]
        body = _post_matmul_kernel
    else:
        args = (sub, x, gate_arr, g2, b2)
        specs = [row, row, gate_spec, vec, vec]
        body = _post_plain_kernel
    return pl.pallas_call(
        body,
        out_shape=jax.ShapeDtypeStruct((t, d), F32),
        grid=(t // tm,),
        in_specs=specs,
        out_specs=row,
        compiler_params=_params("parallel"),
        name="post",
    )(*args)


GLA_MIN_ROWS = 64


def _cumsum_rows(x):
    n = x.shape[0]
    row = lax.broadcasted_iota(jnp.int32, (n, 1), 0)
    shift = 1
    while shift < n:
        x = x + jnp.where(row >= shift, pltpu.roll(x, shift, 0), 0.0)
        shift *= 2
    return x


def _gla_kernel(has_s0, q_ref, k_ref, v_ref, a_ref, wa_ref, ba_ref, *rest):
    if has_s0:
        s0_ref, o_ref, st_ref, s_scr = rest
    else:
        o_ref, st_ref, s_scr = rest
    c = pl.program_id(1)
    n_rows = q_ref.shape[0]
    rows = max(n_rows, GLA_MIN_ROWS)

    @pl.when(c == 0)
    def _():
        s_scr[...] = s0_ref[...] if has_s0 else jnp.zeros_like(s_scr)

    def padded(x):
        if rows == n_rows:
            return x
        return jnp.concatenate([x, jnp.zeros((rows - n_rows, x.shape[1]), x.dtype)], axis=0)

    z = jnp.dot(a_ref[...], wa_ref[...], precision=HIGHEST, preferred_element_type=F32) + ba_ref[...]
    logg = (jnp.minimum(z, 0.0) - jnp.log1p(jnp.exp(-jnp.abs(z)))) / GLA_TAU
    b = _cumsum_rows(padded(logg))
    q, k, v = padded(q_ref[...]), padded(k_ref[...]), padded(v_ref[...])
    qe = q * (GLA_DK ** -0.5) * jnp.exp(b)
    ke = k * jnp.exp(-b)
    b_last = b[rows - 1:rows, :]
    kd = k * jnp.exp(b_last - b)
    e_last = jnp.exp(b_last)
    causal = lax.broadcasted_iota(jnp.int32, (rows, 1), 0) >= lax.broadcasted_iota(jnp.int32, (1, rows), 1)
    eye = lax.broadcasted_iota(jnp.int32, (GLA_DK, 1), 0) == lax.broadcasted_iota(jnp.int32, (1, GLA_DK), 1)
    for h in range(GLA_HEADS):
        ck = slice(h * GLA_DK, (h + 1) * GLA_DK)
        cv = slice(h * GLA_DV, (h + 1) * GLA_DV)
        att = lax.dot_general(qe[:, ck], ke[:, ck], NT, precision=HIGHEST, preferred_element_type=F32)
        att = jnp.where(causal, att, 0.0)
        s_h = s_scr[h]
        o = (jnp.dot(att, v[:, cv], precision=HIGHEST, preferred_element_type=F32)
             + jnp.dot(qe[:, ck], s_h, precision=HIGHEST, preferred_element_type=F32))
        o_ref[:, cv] = o[:n_rows]
        e_col = jnp.sum(jnp.where(eye, e_last[:, ck], 0.0), axis=1, keepdims=True)
        s_scr[h] = e_col * s_h + lax.dot_general(kd[:, ck], v[:, cv], TN, precision=HIGHEST,
                                                 preferred_element_type=F32)

    @pl.when(c == pl.num_programs(1) - 1)
    def _():
        st_ref[...] = s_scr[...]


def gla_recurrence(proj, w_a2, b_a2, s0, b, s):
    chunk = GLA_CHUNK if s % GLA_CHUNK == 0 else s
    nc = s // chunk
    wa = jnp.pad(w_a2, ((0, LANES - w_a2.shape[0]), (0, 0)))
    a_block = (2 * GLA_HK + 2 * GLA_HV) // LANES
    row = lambda width, blk: pl.BlockSpec((chunk, width), lambda bi, c: (bi * nc + c, blk))
    st_spec = pl.BlockSpec((None, GLA_HEADS, GLA_DK, GLA_DV), lambda bi, c: (bi, 0, 0, 0))
    in_specs = [row(GLA_HK, 0), row(GLA_HK, 1), row(GLA_HV, 2 * GLA_HK // GLA_HV), row(LANES, a_block),
                pl.BlockSpec(wa.shape, lambda bi, c: (0, 0)), pl.BlockSpec((1, GLA_HK), lambda bi, c: (0, 0))]
    args = [proj, proj, proj, proj, wa, b_a2.reshape(1, GLA_HK)]
    if s0 is not None:
        in_specs.append(st_spec)
        args.append(s0)
    return pl.pallas_call(
        functools.partial(_gla_kernel, s0 is not None),
        out_shape=(jax.ShapeDtypeStruct((b * s, GLA_HV), F32),
                   jax.ShapeDtypeStruct((b, GLA_HEADS, GLA_DK, GLA_DV), F32)),
        grid=(b, nc),
        in_specs=in_specs,
        out_specs=(pl.BlockSpec((chunk, GLA_HV), lambda bi, c: (bi * nc + c, 0)), st_spec),
        scratch_shapes=[pltpu.VMEM((GLA_HEADS, GLA_DK, GLA_DV), F32)],
        compiler_params=_params("parallel", "arbitrary"),
        name="gla_recurrence",
    )(*args)


def _nsa_q_perm():
    cols = []
    for m in range(NSA_KV_HEADS // 2):
        for r in range(NSA_GROUP):
            for half in range(2):
                head = (2 * m + half) * NSA_GROUP + r
                cols.append(np.arange(NSA_HD) + head * NSA_HD)
    return np.concatenate(cols)


def _nsa_gate_expand():
    perm = _nsa_q_perm()
    ex = np.zeros((LANES, 3 * NSA_QD), np.float32)
    for br in range(3):
        for col in range(NSA_QD):
            ex[br * NSA_HEADS + perm[col] // NSA_HD, br * NSA_QD + col] = 1.0
    return ex


def _nsa_slc_matrix(nc_pad, nsb_pad):
    ratio = SEL_BLOCK // CMP_STRIDE
    m = np.zeros((nc_pad, nsb_pad), np.float32)
    for j in range(nsb_pad):
        for o in range(CMP_LEN // CMP_STRIDE):
            for i in range(ratio):
                n = ratio * j + i + o - (CMP_LEN // CMP_STRIDE - 1)
                if 0 <= n < nc_pad:
                    m[n, j] += 1.0
    return m


def _rope_tables(pos):
    half = ROT_DIM // 2
    inv = ROPE_THETA ** (-jnp.arange(half, dtype=F32) * 2.0 / ROT_DIM)
    ang = pos.astype(F32)[:, None] * inv[None, :]
    cos, sin = jnp.cos(ang), jnp.sin(ang)
    t = pos.shape[0]
    ones = jnp.ones((t, NSA_HD - ROT_DIM), F32)
    zeros = jnp.zeros((t, NSA_HD - ROT_DIM), F32)
    z8 = jnp.zeros((t, half), F32)
    c = jnp.concatenate([cos, cos, ones], 1)
    up = jnp.concatenate([-sin, z8, zeros], 1)
    dn = jnp.concatenate([z8, sin, zeros], 1)
    two = lambda a: jnp.concatenate([a, a], 1)
    return two(c), two(up), two(dn)


def _nsa_prep_kernel(p_ref, cos_ref, up_ref, dn_ref, q_ref, kc_ref, vc_ref, ks_ref, vs_ref, kw_ref, vw_ref,
                     ksb_ref, vsb_ref, kwb_ref, vwb_ref, g_ref):
    def rope(x):
        reps = x.shape[1] // LANES
        tile = lambda a: jnp.concatenate([a] * reps, axis=1)
        w = x.shape[1]
        return (x * tile(cos_ref[...]) + pltpu.roll(x, w - ROT_DIM // 2, 1) * tile(up_ref[...])
                + pltpu.roll(x, ROT_DIM // 2, 1) * tile(dn_ref[...]))

    q_ref[...] = (rope(p_ref[:, :NSA_QD]) * (NSA_HD ** -0.5)).astype(q_ref.dtype)
    kv = lambda k: p_ref[:, NSA_QD + k * NSA_KVD:NSA_QD + (k + 1) * NSA_KVD]
    kc_ref[...] = rope(kv(0))
    vc_ref[...] = kv(1)
    ks = rope(kv(2))
    ks_ref[...] = ks
    ksb_ref[...] = ks.astype(BF16)
    vs_ref[...] = kv(3)
    vsb_ref[...] = kv(3).astype(BF16)
    kw = rope(kv(4))
    kw_ref[...] = kw
    kwb_ref[...] = kw.astype(BF16)
    vw_ref[...] = kv(5)
    vwb_ref[...] = kv(5).astype(BF16)
    g0 = NSA_QD + 6 * NSA_KVD
    g_ref[...] = jax.nn.sigmoid(p_ref[:, g0:g0 + LANES])


def nsa_prep(p, cos, up, dn, *, q_dtype):
    t = p.shape[0]
    tm = _row_tile(t)
    period = cos.shape[0] // tm
    tab = pl.BlockSpec((tm, LANES), lambda i: (i % period, 0))
    kv32 = jax.ShapeDtypeStruct((t, NSA_KVD), F32)
    kv16 = jax.ShapeDtypeStruct((t, NSA_KVD), BF16)
    kvs = pl.BlockSpec((tm, NSA_KVD), lambda i: (i, 0))
    return pl.pallas_call(
        _nsa_prep_kernel,
        out_shape=(jax.ShapeDtypeStruct((t, NSA_QD), q_dtype),) + (kv32,) * 6 + (kv16,) * 4
        + (jax.ShapeDtypeStruct((t, LANES), F32),),
        grid=(t // tm,),
        in_specs=[pl.BlockSpec((tm, p.shape[1]), lambda i: (i, 0)), tab, tab, tab],
        out_specs=(pl.BlockSpec((tm, NSA_QD), lambda i: (i, 0)),) + (kvs,) * 10
        + (pl.BlockSpec((tm, LANES), lambda i: (i, 0)),),
        compiler_params=_params("parallel"),
        name="nsa_prep",
    )(p, cos, up, dn)


def _block_sums(x, w):
    n_sub = x.shape[0] // CMP_STRIDE
    x = x.reshape(n_sub, CMP_STRIDE, NSA_KVD)
    w = 1.0 + w
    return jnp.sum(x * w[None, :CMP_STRIDE], axis=1), jnp.sum(x * w[None, CMP_STRIDE:], axis=1)


def _nsa_compress_kernel(x_ref, w_ref, o_ref):
    first, second = _block_sums(x_ref[...], w_ref[...])
    n_sub = first.shape[0]
    nxt = pltpu.roll(second, n_sub - 1, 0)
    row = lax.broadcasted_iota(jnp.int32, (n_sub, 1), 0)
    o_ref[...] = jnp.where(row < n_sub - 1, (first + nxt) / CMP_LEN, 0.0).astype(o_ref.dtype)


def nsa_compress(x, pe, b, s):
    n_sub = s // CMP_STRIDE
    return pl.pallas_call(
        _nsa_compress_kernel,
        out_shape=jax.ShapeDtypeStruct((b, n_sub, NSA_KVD), BF16),
        grid=(b,),
        in_specs=[pl.BlockSpec((s, NSA_KVD), lambda i: (i, 0)),
                  pl.BlockSpec((CMP_LEN, NSA_KVD), lambda i: (0, 0))],
        out_specs=pl.BlockSpec((None, n_sub, NSA_KVD), lambda i: (i, 0, 0)),
        compiler_params=_params("parallel"),
        name="nsa_compress",
    )(x, pe.reshape(CMP_LEN, NSA_KVD))


NSA_KV_TILE = 512


def _mask_rows(s, allowed):
    if allowed.shape[0] == s.shape[0]:
        return jnp.where(allowed, s, -jnp.inf)
    reps = s.shape[0] // allowed.shape[0]
    s3 = s.reshape(reps, allowed.shape[0], s.shape[1])
    return jnp.where(allowed[None], s3, -jnp.inf).reshape(s.shape)


def _softmax_rows(s, allowed):
    s = _mask_rows(s, allowed)
    m = jnp.max(s, axis=1, keepdims=True)
    m = jnp.where(m == -jnp.inf, 0.0, m)
    e = jnp.exp(s - m)
    return e / jnp.maximum(jnp.sum(e, axis=1, keepdims=True), 1e-30)


def _select_blocks(score, n_sel):
    nb = score.shape[1]
    lane = lax.broadcasted_iota(jnp.int32, score.shape, 1).astype(F32)
    sel = jnp.zeros(score.shape, F32)
    for _ in range(n_sel):
        m = jnp.max(score, axis=1, keepdims=True)
        first = jnp.min(jnp.where(score == m, lane, float(nb)), axis=1, keepdims=True)
        pick = lane == first
        sel = jnp.where(pick, 1.0, sel)
        score = jnp.where(pick, -jnp.inf, score)
    return sel


def _block_scores(p_slc, pos_q):
    jb = lax.broadcasted_iota(jnp.int32, (1, p_slc.shape[1]), 1)
    cur = pos_q // SEL_BLOCK
    valid = jb * SEL_BLOCK <= pos_q
    forced = (jb == 0) | (jb == cur) | (jb == cur - 1)
    return jnp.where(forced, FORCE_SCORE, jnp.where(valid, p_slc, -1.0))


def _select_blocks_t(score, n_sel):
    st = score.T
    nb = st.shape[0]
    blk = lax.broadcasted_iota(jnp.int32, st.shape, 0).astype(F32)
    sel = jnp.zeros(st.shape, F32)
    for _ in range(n_sel):
        m = jnp.max(st, axis=0, keepdims=True)
        first = jnp.min(jnp.where(st == m, blk, float(nb)), axis=0, keepdims=True)
        pick = blk == first
        sel = jnp.where(pick, 1.0, sel)
        st = jnp.where(pick, -jnp.inf, st)
    return sel.T


def _online_softmax_step(s, allowed, v, m_old, l_old, acc_old, feature_major=False):
    s = _mask_rows(s, allowed)
    m_new = jnp.maximum(m_old, jnp.max(s, axis=1, keepdims=True))
    m_use = jnp.where(m_new == -jnp.inf, 0.0, m_new)
    alpha = jnp.exp(m_old - m_use)
    p = jnp.exp(s - m_use)
    l_new = alpha * l_old + jnp.sum(p, axis=1, keepdims=True)
    if feature_major:
        pv = lax.dot_general(p.astype(BF16), v, NT, preferred_element_type=F32)
    else:
        pv = jnp.dot(p.astype(BF16), v, preferred_element_type=F32)
    return m_new, l_new, alpha * acc_old + pv


def _padded_query(tile, m, in_half):
    sel = jnp.where(in_half, tile, jnp.zeros_like(tile))
    z = jnp.zeros_like(sel)
    return jnp.concatenate([sel, z] if m == 0 else [z, sel], axis=1)


def _nsa_attn_kernel(q_ref, g_ref, kcmp_ref, vcmp_ref, ks_ref, vs_ref, kw_ref, vw_ref, slc_ref, ex_ref,
                     o_ref, qz_ref, sel_ref, oc_ref, m_ref, l_ref, acc_ref):
    i = pl.program_id(1)
    start = i * Q_BLOCK
    pos_q = start + lax.broadcasted_iota(jnp.int32, (Q_BLOCK, 1), 0)
    lane128 = lax.broadcasted_iota(jnp.int32, (1, LANES), 1)
    n_cmp = kcmp_ref.shape[1]
    cmp_end = CMP_STRIDE * lax.broadcasted_iota(jnp.int32, (1, n_cmp), 1) + (CMP_LEN - 1)
    cmp_ok = cmp_end <= pos_q
    gexp = jnp.dot(g_ref[...], ex_ref[...], precision=HIGHEST, preferred_element_type=F32)
    o_ref[...] = jnp.zeros_like(o_ref)

    win_base = pl.multiple_of(jnp.maximum(start - WINDOW, 0), Q_BLOCK)
    band = WINDOW + Q_BLOCK
    kw_t = jnp.concatenate([kw_ref[win_base // Q_BLOCK + u] for u in range(band // Q_BLOCK)], axis=1)
    vw_t = vw_ref[pl.ds(win_base, band), :]
    dpos = pos_q - (win_base + lax.broadcasted_iota(jnp.int32, (1, band), 1))
    win_ok = (dpos >= 0) & (dpos < WINDOW)

    for g in range(NSA_KV_HEADS):
        m, half = divmod(g, 2)
        in_half = (lane128 // NSA_HD) == half
        qz = jnp.concatenate(
            [_padded_query(q_ref[:, (m * NSA_GROUP + r) * LANES:(m * NSA_GROUP + r + 1) * LANES], m, in_half)
             for r in range(NSA_GROUP)], axis=0)
        qz_ref[g] = qz
        s = jnp.dot(qz, kcmp_ref[...], preferred_element_type=F32)
        p = _softmax_rows(s, cmp_ok)
        oc_ref[g] = jnp.dot(p.astype(BF16), vcmp_ref[...], preferred_element_type=F32)[:, m * LANES:(m + 1) * LANES]
        imp = p[0:Q_BLOCK]
        for r in range(1, NSA_GROUP):
            imp = imp + p[r * Q_BLOCK:(r + 1) * Q_BLOCK]
        p_slc = jnp.dot(imp, slc_ref[...], precision=HIGHEST, preferred_element_type=F32)
        sel_ref[g] = _select_blocks_t(_block_scores(p_slc, pos_q), N_SEL).astype(BF16)

    m_ref[...] = jnp.full_like(m_ref, -jnp.inf)
    l_ref[...] = jnp.zeros_like(l_ref)
    acc_ref[...] = jnp.zeros_like(acc_ref)

    def kv_step(t, carry):
        k0 = pl.multiple_of(t * NSA_KV_TILE, NSA_KV_TILE)
        k_t = ks_ref[t]
        v_t = vs_ref[pl.ds(k0, NSA_KV_TILE), :]
        key = k0 + lax.broadcasted_iota(jnp.int32, (1, NSA_KV_TILE), 1)
        blk = lax.broadcasted_iota(jnp.int32, (LANES, 1), 0)
        expand = jnp.where(blk == key // SEL_BLOCK, 1.0, 0.0).astype(BF16)
        causal = key <= pos_q
        for g in range(NSA_KV_HEADS):
            chosen = jnp.dot(sel_ref[g], expand, preferred_element_type=F32)
            s = jnp.dot(qz_ref[g], k_t, preferred_element_type=F32)
            m_ref[g], l_ref[g], acc_ref[g] = _online_softmax_step(s, (chosen > 0.5) & causal, v_t,
                                                                  m_ref[g], l_ref[g], acc_ref[g])
        return carry

    lax.fori_loop(0, (start + Q_BLOCK + NSA_KV_TILE - 1) // NSA_KV_TILE, kv_step, 0)

    for g in range(NSA_KV_HEADS):
        m, half = divmod(g, 2)
        in_half = (lane128 // NSA_HD) == half
        mcols = slice(m * LANES, (m + 1) * LANES)
        o_s = acc_ref[g][:, mcols] / jnp.maximum(l_ref[g], 1e-30)
        s = jnp.dot(qz_ref[g], kw_t, preferred_element_type=F32)
        o_w = jnp.dot(_softmax_rows(s, win_ok).astype(BF16), vw_t, preferred_element_type=F32)[:, mcols]
        o_c = oc_ref[g]
        for r in range(NSA_GROUP):
            t = m * NSA_GROUP + r
            cols = slice(t * LANES, (t + 1) * LANES)
            rows = slice(r * Q_BLOCK, (r + 1) * Q_BLOCK)
            comb = (gexp[:, cols] * o_c[rows] + gexp[:, NSA_QD + t * LANES:NSA_QD + (t + 1) * LANES] * o_s[rows]
                    + gexp[:, 2 * NSA_QD + t * LANES:2 * NSA_QD + (t + 1) * LANES] * o_w[rows])
            o_ref[:, cols] += jnp.where(in_half, comb, 0.0)


def nsa_attn_prompt(q, gates, kcmp, vcmp, ks, vs, kw, vw, b, s):
    nq = s // Q_BLOCK
    rows = NSA_GROUP * Q_BLOCK
    slc = jnp.asarray(_nsa_slc_matrix(s // CMP_STRIDE, LANES))
    ex = jnp.asarray(_nsa_gate_expand())
    seq = pl.BlockSpec((s, NSA_KVD), lambda bi, i: (bi, 0))
    n_cmp = s // CMP_STRIDE
    cmp_spec = pl.BlockSpec((None, n_cmp, NSA_KVD), lambda bi, i: (bi, 0, 0))
    cmp_t_spec = pl.BlockSpec((None, NSA_KVD, n_cmp), lambda bi, i: (bi, 0, 0))

    def key_tiles(k, tile):
        kt = k.reshape(b, s // tile, tile, NSA_KVD).transpose(0, 1, 3, 2)
        return kt, pl.BlockSpec((None, s // tile, NSA_KVD, tile), lambda bi, i: (bi, 0, 0, 0))

    ks, ks_spec = key_tiles(ks, NSA_KV_TILE)
    kw, kw_spec = key_tiles(kw, Q_BLOCK)
    kcmp = kcmp.transpose(0, 2, 1)
    return pl.pallas_call(
        _nsa_attn_kernel,
        out_shape=jax.ShapeDtypeStruct((b * s, NSA_QD), F32),
        grid=(b, nq),
        in_specs=[
            pl.BlockSpec((Q_BLOCK, NSA_QD), lambda bi, i: (bi * nq + i, 0)),
            pl.BlockSpec((Q_BLOCK, LANES), lambda bi, i: (bi * nq + i, 0)),
            cmp_t_spec, cmp_spec, ks_spec, seq, kw_spec, seq,
            pl.BlockSpec(slc.shape, lambda bi, i: (0, 0)),
            pl.BlockSpec(ex.shape, lambda bi, i: (0, 0)),
        ],
        out_specs=pl.BlockSpec((Q_BLOCK, NSA_QD), lambda bi, i: (bi * nq + i, 0)),
        scratch_shapes=[pltpu.VMEM((NSA_KV_HEADS, rows, NSA_KVD), BF16),
                        pltpu.VMEM((NSA_KV_HEADS, Q_BLOCK, LANES), BF16),
                        pltpu.VMEM((NSA_KV_HEADS, rows, LANES), F32),
                        pltpu.VMEM((NSA_KV_HEADS, rows, 1), F32),
                        pltpu.VMEM((NSA_KV_HEADS, rows, 1), F32),
                        pltpu.VMEM((NSA_KV_HEADS, rows, NSA_KVD), F32)],
        compiler_params=_params("parallel", "arbitrary"),
        name="nsa_attn_prompt",
    )(q, gates, kcmp, vcmp, ks, vs, kw, vw, slc, ex)


PAGES_PER_STEP = 8
SUM_PAGES_PER_STEP = 16


def _feature_major(cache):
    n, tokens = cache.shape[:2]
    return cache.transpose(0, 2, 3, 1).reshape(n, NSA_KVD, tokens)


def _page_specs(n, page):
    def one(u):
        return pl.BlockSpec((None, NSA_KVD, page), lambda bi, j, pt: (pt[bi, j * n + u], 0, 0))
    return [one(u) for u in range(n)]


def _nsa_page_sums_kernel(pt_ref, *refs):
    n = SUM_PAGES_PER_STEP
    kp, vp = refs[:n], refs[n:2 * n]
    wk1_ref, wk2_ref, wv1_ref, wv2_ref, grp_ref, fk_ref, sk_ref, fv_ref, sv_ref = refs[2 * n:]

    def sums(pages, w_ref, o_ref):
        xw = jnp.concatenate([(p[...] * w_ref[...]).astype(BF16) for p in pages], axis=1)
        o_ref[...] = jnp.dot(xw, grp_ref[...], preferred_element_type=F32)

    sums(kp, wk1_ref, fk_ref)
    sums(kp, wk2_ref, sk_ref)
    sums(vp, wv1_ref, fv_ref)
    sums(vp, wv2_ref, sv_ref)


def nsa_page_sums(page_table, pool_k, pool_v, pe_k, pe_v):
    db, n_pages = page_table.shape
    page = pool_k.shape[2]
    n = SUM_PAGES_PER_STEP
    per_page = page // CMP_STRIDE
    assert n * per_page == LANES
    out = jax.ShapeDtypeStruct((db, NSA_KVD, n_pages * per_page), F32)
    out_spec = pl.BlockSpec((None, NSA_KVD, LANES), lambda bi, j, pt: (bi, 0, j))
    w_spec = pl.BlockSpec((NSA_KVD, page), lambda bi, j, pt: (0, 0))
    halves = lambda pe: [jnp.tile((1.0 + pe.reshape(CMP_LEN, NSA_KVD)[o:o + CMP_STRIDE]).T, (1, per_page))
                         for o in (0, CMP_STRIDE)]
    grp = np.zeros((n * page, LANES), np.float32)
    tok = np.arange(n * page)
    grp[tok, tok // CMP_STRIDE] = 1.0
    grp = jnp.asarray(grp, BF16)
    return pl.pallas_call(
        _nsa_page_sums_kernel,
        out_shape=(out,) * 4,
        grid_spec=pltpu.PrefetchScalarGridSpec(
            num_scalar_prefetch=1,
            grid=(db, n_pages // n),
            in_specs=_page_specs(n, page) * 2 + [w_spec] * 4 + [pl.BlockSpec(grp.shape, lambda bi, j, pt: (0, 0))],
            out_specs=(out_spec,) * 4,
        ),
        compiler_params=_params("parallel", "arbitrary"),
        name="nsa_page_sums",
    )(page_table, *([pool_k] * n), *([pool_v] * n), *halves(pe_k), *halves(pe_v), grp)


def _decode_queries(q_ref, g):
    m, half = divmod(g, 2)
    in_half = (lax.broadcasted_iota(jnp.int32, (1, LANES), 1) // NSA_HD) == half
    tiles = [_padded_query(q_ref[:, (m * NSA_GROUP + r) * LANES:(m * NSA_GROUP + r + 1) * LANES], m, in_half)
             for r in range(NSA_GROUP)]
    return jnp.concatenate(tiles, axis=0).astype(BF16)


def _group_rows(x):
    return jnp.concatenate([x] * NSA_GROUP, axis=0)


def _nsa_decode_select_kernel(past, q_ref, fk_ref, sk_ref, fv_ref, sv_ref, kn_ref, vn_ref, wk_ref, wv_ref,
                              slc_ref, sel_ref, oc_ref):
    nq = q_ref.shape[0]
    n_cmp = fk_ref.shape[1]
    col = lax.broadcasted_iota(jnp.int32, (1, n_cmp), 1)
    eye = (lax.broadcasted_iota(jnp.int32, (NSA_KVD, 1), 0)
           == lax.broadcasted_iota(jnp.int32, (1, NSA_KVD), 1))

    def summaries(f_ref, s_ref, new_ref, w_ref):
        w = 1.0 + w_ref[CMP_STRIDE:CMP_STRIDE + nq, :]
        second_new = jnp.sum(new_ref[...] * w, axis=0, keepdims=True)
        new_col = jnp.sum(jnp.where(eye, second_new, 0.0), axis=1, keepdims=True)
        nxt = jnp.where(col == n_cmp - 1, new_col, pltpu.roll(s_ref[...], n_cmp - 1, 1))
        return ((f_ref[...] + nxt) / CMP_LEN).astype(BF16)

    kcmp_t = summaries(fk_ref, sk_ref, kn_ref, wk_ref)
    vcmp_t = summaries(fv_ref, sv_ref, vn_ref, wv_ref)
    pos_q = past + lax.broadcasted_iota(jnp.int32, (nq, 1), 0)
    cmp_end = CMP_STRIDE * lax.broadcasted_iota(jnp.int32, (1, n_cmp), 1) + (CMP_LEN - 1)
    cmp_ok = _group_rows(cmp_end <= pos_q)
    lane128 = lax.broadcasted_iota(jnp.int32, (1, LANES), 1)
    oc_ref[...] = jnp.zeros_like(oc_ref)
    for g in range(NSA_KV_HEADS):
        m, half = divmod(g, 2)
        in_half = (lane128 // NSA_HD) == half
        s = jnp.dot(_decode_queries(q_ref, g), kcmp_t, preferred_element_type=F32)
        p = _softmax_rows(s, cmp_ok)
        o_c = lax.dot_general(p.astype(BF16), vcmp_t, NT,
                              preferred_element_type=F32)[:, m * LANES:(m + 1) * LANES]
        imp = p[0:nq]
        for r in range(1, NSA_GROUP):
            imp = imp + p[r * nq:(r + 1) * nq]
        for r in range(NSA_GROUP):
            cols = slice((m * NSA_GROUP + r) * LANES, (m * NSA_GROUP + r + 1) * LANES)
            oc_ref[:, cols] += jnp.where(in_half, o_c[r * nq:(r + 1) * nq], 0.0)
        p_slc = jnp.dot(imp, slc_ref[...], precision=HIGHEST, preferred_element_type=F32)
        sel = _group_rows(_select_blocks(_block_scores(p_slc, pos_q), N_SEL)).astype(BF16)
        rows = slice(g * NSA_GROUP * nq, (g + 1) * NSA_GROUP * nq)
        for lt in range(sel_ref.shape[0]):
            sel_ref[lt, rows, :] = sel[:, lt * LANES:(lt + 1) * LANES]


def nsa_decode_select(q, sums, kc_new, vc_new, pe_k, pe_v, db, nq, past):
    n_cmp = sums[0].shape[2]
    nsb_pad = _round_up(-(-(past + nq) // SEL_BLOCK), LANES)
    sel_rows = NSA_KV_HEADS * NSA_GROUP * nq
    slc = jnp.asarray(_nsa_slc_matrix(n_cmp, nsb_pad))
    cmp_spec = pl.BlockSpec((None, NSA_KVD, n_cmp), lambda bi: (bi, 0, 0))
    new_spec = pl.BlockSpec((nq, NSA_KVD), lambda bi: (bi, 0))
    pe_spec = pl.BlockSpec((CMP_LEN, NSA_KVD), lambda bi: (0, 0))
    return pl.pallas_call(
        functools.partial(_nsa_decode_select_kernel, past),
        out_shape=(jax.ShapeDtypeStruct((db, nsb_pad // LANES, sel_rows, LANES), BF16),
                   jax.ShapeDtypeStruct((db * nq, NSA_QD), F32)),
        grid=(db,),
        in_specs=[pl.BlockSpec((nq, NSA_QD), lambda bi: (bi, 0)), cmp_spec, cmp_spec, cmp_spec, cmp_spec,
                  new_spec, new_spec, pe_spec, pe_spec, pl.BlockSpec(slc.shape, lambda bi: (0, 0))],
        out_specs=(pl.BlockSpec((None, nsb_pad // LANES, sel_rows, LANES), lambda bi: (bi, 0, 0, 0)),
                   pl.BlockSpec((nq, NSA_QD), lambda bi: (bi, 0))),
        compiler_params=_params("parallel"),
        name="nsa_decode_select",
    )(q, *sums, kc_new, vc_new, pe_k.reshape(CMP_LEN, NSA_KVD), pe_v.reshape(CMP_LEN, NSA_KVD), slc)


def _nsa_decode_attend_kernel(past, pt_ref, q_ref, sel_ref, g_ref, oc_ref, ksn_ref, vsn_ref, wk_ref, wv_ref,
                              kwn_ref, vwn_ref, ex_ref, xp_ref, *refs):
    n = PAGES_PER_STEP
    kp, vp = refs[:n], refs[n:2 * n]
    o_ref, qz_scr, m_scr, l_scr, acc_scr = refs[2 * n:]
    j = pl.program_id(1)
    nq = q_ref.shape[0]
    n_keys = n * kp[0].shape[1]
    steps_per_tile = xp_ref.shape[0]
    pos_q = jnp.concatenate([past + lax.broadcasted_iota(jnp.int32, (nq, 1), 0)] * (NSA_KV_HEADS * NSA_GROUP),
                            axis=0)

    @pl.when(j == 0)
    def _():
        m_scr[...] = jnp.full_like(m_scr, -jnp.inf)
        l_scr[...] = jnp.zeros_like(l_scr)
        acc_scr[...] = jnp.zeros_like(acc_scr)
        qz_scr[...] = jnp.concatenate([_decode_queries(q_ref, g) for g in range(NSA_KV_HEADS)], axis=0)

    def attend(s, allowed, v, feature_major):
        m_scr[...], l_scr[...], acc_scr[...] = _online_softmax_step(
            s, allowed, v, m_scr[...], l_scr[...], acc_scr[...], feature_major)

    k_t = jnp.concatenate([r[...].astype(BF16) for r in kp], axis=1)
    v_t = jnp.concatenate([r[...].astype(BF16) for r in vp], axis=1)
    key = j * n_keys + lax.broadcasted_iota(jnp.int32, (1, n_keys), 1)
    chosen = jnp.dot(sel_ref[j // steps_per_tile], xp_ref[j % steps_per_tile], preferred_element_type=F32)
    attend(jnp.dot(qz_scr[...], k_t, preferred_element_type=F32), (chosen > 0.5) & (key <= pos_q), v_t, True)

    @pl.when(j == pl.num_programs(1) - 1)
    def _():
        qz = qz_scr[...]
        lane128 = lax.broadcasted_iota(jnp.int32, (1, LANES), 1)
        n_new = ksn_ref.shape[0]
        new_idx = lax.broadcasted_iota(jnp.int32, (1, n_new), 1)
        new_key = past + new_idx
        new_blk = past // SEL_BLOCK
        chosen_new = sel_ref[new_blk // LANES][:, new_blk % LANES:new_blk % LANES + 1].astype(F32) > 0.5
        attend(lax.dot_general(qz, ksn_ref[...], NT, preferred_element_type=F32),
               chosen_new & (new_key <= pos_q) & (new_idx < nq), vsn_ref[...], False)
        o_s = acc_scr[...] / jnp.maximum(l_scr[...], 1e-30)
        n_win = wk_ref.shape[1]
        k_wc, v_wc = wk_ref[...].astype(BF16), wv_ref[...].astype(BF16)
        pos_wc = past - n_win + lax.broadcasted_iota(jnp.int32, (1, n_win), 1)
        ok_wc = (pos_q - pos_wc >= 0) & (pos_q - pos_wc < WINDOW) & (pos_wc >= 0)
        ok_wn = (pos_q - new_key >= 0) & (pos_q - new_key < WINDOW) & (new_idx < nq)
        win = (jnp.full(m_scr.shape, -jnp.inf, F32), jnp.zeros(l_scr.shape, F32), jnp.zeros(acc_scr.shape, F32))
        win = _online_softmax_step(jnp.dot(qz, k_wc, preferred_element_type=F32), ok_wc, v_wc, *win, True)
        win = _online_softmax_step(lax.dot_general(qz, kwn_ref[...], NT, preferred_element_type=F32), ok_wn,
                                   vwn_ref[...], *win, False)
        o_w = win[2] / jnp.maximum(win[1], 1e-30)
        gexp = jnp.dot(g_ref[...], ex_ref[...], precision=HIGHEST, preferred_element_type=F32)
        o_ref[...] = gexp[:, :NSA_QD] * oc_ref[...]
        for g in range(NSA_KV_HEADS):
            m, half = divmod(g, 2)
            in_half = (lane128 // NSA_HD) == half
            mcols = slice(m * LANES, (m + 1) * LANES)
            for r in range(NSA_GROUP):
                t = m * NSA_GROUP + r
                cols = slice(t * LANES, (t + 1) * LANES)
                rows = slice((g * NSA_GROUP + r) * nq, (g * NSA_GROUP + r + 1) * nq)
                comb = (gexp[:, NSA_QD + t * LANES:NSA_QD + (t + 1) * LANES] * o_s[rows, mcols]
                        + gexp[:, 2 * NSA_QD + t * LANES:2 * NSA_QD + (t + 1) * LANES] * o_w[rows, mcols])
                o_ref[:, cols] += jnp.where(in_half, comb, 0.0)


def nsa_decode_attend(page_table, q, sel, gates, o_c, ks_new, vs_new, wk, wv, kw_new, vw_new, pool_k, pool_v,
                      db, nq, past):
    n_pages = page_table.shape[1]
    page = pool_k.shape[2]
    ex = jnp.asarray(_nsa_gate_expand())
    n_win = wk.shape[2]
    rows = NSA_KV_HEADS * NSA_GROUP * nq
    n_keys = PAGES_PER_STEP * page
    assert past % SEL_BLOCK + nq <= SEL_BLOCK and (LANES * SEL_BLOCK) % n_keys == 0
    steps_per_tile = LANES * SEL_BLOCK // n_keys
    xp = np.zeros((steps_per_tile, LANES, n_keys), np.float32)
    keys = np.arange(n_keys)
    for u in range(steps_per_tile):
        xp[u, u * (n_keys // SEL_BLOCK) + keys // SEL_BLOCK, keys] = 1.0
    xp = jnp.asarray(xp, BF16)
    per_b = lambda shape: pl.BlockSpec((None,) + shape, lambda bi, j, pt: (bi,) + (0,) * len(shape))
    q_rows = lambda width: pl.BlockSpec((nq, width), lambda bi, j, pt: (bi, 0))
    new_rows = ks_new.shape[1]
    return pl.pallas_call(
        functools.partial(_nsa_decode_attend_kernel, past),
        out_shape=jax.ShapeDtypeStruct((db * nq, NSA_QD), F32),
        grid_spec=pltpu.PrefetchScalarGridSpec(
            num_scalar_prefetch=1,
            grid=(db, n_pages // PAGES_PER_STEP),
            in_specs=[q_rows(NSA_QD), per_b(sel.shape[1:]), q_rows(LANES), q_rows(NSA_QD),
                      per_b((new_rows, NSA_KVD)), per_b((new_rows, NSA_KVD)),
                      per_b((NSA_KVD, n_win)), per_b((NSA_KVD, n_win)),
                      per_b((new_rows, NSA_KVD)), per_b((new_rows, NSA_KVD)),
                      pl.BlockSpec(ex.shape, lambda bi, j, pt: (0, 0)),
                      pl.BlockSpec(xp.shape, lambda bi, j, pt: (0, 0, 0))]
            + _page_specs(PAGES_PER_STEP, page) * 2,
            out_specs=q_rows(NSA_QD),
            scratch_shapes=[pltpu.VMEM((rows, NSA_KVD), BF16), pltpu.VMEM((rows, 1), F32),
                            pltpu.VMEM((rows, 1), F32), pltpu.VMEM((rows, NSA_KVD), F32)],
        ),
        compiler_params=_params("parallel", "arbitrary"),
        name="nsa_decode_attend",
    )(page_table, q, sel, gates, o_c, ks_new, vs_new, wk, wv, kw_new, vw_new, ex, xp,
      *([pool_k] * PAGES_PER_STEP), *([pool_v] * PAGES_PER_STEP))


def _nsa_weights(w_in, w_out):
    perm = _nsa_q_perm()
    return jnp.concatenate([w_in[:, perm], w_in[:, NSA_QD:]], axis=1), w_out[perm]


def nsa_prompt(x, scale, shift, w_in_p, pe_k, pe_v, b, s):
    p, _ = project(x, scale, shift, w_in_p, b, s)
    cos, up, dn = _rope_tables(jnp.arange(s, dtype=jnp.int32))
    q, kc, vc, ks, vs, kw, vw, ksb, vsb, kwb, vwb, gates = nsa_prep(p, cos, up, dn, q_dtype=BF16)
    kcmp = nsa_compress(kc, pe_k, b, s)
    vcmp = nsa_compress(vc, pe_v, b, s)
    o = nsa_attn_prompt(q, gates, kcmp, vcmp, ksb, vsb, kwb, vwb, b, s)
    nw = min(WINDOW, s)
    rs = lambda a: a.reshape(b, s, NSA_KV_HEADS, NSA_HD)
    return o, (rs(kc), rs(vc), rs(ks), rs(vs), rs(kw)[:, s - nw:], rs(vw)[:, s - nw:])


def nsa_sample(x, scale, shift, ck, cv, sk, sv, wk, wv, page_table, w_in_p, pe_k, pe_v, db, nq):
    past = page_table.shape[1] * ck.shape[1]
    p, _ = project(x, scale, shift, w_in_p, db, nq)
    pos = past + jnp.tile(jnp.arange(nq, dtype=jnp.int32), db)
    cos, up, dn = _rope_tables(pos)
    q, kc, vc, ks, vs, kw, vw, ksb, vsb, kwb, vwb, gates = nsa_prep(p, cos, up, dn, q_dtype=F32)
    sums = nsa_page_sums(page_table, _feature_major(ck), _feature_major(cv), pe_k, pe_v)
    sel, o_c = nsa_decode_select(q, sums, kc, vc, pe_k, pe_v, db, nq, past)
    new_pad = lambda a: jnp.pad(a.reshape(db, nq, NSA_KVD), ((0, 0), (0, LANES - nq), (0, 0)))
    o = nsa_decode_attend(page_table, q, sel, gates, o_c, new_pad(ksb), new_pad(vsb),
                          _feature_major(wk), _feature_major(wv), new_pad(kwb), new_pad(vwb),
                          _feature_major(sk), _feature_major(sv), db, nq, past)
    rs = lambda a: a.reshape(db, nq, NSA_KV_HEADS, NSA_HD)
    slide = lambda cache, new: jnp.concatenate([cache, rs(new).astype(cache.dtype)], 1)[:, nq:]
    return o, (rs(kc), rs(vc), rs(ks), rs(vs), slide(wk, kw), slide(wv, vw))


def _top_values(s, k):
    vals = []
    for _ in range(k):
        m = jnp.max(s, axis=0, keepdims=True)
        vals.append(m)
        s = jnp.where(s >= m, -jnp.inf, s)
    return vals


def _peer_route_kernel(qv_ref, keys_ref, s1_ref, s2_ref, e1_ref, e2_ref, th_ref):
    for h in range(PEER_HEADS):
        q1 = qv_ref[:, (2 * h) * PEER_DKEY:(2 * h + 1) * PEER_DKEY]
        q2 = qv_ref[:, (2 * h + 1) * PEER_DKEY:(2 * h + 2) * PEER_DKEY]
        s1 = lax.dot_general(keys_ref[h, 0], q1, NT, precision=HIGHEST, preferred_element_type=F32)
        s2 = lax.dot_general(keys_ref[h, 1], q2, NT, precision=HIGHEST, preferred_element_type=F32)
        top1 = _top_values(s1, PEER_TOPK)
        top2 = _top_values(s2, PEER_TOPK)
        t1 = jnp.concatenate(top1, axis=0)
        t2 = jnp.concatenate(top2, axis=0)
        row8 = lax.broadcasted_iota(jnp.int32, (SUBLANES, 1), 0)
        pairs = [top1[0] + t2, top1[1] + t2[:SUBLANES], t1[SUBLANES:] + top2[0]]
        for i in range(2, SUBLANES):
            pairs.append(jnp.where(row8 < PEER_TOPK // (i + 1), top1[i] + t2[:SUBLANES], -jnp.inf))
        best = _top_values(jnp.concatenate(pairs, axis=0), PEER_TOPK)
        z = best[0] * 0.0
        for v in best:
            z = z + jnp.exp(v - best[0])
        e1 = jnp.exp(s1 - top1[0])
        e2 = jnp.exp(s2 - top2[0]) / z
        for c in range(qv_ref.shape[0] // LANES):
            cols = slice(c * LANES, (c + 1) * LANES)
            s1_ref[h, c] = s1[:, cols]
            s2_ref[h, c] = s2[:, cols]
            e1_ref[h, c] = e1[:, cols]
            e2_ref[h, c] = e2[:, cols]
            th_ref[h, c] = best[PEER_TOPK - 1][:, cols]


def peer_route(qv, keys, *, tm):
    t = qv.shape[0]
    nc = tm // LANES
    big = jax.ShapeDtypeStruct((PEER_HEADS, t // LANES, PEER_NKEYS, LANES), F32)
    big_spec = pl.BlockSpec((PEER_HEADS, nc, PEER_NKEYS, LANES), lambda i: (0, i, 0, 0))
    return pl.pallas_call(
        _peer_route_kernel,
        out_shape=(big, big, big, big, jax.ShapeDtypeStruct((PEER_HEADS, t // LANES, 1, LANES), F32)),
        grid=(t // tm,),
        in_specs=[
            pl.BlockSpec((tm, 2 * PEER_HEADS * PEER_DKEY), lambda i: (i, 0)),
            pl.BlockSpec(keys.shape, lambda i: (0, 0, 0, 0)),
        ],
        out_specs=(big_spec, big_spec, big_spec, big_spec,
                   pl.BlockSpec((PEER_HEADS, nc, 1, LANES), lambda i: (0, i, 0, 0))),
        compiler_params=_params("parallel"),
        name="peer_route",
    )(qv, keys)


PEER_A_PER_STEP = SUBLANES
PEER_ROUTE_TILE = 256


def _gelu_tanh(x):
    return 0.5 * x * (1.0 + jnp.tanh(math.sqrt(2.0 / math.pi) * (x + 0.044715 * (x * x * x))))


PEER_K_CHUNK = 256


def _peer_dense_kernel(h_ref, s1_ref, e1_ref, s2_ref, e2_ref, th_ref, u_ref, vt_ref, o_ref,
                       act0_ref, act1_ref, g0_ref, g1_ref, acc_ref):
    j = pl.program_id(1)
    n_groups = pl.num_programs(1) - 2
    tm = h_ref.shape[2]
    nc = tm // LANES
    ne, d = u_ref.shape[1], vt_ref.shape[1]
    w_valid = (j >= 1) & (j <= n_groups)

    @pl.when(j == 0)
    def _():
        acc_ref[...] = jnp.zeros_like(acc_ref)
        g0_ref[...] = jnp.zeros_like(g0_ref)
        g1_ref[...] = jnp.zeros_like(g1_ref)

    def run(act_w, act_r, g_w, g_r):
        def step(k, carry):
            kk, half = k // 2, k % 2
            rows_e = pl.ds(pl.multiple_of(half * (ne // 2), ne // 2), ne // 2)
            part = jnp.dot(u_ref[kk, rows_e, :], h_ref[kk], preferred_element_type=F32)
            for c in range(nc):
                prev = jnp.where(kk > 0, act_w[c, rows_e, :], 0.0)
                act_w[c, rows_e, :] = part[:, c * LANES:(c + 1) * LANES] + prev
            rows_a = pl.ds(pl.multiple_of(k * PEER_NKEYS, PEER_NKEYS), PEER_NKEYS)
            for c in range(nc):
                w = jnp.zeros((PEER_NKEYS, LANES), F32)
                for h in range(PEER_HEADS):
                    cand = s2_ref[h, c] + s1_ref[h, c, pl.ds(k, 1), :]
                    w = w + e1_ref[h, c, pl.ds(k, 1), :] * jnp.where(cand >= th_ref[h, c], e2_ref[h, c], 0.0)
                val = w * _gelu_tanh(act_r[c, rows_a, :])
                g_w[c, rows_a, :] = jnp.where(w_valid, val, 0.0).astype(BF16)
            rows_g = pl.ds(pl.multiple_of(kk * PEER_K_CHUNK, PEER_K_CHUNK), PEER_K_CHUNK)
            g = jnp.concatenate([g_r[c, rows_g, :] for c in range(nc)], axis=1)
            rows_d = pl.ds(pl.multiple_of(half * (d // 2), d // 2), d // 2)
            acc_ref[rows_d, :] += jnp.dot(vt_ref[kk, rows_d, :], g, preferred_element_type=F32)
            return carry

        lax.fori_loop(0, PEER_A_PER_STEP, step, 0)

    @pl.when(j % 2 == 0)
    def _():
        run(act0_ref, act1_ref, g1_ref, g0_ref)

    @pl.when(j % 2 == 1)
    def _():
        run(act1_ref, act0_ref, g0_ref, g1_ref)

    @pl.when(j == pl.num_programs(1) - 1)
    def _():
        o_ref[...] = acc_ref[...].T


def peer_dense(h4, s1, s2, e1, e2, th, u4, vt4, *, tm):
    n_k, _, t = h4.shape
    d = vt4.shape[1]
    ne = PEER_A_PER_STEP * PEER_NKEYS
    assert ne == n_k * PEER_K_CHUNK and 2 * n_k == PEER_A_PER_STEP
    n_groups = u4.shape[1] // ne
    nc = tm // LANES
    clamp = lambda g: jnp.minimum(jnp.maximum(g, 0), n_groups - 1)
    tok = pl.BlockSpec((PEER_HEADS, nc, PEER_NKEYS, LANES), lambda i, j: (0, i, 0, 0))
    arow = pl.BlockSpec((PEER_HEADS, nc, PEER_A_PER_STEP, LANES), lambda i, j: (0, i, clamp(j - 1), 0))
    return pl.pallas_call(
        _peer_dense_kernel,
        out_shape=jax.ShapeDtypeStruct((t, d), F32),
        grid=(t // tm, n_groups + 2),
        in_specs=[
            pl.BlockSpec((n_k, PEER_K_CHUNK, tm), lambda i, j: (0, 0, i)),
            arow, arow, tok, tok,
            pl.BlockSpec((PEER_HEADS, nc, 1, LANES), lambda i, j: (0, i, 0, 0)),
            pl.BlockSpec((n_k, ne, PEER_K_CHUNK), lambda i, j: (0, clamp(j), 0)),
            pl.BlockSpec((n_k, d, PEER_K_CHUNK), lambda i, j: (clamp(j - 2), 0, 0)),
        ],
        out_specs=pl.BlockSpec((tm, d), lambda i, j: (i, 0)),
        scratch_shapes=[pltpu.VMEM((nc, ne, LANES), F32), pltpu.VMEM((nc, ne, LANES), F32),
                        pltpu.VMEM((nc, ne, LANES), BF16), pltpu.VMEM((nc, ne, LANES), BF16),
                        pltpu.VMEM((d, tm), F32)],
        compiler_params=_params("parallel", "arbitrary"),
        name="peer_dense",
    )(h4, s1, e1, s2, e2, th, u4, vt4)


def _peer_tables(u, v):
    e, d = u.shape
    u4 = u.astype(BF16).reshape(e, d // PEER_K_CHUNK, PEER_K_CHUNK).transpose(1, 0, 2)
    vt4 = v.astype(BF16).reshape(e // PEER_K_CHUNK, PEER_K_CHUNK, d).transpose(0, 2, 1)
    return u4, vt4


def peer_ffn(x, scale, shift, wq, keys, u4, vt4, b, s):
    qv, h = project(x, scale, shift, wq, b, s)
    t, d = h.shape
    tm = _row_tile(t)
    s1, s2, e1, e2, th = peer_route(qv, keys, tm=min(tm, PEER_ROUTE_TILE))
    h4 = h.reshape(t, d // PEER_K_CHUNK, PEER_K_CHUNK).transpose(1, 2, 0)
    return peer_dense(h4, s1, s2, e1, e2, th, u4, vt4, tm=tm)


def kernel(x_prompt, x_sample, state_gla, cache_cmp_k, cache_cmp_v, cache_sel_k, cache_sel_v, cache_win_k, cache_win_v, page_table, c_prompt, c_sample, ada_w, ada_b, ln_g, ln_b, gla_w_in, gla_w_a2, gla_b_a2, gla_gn, gla_w_out, nsa_w_in, nsa_pe_k, nsa_pe_v, nsa_w_out, peer_wq, peer_keys, peer_u, peer_v):
    bp, sp, d = x_prompt.shape
    bs, ss, _ = x_sample.shape
    groups = ((bp, sp), (bs, ss))
    c_all = jnp.concatenate([c_prompt, c_sample], axis=0)
    c_rows = _round_up(bp + bs, SUBLANES)
    mod = ada_mod(jnp.pad(c_all, ((0, c_rows - bp - bs), (0, 0))), ada_w, ada_b)
    ys = [x_prompt.reshape(bp * sp, d), x_sample.reshape(bs * ss, d)]
    gla_states, nsa_rows = ([], []), ([], [])
    for i in range(DEPTH):
        row0 = (0, bp)
        mods = [[mod[i, row0[n]:row0[n] + b, k * d:(k + 1) * d][:, None, :] for k in range(6)]
                for n, (b, _) in enumerate(groups)]
        j = i // N_MIXERS
        if i % N_MIXERS == 0:
            for n, (b, s) in enumerate(groups):
                shift, scale, gate = mods[n][:3]
                proj, _ = project(ys[n], scale, shift, gla_w_in[j], b, s)
                o, st = gla_recurrence(proj, gla_w_a2[j], gla_b_a2[j], None if n == 0 else state_gla[j], b, s)
                gla_states[n].append(st.astype(state_gla.dtype))
                ys[n] = post(ys[n], gate, ln_g[i, 0], ln_b[i, 0], b, s, sub=o, w_out=gla_w_out[j],
                             gla_proj=proj, gla_gn=gla_gn[j])
        else:
            w_in_p, w_out_p = _nsa_weights(nsa_w_in[j], nsa_w_out[j])
            for n, (b, s) in enumerate(groups):
                shift, scale, gate = mods[n][:3]
                if n == 0:
                    o, rows = nsa_prompt(ys[n], scale, shift, w_in_p, nsa_pe_k[j], nsa_pe_v[j], b, s)
                else:
                    o, rows = nsa_sample(ys[n], scale, shift, cache_cmp_k[j], cache_cmp_v[j], cache_sel_k[j],
                                         cache_sel_v[j], cache_win_k[j], cache_win_v[j], page_table,
                                         w_in_p, nsa_pe_k[j], nsa_pe_v[j], b, s)
                nsa_rows[n].append(rows)
                ys[n] = post(ys[n], gate, ln_g[i, 0], ln_b[i, 0], b, s, sub=o, w_out=w_out_p)
        u4, vt4 = _peer_tables(peer_u[i], peer_v[i])
        for n, (b, s) in enumerate(groups):
            shift, scale, gate = mods[n][3:]
            f = peer_ffn(ys[n], scale, shift, peer_wq[i], peer_keys[i], u4, vt4, b, s)
            ys[n] = post(ys[n], gate, ln_g[i, 1], ln_b[i, 1], b, s, sub=f)

    st = lambda ts, k: jnp.stack([t[k] for t in ts])
    return (ys[0].reshape(bp, sp, d), ys[1].reshape(bs, ss, d),
            jnp.stack(gla_states[0]), jnp.stack(gla_states[1]),
            *(st(nsa_rows[0], k) for k in range(6)), *(st(nsa_rows[1], k) for k in range(6)))
```

```python
import functools
import math

import jax
import jax.numpy as jnp
import numpy as np
from jax import lax
from jax.experimental import pallas as pl
from jax.experimental.pallas import tpu as pltpu

D_MODEL = 1024
DEPTH = 2
N_MIXERS = 2
DN_ALPHA = (2.0 * DEPTH) ** 0.25
LN_EPS = 1e-5
F32 = jnp.float32
BF16 = jnp.bfloat16
HIGHEST = lax.Precision.HIGHEST

GLA_HEADS = 4
GLA_DK = D_MODEL // 2 // GLA_HEADS
GLA_DV = D_MODEL // GLA_HEADS
GLA_TAU = 16.0
GLA_CHUNK = 64
GLA_HK = GLA_HEADS * GLA_DK
GLA_HV = GLA_HEADS * GLA_DV

NSA_HEADS = 16
NSA_KV_HEADS = 4
NSA_GROUP = NSA_HEADS // NSA_KV_HEADS
NSA_HD = D_MODEL // NSA_HEADS
NSA_QD = NSA_HEADS * NSA_HD
NSA_KVD = NSA_KV_HEADS * NSA_HD
CMP_LEN = 32
CMP_STRIDE = 16
SEL_BLOCK = 64
N_SEL = 16
WINDOW = 512
Q_BLOCK = 128
FORCE_SCORE = 1e6
ROT_DIM = NSA_HD // 4
ROPE_THETA = 500000.0

PEER_HEADS = 8
PEER_NKEYS = 128
PEER_DKEY = 128
PEER_TOPK = 16

LANES = 128
SUBLANES = 8
VMEM_LIMIT_BYTES = 56 * 1024 * 1024
ROW_TILE = 512

NT = (((1,), (1,)), ((), ()))
TN = (((0,), (0,)), ((), ()))


def _round_up(n, m):
    return -(-n // m) * m


def _params(*sem):
    return pltpu.CompilerParams(dimension_semantics=sem, vmem_limit_bytes=VMEM_LIMIT_BYTES)


def _row_tile(t):
    return ROW_TILE if t % ROW_TILE == 0 else t


def _mod_operand(m, b, s, tm):
    d = m.shape[-1]
    if s % tm == 0:
        return m, pl.BlockSpec((None, 1, d), lambda i, *_: (i * tm // s, 0, 0))
    rows = jnp.broadcast_to(m, (b, s, d)).reshape(b * s, d)
    return rows, pl.BlockSpec((tm, d), lambda i, *_: (i, 0))


def _ada_kernel(c_ref, w_ref, b_ref, o_ref):
    c = c_ref[...]
    act = (c * jax.nn.sigmoid(c)).astype(BF16)
    o_ref[...] = jnp.dot(act, w_ref[...].astype(BF16), preferred_element_type=F32) + b_ref[...]


def ada_mod(c, w, b):
    r, d = c.shape
    nl, _, n = w.shape
    tn = n // 4
    return pl.pallas_call(
        _ada_kernel,
        out_shape=jax.ShapeDtypeStruct((nl, r, n), F32),
        grid=(nl, n // tn),
        in_specs=[pl.BlockSpec((r, d), lambda l, j: (0, 0)),
                  pl.BlockSpec((None, d, tn), lambda l, j: (l, 0, j)),
                  pl.BlockSpec((None, 1, tn), lambda l, j: (l, 0, j))],
        out_specs=pl.BlockSpec((None, r, tn), lambda l, j: (l, 0, j)),
        compiler_params=_params("parallel", "parallel"),
        name="ada_mod",
    )(c, w, b.reshape(nl, 1, n))


def _mod_matmul_kernel(x_ref, sc_ref, sh_ref, w_ref, o_ref, h_ref):
    @pl.when(pl.program_id(1) == 0)
    def _():
        h = x_ref[...] * (1.0 + sc_ref[...]) + sh_ref[...]
        h_ref[...] = h.astype(BF16)

    o_ref[...] = jnp.dot(h_ref[...], w_ref[...], preferred_element_type=F32)


def project(x, scale, shift, w, b, s):
    t, d = x.shape
    npad = _round_up(w.shape[1], LANES)
    wp = jnp.pad(w, ((0, 0), (0, npad - w.shape[1]))).astype(BF16)
    tn = npad
    tm = _row_tile(t)
    sc, mod_spec = _mod_operand(scale, b, s, tm)
    sh, _ = _mod_operand(shift, b, s, tm)
    return pl.pallas_call(
        _mod_matmul_kernel,
        out_shape=(jax.ShapeDtypeStruct((t, npad), F32), jax.ShapeDtypeStruct((t, d), BF16)),
        grid=(t // tm, npad // tn),
        in_specs=[pl.BlockSpec((tm, d), lambda i, j: (i, 0)), mod_spec, mod_spec,
                  pl.BlockSpec((d, tn), lambda i, j: (0, j))],
        out_specs=(pl.BlockSpec((tm, tn), lambda i, j: (i, j)), pl.BlockSpec((tm, d), lambda i, j: (i, 0))),
        compiler_params=_params("parallel", "arbitrary"),
        name="mod_matmul",
    )(x, sc, sh, wp)


def _deepnorm(x, sub, gate, g, b):
    y = DN_ALPHA * x + (1.0 + gate) * sub
    mu = jnp.mean(y, axis=-1, keepdims=True)
    var = jnp.mean(jnp.square(y - mu), axis=-1, keepdims=True)
    return (y - mu) * lax.rsqrt(var + LN_EPS) * g + b


def _post_gla_kernel(o_ref, r_ref, x_ref, gate_ref, gn_ref, w_ref, g_ref, b_ref, y_ref):
    parts = []
    for h in range(GLA_HEADS):
        cols = slice(h * GLA_DV, (h + 1) * GLA_DV)
        o = o_ref[:, cols]
        mu = jnp.mean(o, axis=-1, keepdims=True)
        var = jnp.mean(jnp.square(o - mu), axis=-1, keepdims=True)
        parts.append((o - mu) * lax.rsqrt(var + LN_EPS) * gn_ref[:, cols])
    r = r_ref[...]
    f = (jnp.concatenate(parts, axis=1) * (r * jax.nn.sigmoid(r))).astype(BF16)
    sub = jnp.dot(f, w_ref[...], preferred_element_type=F32)
    y_ref[...] = _deepnorm(x_ref[...], sub, gate_ref[...], g_ref[...], b_ref[...])


def _post_matmul_kernel(o_ref, x_ref, gate_ref, w_ref, g_ref, b_ref, y_ref):
    sub = jnp.dot(o_ref[...].astype(BF16), w_ref[...], preferred_element_type=F32)
    y_ref[...] = _deepnorm(x_ref[...], sub, gate_ref[...], g_ref[...], b_ref[...])


def _post_plain_kernel(o_ref, x_ref, gate_ref, g_ref, b_ref, y_ref):
    y_ref[...] = _deepnorm(x_ref[...], o_ref[...], gate_ref[...], g_ref[...], b_ref[...])


def post(x, gate, ln_g, ln_b, b, s, *, sub=None, w_out=None, gla_proj=None, gla_gn=None):
    t, d = x.shape
    tm = _row_tile(t)
    gate_arr, gate_spec = _mod_operand(gate, b, s, tm)
    row = pl.BlockSpec((tm, d), lambda i: (i, 0))
    vec = pl.BlockSpec((1, d), lambda i: (0, 0))
    mat = pl.BlockSpec((d, d), lambda i: (0, 0))
    g2, b2 = ln_g.reshape(1, d), ln_b.reshape(1, d)
    if gla_proj is not None:
        r_spec = pl.BlockSpec((tm, GLA_HV), lambda i: (i, (2 * GLA_HK + GLA_HV) // GLA_HV))
        args = (sub, gla_proj, x, gate_arr, gla_gn.reshape(1, d), w_out.astype(BF16), g2, b2)
        specs = [row, r_spec, row, gate_spec, vec, mat, vec, vec]
        body = _post_gla_kernel
    elif w_out is not None:
        args = (sub, x, gate_arr, w_out.astype(BF16), g2, b2)
        specs = [row, row, gate_spec, mat, vec, vec]
        body = _post_matmul_kernel
    else:
        args = (sub, x, gate_arr, g2, b2)
        specs = [row, row, gate_spec, vec, vec]
        body = _post_plain_kernel
    return pl.pallas_call(
        body,
        out_shape=jax.ShapeDtypeStruct((t, d), F32),
        grid=(t // tm,),
        in_specs=specs,
        out_specs=row,
        compiler_params=_params("parallel"),
        name="post",
    )(*args)


GLA_MIN_ROWS = 64


def _cumsum_rows(x):
    n = x.shape[0]
    row = lax.broadcasted_iota(jnp.int32, (n, 1), 0)
    shift = 1
    while shift < n:
        x = x + jnp.where(row >= shift, pltpu.roll(x, shift, 0), 0.0)
        shift *= 2
    return x


def _gla_kernel(has_s0, q_ref, k_ref, v_ref, a_ref, wa_ref, ba_ref, *rest):
    if has_s0:
        s0_ref, o_ref, st_ref, s_scr = rest
    else:
        o_ref, st_ref, s_scr = rest
    c = pl.program_id(1)
    n_rows = q_ref.shape[0]
    rows = max(n_rows, GLA_MIN_ROWS)

    @pl.when(c == 0)
    def _():
        s_scr[...] = s0_ref[...] if has_s0 else jnp.zeros_like(s_scr)

    def padded(x):
        if rows == n_rows:
            return x
        return jnp.concatenate([x, jnp.zeros((rows - n_rows, x.shape[1]), x.dtype)], axis=0)

    z = jnp.dot(a_ref[...], wa_ref[...], precision=HIGHEST, preferred_element_type=F32) + ba_ref[...]
    logg = (jnp.minimum(z, 0.0) - jnp.log1p(jnp.exp(-jnp.abs(z)))) / GLA_TAU
    b = _cumsum_rows(padded(logg))
    q, k, v = padded(q_ref[...]), padded(k_ref[...]), padded(v_ref[...])
    qe = q * (GLA_DK ** -0.5) * jnp.exp(b)
    ke = k * jnp.exp(-b)
    b_last = b[rows - 1:rows, :]
    kd = k * jnp.exp(b_last - b)
    e_last = jnp.exp(b_last)
    causal = lax.broadcasted_iota(jnp.int32, (rows, 1), 0) >= lax.broadcasted_iota(jnp.int32, (1, rows), 1)
    eye = lax.broadcasted_iota(jnp.int32, (GLA_DK, 1), 0) == lax.broadcasted_iota(jnp.int32, (1, GLA_DK), 1)
    for h in range(GLA_HEADS):
        ck = slice(h * GLA_DK, (h + 1) * GLA_DK)
        cv = slice(h * GLA_DV, (h + 1) * GLA_DV)
        att = lax.dot_general(qe[:, ck], ke[:, ck], NT, preferred_element_type=F32)
        att = jnp.where(causal, att, 0.0)
        s_h = s_scr[h]
        o = (jnp.dot(att, v[:, cv], preferred_element_type=F32)
             + jnp.dot(qe[:, ck], s_h, preferred_element_type=F32))
        o_ref[:, cv] = o[:n_rows]
        e_col = jnp.sum(jnp.where(eye, e_last[:, ck], 0.0), axis=1, keepdims=True)
        s_scr[h] = e_col * s_h + lax.dot_general(kd[:, ck], v[:, cv], TN, preferred_element_type=F32)

    @pl.when(c == pl.num_programs(1) - 1)
    def _():
        st_ref[...] = s_scr[...]


def gla_recurrence(proj, w_a2, b_a2, s0, b, s):
    chunk = GLA_CHUNK if s % GLA_CHUNK == 0 else s
    nc = s // chunk
    wa = jnp.pad(w_a2, ((0, LANES - w_a2.shape[0]), (0, 0)))
    a_block = (2 * GLA_HK + 2 * GLA_HV) // LANES
    row = lambda width, blk: pl.BlockSpec((chunk, width), lambda bi, c: (bi * nc + c, blk))
    st_spec = pl.BlockSpec((None, GLA_HEADS, GLA_DK, GLA_DV), lambda bi, c: (bi, 0, 0, 0))
    in_specs = [row(GLA_HK, 0), row(GLA_HK, 1), row(GLA_HV, 2 * GLA_HK // GLA_HV), row(LANES, a_block),
                pl.BlockSpec(wa.shape, lambda bi, c: (0, 0)), pl.BlockSpec((1, GLA_HK), lambda bi, c: (0, 0))]
    args = [proj, proj, proj, proj, wa, b_a2.reshape(1, GLA_HK)]
    if s0 is not None:
        in_specs.append(st_spec)
        args.append(s0)
    return pl.pallas_call(
        functools.partial(_gla_kernel, s0 is not None),
        out_shape=(jax.ShapeDtypeStruct((b * s, GLA_HV), F32),
                   jax.ShapeDtypeStruct((b, GLA_HEADS, GLA_DK, GLA_DV), F32)),
        grid=(b, nc),
        in_specs=in_specs,
        out_specs=(pl.BlockSpec((chunk, GLA_HV), lambda bi, c: (bi * nc + c, 0)), st_spec),
        scratch_shapes=[pltpu.VMEM((GLA_HEADS, GLA_DK, GLA_DV), F32)],
        compiler_params=_params("parallel", "arbitrary"),
        name="gla_recurrence",
    )(*args)


def _nsa_q_perm():
    cols = []
    for m in range(NSA_KV_HEADS // 2):
        for r in range(NSA_GROUP):
            for half in range(2):
                head = (2 * m + half) * NSA_GROUP + r
                cols.append(np.arange(NSA_HD) + head * NSA_HD)
    return np.concatenate(cols)


def _nsa_gate_expand():
    perm = _nsa_q_perm()
    ex = np.zeros((LANES, 3 * NSA_QD), np.float32)
    for br in range(3):
        for col in range(NSA_QD):
            ex[br * NSA_HEADS + perm[col] // NSA_HD, br * NSA_QD + col] = 1.0
    return ex


def _nsa_slc_matrix(nc_pad, nsb_pad):
    ratio = SEL_BLOCK // CMP_STRIDE
    m = np.zeros((nc_pad, nsb_pad), np.float32)
    for j in range(nsb_pad):
        for o in range(CMP_LEN // CMP_STRIDE):
            for i in range(ratio):
                n = ratio * j + i + o - (CMP_LEN // CMP_STRIDE - 1)
                if 0 <= n < nc_pad:
                    m[n, j] += 1.0
    return m


def _rope_tables(pos):
    half = ROT_DIM // 2
    inv = ROPE_THETA ** (-jnp.arange(half, dtype=F32) * 2.0 / ROT_DIM)
    ang = pos.astype(F32)[:, None] * inv[None, :]
    cos, sin = jnp.cos(ang), jnp.sin(ang)
    t = pos.shape[0]
    ones = jnp.ones((t, NSA_HD - ROT_DIM), F32)
    zeros = jnp.zeros((t, NSA_HD - ROT_DIM), F32)
    z8 = jnp.zeros((t, half), F32)
    c = jnp.concatenate([cos, cos, ones], 1)
    up = jnp.concatenate([-sin, z8, zeros], 1)
    dn = jnp.concatenate([z8, sin, zeros], 1)
    two = lambda a: jnp.concatenate([a, a], 1)
    return two(c), two(up), two(dn)


def _nsa_prep_kernel(p_ref, cos_ref, up_ref, dn_ref, q_ref, kc_ref, vc_ref, ks_ref, vs_ref, kw_ref, vw_ref,
                     ksb_ref, vsb_ref, kwb_ref, vwb_ref, g_ref):
    def rope(x):
        reps = x.shape[1] // LANES
        tile = lambda a: jnp.concatenate([a] * reps, axis=1)
        w = x.shape[1]
        return (x * tile(cos_ref[...]) + pltpu.roll(x, w - ROT_DIM // 2, 1) * tile(up_ref[...])
                + pltpu.roll(x, ROT_DIM // 2, 1) * tile(dn_ref[...]))

    q_ref[...] = (rope(p_ref[:, :NSA_QD]) * (NSA_HD ** -0.5)).astype(q_ref.dtype)
    kv = lambda k: p_ref[:, NSA_QD + k * NSA_KVD:NSA_QD + (k + 1) * NSA_KVD]
    kc_ref[...] = rope(kv(0))
    vc_ref[...] = kv(1)
    ks = rope(kv(2))
    ks_ref[...] = ks
    ksb_ref[...] = ks.astype(BF16)
    vs_ref[...] = kv(3)
    vsb_ref[...] = kv(3).astype(BF16)
    kw = rope(kv(4))
    kw_ref[...] = kw
    kwb_ref[...] = kw.astype(BF16)
    vw_ref[...] = kv(5)
    vwb_ref[...] = kv(5).astype(BF16)
    g0 = NSA_QD + 6 * NSA_KVD
    g_ref[...] = jax.nn.sigmoid(p_ref[:, g0:g0 + LANES])


def nsa_prep(p, cos, up, dn, *, q_dtype):
    t = p.shape[0]
    tm = _row_tile(t)
    period = cos.shape[0] // tm
    tab = pl.BlockSpec((tm, LANES), lambda i: (i % period, 0))
    kv32 = jax.ShapeDtypeStruct((t, NSA_KVD), F32)
    kv16 = jax.ShapeDtypeStruct((t, NSA_KVD), BF16)
    kvs = pl.BlockSpec((tm, NSA_KVD), lambda i: (i, 0))
    return pl.pallas_call(
        _nsa_prep_kernel,
        out_shape=(jax.ShapeDtypeStruct((t, NSA_QD), q_dtype),) + (kv32,) * 6 + (kv16,) * 4
        + (jax.ShapeDtypeStruct((t, LANES), F32),),
        grid=(t // tm,),
        in_specs=[pl.BlockSpec((tm, p.shape[1]), lambda i: (i, 0)), tab, tab, tab],
        out_specs=(pl.BlockSpec((tm, NSA_QD), lambda i: (i, 0)),) + (kvs,) * 10
        + (pl.BlockSpec((tm, LANES), lambda i: (i, 0)),),
        compiler_params=_params("parallel"),
        name="nsa_prep",
    )(p, cos, up, dn)


def _block_sums(x, w):
    n_sub = x.shape[0] // CMP_STRIDE
    x = x.reshape(n_sub, CMP_STRIDE, NSA_KVD)
    w = 1.0 + w
    return jnp.sum(x * w[None, :CMP_STRIDE], axis=1), jnp.sum(x * w[None, CMP_STRIDE:], axis=1)


def _nsa_compress_kernel(x_ref, w_ref, o_ref):
    first, second = _block_sums(x_ref[...], w_ref[...])
    n_sub = first.shape[0]
    nxt = pltpu.roll(second, n_sub - 1, 0)
    row = lax.broadcasted_iota(jnp.int32, (n_sub, 1), 0)
    o_ref[...] = jnp.where(row < n_sub - 1, (first + nxt) / CMP_LEN, 0.0).astype(o_ref.dtype)


def nsa_compress(x, pe, b, s):
    n_sub = s // CMP_STRIDE
    return pl.pallas_call(
        _nsa_compress_kernel,
        out_shape=jax.ShapeDtypeStruct((b, n_sub, NSA_KVD), BF16),
        grid=(b,),
        in_specs=[pl.BlockSpec((s, NSA_KVD), lambda i: (i, 0)),
                  pl.BlockSpec((CMP_LEN, NSA_KVD), lambda i: (0, 0))],
        out_specs=pl.BlockSpec((None, n_sub, NSA_KVD), lambda i: (i, 0, 0)),
        compiler_params=_params("parallel"),
        name="nsa_compress",
    )(x, pe.reshape(CMP_LEN, NSA_KVD))


NSA_KV_TILE = 512


def _mask_rows(s, allowed):
    if allowed.shape[0] == s.shape[0]:
        return jnp.where(allowed, s, -jnp.inf)
    reps = s.shape[0] // allowed.shape[0]
    s3 = s.reshape(reps, allowed.shape[0], s.shape[1])
    return jnp.where(allowed[None], s3, -jnp.inf).reshape(s.shape)


def _softmax_rows(s, allowed):
    s = _mask_rows(s, allowed)
    m = jnp.max(s, axis=1, keepdims=True)
    m = jnp.where(m == -jnp.inf, 0.0, m)
    e = jnp.exp(s - m)
    return e / jnp.maximum(jnp.sum(e, axis=1, keepdims=True), 1e-30)


def _select_blocks(score, n_sel):
    nb = score.shape[1]
    lane = lax.broadcasted_iota(jnp.int32, score.shape, 1).astype(F32)
    sel = jnp.zeros(score.shape, F32)
    for _ in range(n_sel):
        m = jnp.max(score, axis=1, keepdims=True)
        first = jnp.min(jnp.where(score == m, lane, float(nb)), axis=1, keepdims=True)
        pick = lane == first
        sel = jnp.where(pick, 1.0, sel)
        score = jnp.where(pick, -jnp.inf, score)
    return sel


def _block_scores(p_slc, pos_q):
    jb = lax.broadcasted_iota(jnp.int32, (1, p_slc.shape[1]), 1)
    cur = pos_q // SEL_BLOCK
    valid = jb * SEL_BLOCK <= pos_q
    forced = (jb == 0) | (jb == cur) | (jb == cur - 1)
    return jnp.where(forced, FORCE_SCORE, jnp.where(valid, p_slc, -1.0))


def _select_blocks_t(score, n_sel):
    st = score.T
    nb = st.shape[0]
    blk = lax.broadcasted_iota(jnp.int32, st.shape, 0).astype(F32)
    sel = jnp.zeros(st.shape, F32)
    for _ in range(n_sel):
        m = jnp.max(st, axis=0, keepdims=True)
        first = jnp.min(jnp.where(st == m, blk, float(nb)), axis=0, keepdims=True)
        pick = blk == first
        sel = jnp.where(pick, 1.0, sel)
        st = jnp.where(pick, -jnp.inf, st)
    return sel.T


def _online_softmax_step(s, allowed, v, m_old, l_old, acc_old, feature_major=False):
    s = _mask_rows(s, allowed)
    m_new = jnp.maximum(m_old, jnp.max(s, axis=1, keepdims=True))
    m_use = jnp.where(m_new == -jnp.inf, 0.0, m_new)
    alpha = jnp.exp(m_old - m_use)
    p = jnp.exp(s - m_use)
    l_new = alpha * l_old + jnp.sum(p, axis=1, keepdims=True)
    if feature_major:
        pv = lax.dot_general(p.astype(BF16), v, NT, preferred_element_type=F32)
    else:
        pv = jnp.dot(p.astype(BF16), v, preferred_element_type=F32)
    return m_new, l_new, alpha * acc_old + pv


def _padded_query(tile, m, in_half):
    sel = jnp.where(in_half, tile, jnp.zeros_like(tile))
    z = jnp.zeros_like(sel)
    return jnp.concatenate([sel, z] if m == 0 else [z, sel], axis=1)


def _nsa_attn_kernel(q_ref, g_ref, kcmp_ref, vcmp_ref, ks_ref, vs_ref, kw_ref, vw_ref, slc_ref, ex_ref,
                     o_ref, qz_ref, sel_ref, oc_ref, m_ref, l_ref, acc_ref):
    i = pl.program_id(1)
    start = i * Q_BLOCK
    pos_q = start + lax.broadcasted_iota(jnp.int32, (Q_BLOCK, 1), 0)
    lane128 = lax.broadcasted_iota(jnp.int32, (1, LANES), 1)
    n_cmp = kcmp_ref.shape[1]
    cmp_end = CMP_STRIDE * lax.broadcasted_iota(jnp.int32, (1, n_cmp), 1) + (CMP_LEN - 1)
    cmp_ok = cmp_end <= pos_q
    gexp = jnp.dot(g_ref[...], ex_ref[...], precision=HIGHEST, preferred_element_type=F32)
    o_ref[...] = jnp.zeros_like(o_ref)

    win_base = pl.multiple_of(jnp.maximum(start - WINDOW, 0), Q_BLOCK)
    band = WINDOW + Q_BLOCK
    kw_t = jnp.concatenate([kw_ref[win_base // Q_BLOCK + u] for u in range(band // Q_BLOCK)], axis=1)
    vw_t = vw_ref[pl.ds(win_base, band), :]
    dpos = pos_q - (win_base + lax.broadcasted_iota(jnp.int32, (1, band), 1))
    win_ok = (dpos >= 0) & (dpos < WINDOW)

    for g in range(NSA_KV_HEADS):
        m, half = divmod(g, 2)
        in_half = (lane128 // NSA_HD) == half
        qz = jnp.concatenate(
            [_padded_query(q_ref[:, (m * NSA_GROUP + r) * LANES:(m * NSA_GROUP + r + 1) * LANES], m, in_half)
             for r in range(NSA_GROUP)], axis=0)
        qz_ref[g] = qz
        s = jnp.dot(qz, kcmp_ref[...], preferred_element_type=F32)
        p = _softmax_rows(s, cmp_ok)
        oc_ref[g] = jnp.dot(p.astype(BF16), vcmp_ref[...], preferred_element_type=F32)[:, m * LANES:(m + 1) * LANES]
        imp = p[0:Q_BLOCK]
        for r in range(1, NSA_GROUP):
            imp = imp + p[r * Q_BLOCK:(r + 1) * Q_BLOCK]
        p_slc = jnp.dot(imp, slc_ref[...], precision=HIGHEST, preferred_element_type=F32)
        sel_ref[g] = _select_blocks_t(_block_scores(p_slc, pos_q), N_SEL).astype(BF16)

    m_ref[...] = jnp.full_like(m_ref, -jnp.inf)
    l_ref[...] = jnp.zeros_like(l_ref)
    acc_ref[...] = jnp.zeros_like(acc_ref)

    def kv_step(t, carry):
        k0 = pl.multiple_of(t * NSA_KV_TILE, NSA_KV_TILE)
        k_t = ks_ref[t]
        v_t = vs_ref[pl.ds(k0, NSA_KV_TILE), :]
        key = k0 + lax.broadcasted_iota(jnp.int32, (1, NSA_KV_TILE), 1)
        blk = lax.broadcasted_iota(jnp.int32, (LANES, 1), 0)
        expand = jnp.where(blk == key // SEL_BLOCK, 1.0, 0.0).astype(BF16)
        causal = key <= pos_q
        for g in range(NSA_KV_HEADS):
            chosen = jnp.dot(sel_ref[g], expand, preferred_element_type=F32)
            s = jnp.dot(qz_ref[g], k_t, preferred_element_type=F32)
            m_ref[g], l_ref[g], acc_ref[g] = _online_softmax_step(s, (chosen > 0.5) & causal, v_t,
                                                                  m_ref[g], l_ref[g], acc_ref[g])
        return carry

    lax.fori_loop(0, (start + Q_BLOCK + NSA_KV_TILE - 1) // NSA_KV_TILE, kv_step, 0)

    for g in range(NSA_KV_HEADS):
        m, half = divmod(g, 2)
        in_half = (lane128 // NSA_HD) == half
        mcols = slice(m * LANES, (m + 1) * LANES)
        o_s = acc_ref[g][:, mcols] / jnp.maximum(l_ref[g], 1e-30)
        s = jnp.dot(qz_ref[g], kw_t, preferred_element_type=F32)
        o_w = jnp.dot(_softmax_rows(s, win_ok).astype(BF16), vw_t, preferred_element_type=F32)[:, mcols]
        o_c = oc_ref[g]
        for r in range(NSA_GROUP):
            t = m * NSA_GROUP + r
            cols = slice(t * LANES, (t + 1) * LANES)
            rows = slice(r * Q_BLOCK, (r + 1) * Q_BLOCK)
            comb = (gexp[:, cols] * o_c[rows] + gexp[:, NSA_QD + t * LANES:NSA_QD + (t + 1) * LANES] * o_s[rows]
                    + gexp[:, 2 * NSA_QD + t * LANES:2 * NSA_QD + (t + 1) * LANES] * o_w[rows])
            o_ref[:, cols] += jnp.where(in_half, comb, 0.0)


def nsa_attn_prompt(q, gates, kcmp, vcmp, ks, vs, kw, vw, b, s):
    nq = s // Q_BLOCK
    rows = NSA_GROUP * Q_BLOCK
    slc = jnp.asarray(_nsa_slc_matrix(s // CMP_STRIDE, LANES))
    ex = jnp.asarray(_nsa_gate_expand())
    seq = pl.BlockSpec((s, NSA_KVD), lambda bi, i: (bi, 0))
    n_cmp = s // CMP_STRIDE
    cmp_spec = pl.BlockSpec((None, n_cmp, NSA_KVD), lambda bi, i: (bi, 0, 0))
    cmp_t_spec = pl.BlockSpec((None, NSA_KVD, n_cmp), lambda bi, i: (bi, 0, 0))

    def key_tiles(k, tile):
        kt = k.reshape(b, s // tile, tile, NSA_KVD).transpose(0, 1, 3, 2)
        return kt, pl.BlockSpec((None, s // tile, NSA_KVD, tile), lambda bi, i: (bi, 0, 0, 0))

    ks, ks_spec = key_tiles(ks, NSA_KV_TILE)
    kw, kw_spec = key_tiles(kw, Q_BLOCK)
    kcmp = kcmp.transpose(0, 2, 1)
    return pl.pallas_call(
        _nsa_attn_kernel,
        out_shape=jax.ShapeDtypeStruct((b * s, NSA_QD), F32),
        grid=(b, nq),
        in_specs=[
            pl.BlockSpec((Q_BLOCK, NSA_QD), lambda bi, i: (bi * nq + i, 0)),
            pl.BlockSpec((Q_BLOCK, LANES), lambda bi, i: (bi * nq + i, 0)),
            cmp_t_spec, cmp_spec, ks_spec, seq, kw_spec, seq,
            pl.BlockSpec(slc.shape, lambda bi, i: (0, 0)),
            pl.BlockSpec(ex.shape, lambda bi, i: (0, 0)),
        ],
        out_specs=pl.BlockSpec((Q_BLOCK, NSA_QD), lambda bi, i: (bi * nq + i, 0)),
        scratch_shapes=[pltpu.VMEM((NSA_KV_HEADS, rows, NSA_KVD), BF16),
                        pltpu.VMEM((NSA_KV_HEADS, Q_BLOCK, LANES), BF16),
                        pltpu.VMEM((NSA_KV_HEADS, rows, LANES), F32),
                        pltpu.VMEM((NSA_KV_HEADS, rows, 1), F32),
                        pltpu.VMEM((NSA_KV_HEADS, rows, 1), F32),
                        pltpu.VMEM((NSA_KV_HEADS, rows, NSA_KVD), F32)],
        compiler_params=_params("parallel", "arbitrary"),
        name="nsa_attn_prompt",
    )(q, gates, kcmp, vcmp, ks, vs, kw, vw, slc, ex)


PAGES_PER_STEP = 8
SUM_PAGES_PER_STEP = 16


def _feature_major(cache):
    n, tokens = cache.shape[:2]
    return cache.transpose(0, 2, 3, 1).reshape(n, NSA_KVD, tokens)


def _page_specs(n, page):
    def one(u):
        return pl.BlockSpec((None, NSA_KVD, page), lambda bi, j, pt: (pt[bi, j * n + u], 0, 0))
    return [one(u) for u in range(n)]


def _nsa_page_sums_kernel(pt_ref, *refs):
    n = SUM_PAGES_PER_STEP
    kp, vp = refs[:n], refs[n:2 * n]
    wk1_ref, wk2_ref, wv1_ref, wv2_ref, grp_ref, fk_ref, sk_ref, fv_ref, sv_ref = refs[2 * n:]

    def sums(pages, w_ref, o_ref):
        xw = jnp.concatenate([(p[...] * w_ref[...]).astype(BF16) for p in pages], axis=1)
        o_ref[...] = jnp.dot(xw, grp_ref[...], preferred_element_type=F32)

    sums(kp, wk1_ref, fk_ref)
    sums(kp, wk2_ref, sk_ref)
    sums(vp, wv1_ref, fv_ref)
    sums(vp, wv2_ref, sv_ref)


def nsa_page_sums(page_table, pool_k, pool_v, pe_k, pe_v):
    db, n_pages = page_table.shape
    page = pool_k.shape[2]
    n = SUM_PAGES_PER_STEP
    per_page = page // CMP_STRIDE
    assert n * per_page == LANES
    out = jax.ShapeDtypeStruct((db, NSA_KVD, n_pages * per_page), F32)
    out_spec = pl.BlockSpec((None, NSA_KVD, LANES), lambda bi, j, pt: (bi, 0, j))
    w_spec = pl.BlockSpec((NSA_KVD, page), lambda bi, j, pt: (0, 0))
    halves = lambda pe: [jnp.tile((1.0 + pe.reshape(CMP_LEN, NSA_KVD)[o:o + CMP_STRIDE]).T, (1, per_page))
                         for o in (0, CMP_STRIDE)]
    grp = np.zeros((n * page, LANES), np.float32)
    tok = np.arange(n * page)
    grp[tok, tok // CMP_STRIDE] = 1.0
    grp = jnp.asarray(grp, BF16)
    return pl.pallas_call(
        _nsa_page_sums_kernel,
        out_shape=(out,) * 4,
        grid_spec=pltpu.PrefetchScalarGridSpec(
            num_scalar_prefetch=1,
            grid=(db, n_pages // n),
            in_specs=_page_specs(n, page) * 2 + [w_spec] * 4 + [pl.BlockSpec(grp.shape, lambda bi, j, pt: (0, 0))],
            out_specs=(out_spec,) * 4,
        ),
        compiler_params=_params("parallel", "arbitrary"),
        name="nsa_page_sums",
    )(page_table, *([pool_k] * n), *([pool_v] * n), *halves(pe_k), *halves(pe_v), grp)


def _decode_queries(q_ref, g):
    m, half = divmod(g, 2)
    in_half = (lax.broadcasted_iota(jnp.int32, (1, LANES), 1) // NSA_HD) == half
    tiles = [_padded_query(q_ref[:, (m * NSA_GROUP + r) * LANES:(m * NSA_GROUP + r + 1) * LANES], m, in_half)
             for r in range(NSA_GROUP)]
    return jnp.concatenate(tiles, axis=0).astype(BF16)


def _group_rows(x):
    return jnp.concatenate([x] * NSA_GROUP, axis=0)


def _nsa_decode_select_kernel(past, q_ref, fk_ref, sk_ref, fv_ref, sv_ref, kn_ref, vn_ref, wk_ref, wv_ref,
                              slc_ref, sel_ref, oc_ref):
    nq = q_ref.shape[0]
    n_cmp = fk_ref.shape[1]
    col = lax.broadcasted_iota(jnp.int32, (1, n_cmp), 1)
    eye = (lax.broadcasted_iota(jnp.int32, (NSA_KVD, 1), 0)
           == lax.broadcasted_iota(jnp.int32, (1, NSA_KVD), 1))

    def summaries(f_ref, s_ref, new_ref, w_ref):
        w = 1.0 + w_ref[CMP_STRIDE:CMP_STRIDE + nq, :]
        second_new = jnp.sum(new_ref[...] * w, axis=0, keepdims=True)
        new_col = jnp.sum(jnp.where(eye, second_new, 0.0), axis=1, keepdims=True)
        nxt = jnp.where(col == n_cmp - 1, new_col, pltpu.roll(s_ref[...], n_cmp - 1, 1))
        return ((f_ref[...] + nxt) / CMP_LEN).astype(BF16)

    kcmp_t = summaries(fk_ref, sk_ref, kn_ref, wk_ref)
    vcmp_t = summaries(fv_ref, sv_ref, vn_ref, wv_ref)
    pos_q = past + lax.broadcasted_iota(jnp.int32, (nq, 1), 0)
    cmp_end = CMP_STRIDE * lax.broadcasted_iota(jnp.int32, (1, n_cmp), 1) + (CMP_LEN - 1)
    cmp_ok = _group_rows(cmp_end <= pos_q)
    lane128 = lax.broadcasted_iota(jnp.int32, (1, LANES), 1)
    oc_ref[...] = jnp.zeros_like(oc_ref)
    for g in range(NSA_KV_HEADS):
        m, half = divmod(g, 2)
        in_half = (lane128 // NSA_HD) == half
        s = jnp.dot(_decode_queries(q_ref, g), kcmp_t, preferred_element_type=F32)
        p = _softmax_rows(s, cmp_ok)
        o_c = lax.dot_general(p.astype(BF16), vcmp_t, NT,
                              preferred_element_type=F32)[:, m * LANES:(m + 1) * LANES]
        imp = p[0:nq]
        for r in range(1, NSA_GROUP):
            imp = imp + p[r * nq:(r + 1) * nq]
        for r in range(NSA_GROUP):
            cols = slice((m * NSA_GROUP + r) * LANES, (m * NSA_GROUP + r + 1) * LANES)
            oc_ref[:, cols] += jnp.where(in_half, o_c[r * nq:(r + 1) * nq], 0.0)
        p_slc = jnp.dot(imp, slc_ref[...], precision=HIGHEST, preferred_element_type=F32)
        sel = _group_rows(_select_blocks(_block_scores(p_slc, pos_q), N_SEL)).astype(BF16)
        rows = slice(g * NSA_GROUP * nq, (g + 1) * NSA_GROUP * nq)
        for lt in range(sel_ref.shape[0]):
            sel_ref[lt, rows, :] = sel[:, lt * LANES:(lt + 1) * LANES]


def nsa_decode_select(q, sums, kc_new, vc_new, pe_k, pe_v, db, nq, past):
    n_cmp = sums[0].shape[2]
    nsb_pad = _round_up(-(-(past + nq) // SEL_BLOCK), LANES)
    sel_rows = NSA_KV_HEADS * NSA_GROUP * nq
    slc = jnp.asarray(_nsa_slc_matrix(n_cmp, nsb_pad))
    cmp_spec = pl.BlockSpec((None, NSA_KVD, n_cmp), lambda bi: (bi, 0, 0))
    new_spec = pl.BlockSpec((nq, NSA_KVD), lambda bi: (bi, 0))
    pe_spec = pl.BlockSpec((CMP_LEN, NSA_KVD), lambda bi: (0, 0))
    return pl.pallas_call(
        functools.partial(_nsa_decode_select_kernel, past),
        out_shape=(jax.ShapeDtypeStruct((db, nsb_pad // LANES, sel_rows, LANES), BF16),
                   jax.ShapeDtypeStruct((db * nq, NSA_QD), F32)),
        grid=(db,),
        in_specs=[pl.BlockSpec((nq, NSA_QD), lambda bi: (bi, 0)), cmp_spec, cmp_spec, cmp_spec, cmp_spec,
                  new_spec, new_spec, pe_spec, pe_spec, pl.BlockSpec(slc.shape, lambda bi: (0, 0))],
        out_specs=(pl.BlockSpec((None, nsb_pad // LANES, sel_rows, LANES), lambda bi: (bi, 0, 0, 0)),
                   pl.BlockSpec((nq, NSA_QD), lambda bi: (bi, 0))),
        compiler_params=_params("parallel"),
        name="nsa_decode_select",
    )(q, *sums, kc_new, vc_new, pe_k.reshape(CMP_LEN, NSA_KVD), pe_v.reshape(CMP_LEN, NSA_KVD), slc)


def _nsa_decode_attend_kernel(past, pt_ref, q_ref, sel_ref, g_ref, oc_ref, ksn_ref, vsn_ref, wk_ref, wv_ref,
                              kwn_ref, vwn_ref, ex_ref, xp_ref, *refs):
    n = PAGES_PER_STEP
    kp, vp = refs[:n], refs[n:2 * n]
    o_ref, qz_scr, m_scr, l_scr, acc_scr = refs[2 * n:]
    j = pl.program_id(1)
    nq = q_ref.shape[0]
    n_keys = n * kp[0].shape[1]
    steps_per_tile = xp_ref.shape[0]
    pos_q = jnp.concatenate([past + lax.broadcasted_iota(jnp.int32, (nq, 1), 0)] * (NSA_KV_HEADS * NSA_GROUP),
                            axis=0)

    @pl.when(j == 0)
    def _():
        m_scr[...] = jnp.full_like(m_scr, -jnp.inf)
        l_scr[...] = jnp.zeros_like(l_scr)
        acc_scr[...] = jnp.zeros_like(acc_scr)
        qz_scr[...] = jnp.concatenate([_decode_queries(q_ref, g) for g in range(NSA_KV_HEADS)], axis=0)

    def attend(s, allowed, v, feature_major):
        m_scr[...], l_scr[...], acc_scr[...] = _online_softmax_step(
            s, allowed, v, m_scr[...], l_scr[...], acc_scr[...], feature_major)

    k_t = jnp.concatenate([r[...].astype(BF16) for r in kp], axis=1)
    v_t = jnp.concatenate([r[...].astype(BF16) for r in vp], axis=1)
    key = j * n_keys + lax.broadcasted_iota(jnp.int32, (1, n_keys), 1)
    chosen = jnp.dot(sel_ref[j // steps_per_tile], xp_ref[j % steps_per_tile], preferred_element_type=F32)
    attend(jnp.dot(qz_scr[...], k_t, preferred_element_type=F32), (chosen > 0.5) & (key <= pos_q), v_t, True)

    @pl.when(j == pl.num_programs(1) - 1)
    def _():
        qz = qz_scr[...]
        lane128 = lax.broadcasted_iota(jnp.int32, (1, LANES), 1)
        n_new = ksn_ref.shape[0]
        new_idx = lax.broadcasted_iota(jnp.int32, (1, n_new), 1)
        new_key = past + new_idx
        new_blk = past // SEL_BLOCK
        chosen_new = sel_ref[new_blk // LANES][:, new_blk % LANES:new_blk % LANES + 1].astype(F32) > 0.5
        attend(lax.dot_general(qz, ksn_ref[...], NT, preferred_element_type=F32),
               chosen_new & (new_key <= pos_q) & (new_idx < nq), vsn_ref[...], False)
        o_s = acc_scr[...] / jnp.maximum(l_scr[...], 1e-30)
        n_win = wk_ref.shape[1]
        k_wc, v_wc = wk_ref[...].astype(BF16), wv_ref[...].astype(BF16)
        pos_wc = past - n_win + lax.broadcasted_iota(jnp.int32, (1, n_win), 1)
        ok_wc = (pos_q - pos_wc >= 0) & (pos_q - pos_wc < WINDOW) & (pos_wc >= 0)
        ok_wn = (pos_q - new_key >= 0) & (pos_q - new_key < WINDOW) & (new_idx < nq)
        win = (jnp.full(m_scr.shape, -jnp.inf, F32), jnp.zeros(l_scr.shape, F32), jnp.zeros(acc_scr.shape, F32))
        win = _online_softmax_step(jnp.dot(qz, k_wc, preferred_element_type=F32), ok_wc, v_wc, *win, True)
        win = _online_softmax_step(lax.dot_general(qz, kwn_ref[...], NT, preferred_element_type=F32), ok_wn,
                                   vwn_ref[...], *win, False)
        o_w = win[2] / jnp.maximum(win[1], 1e-30)
        gexp = jnp.dot(g_ref[...], ex_ref[...], precision=HIGHEST, preferred_element_type=F32)
        o_ref[...] = gexp[:, :NSA_QD] * oc_ref[...]
        for g in range(NSA_KV_HEADS):
            m, half = divmod(g, 2)
            in_half = (lane128 // NSA_HD) == half
            mcols = slice(m * LANES, (m + 1) * LANES)
            for r in range(NSA_GROUP):
                t = m * NSA_GROUP + r
                cols = slice(t * LANES, (t + 1) * LANES)
                rows = slice((g * NSA_GROUP + r) * nq, (g * NSA_GROUP + r + 1) * nq)
                comb = (gexp[:, NSA_QD + t * LANES:NSA_QD + (t + 1) * LANES] * o_s[rows, mcols]
                        + gexp[:, 2 * NSA_QD + t * LANES:2 * NSA_QD + (t + 1) * LANES] * o_w[rows, mcols])
                o_ref[:, cols] += jnp.where(in_half, comb, 0.0)


def nsa_decode_attend(page_table, q, sel, gates, o_c, ks_new, vs_new, wk, wv, kw_new, vw_new, pool_k, pool_v,
                      db, nq, past):
    n_pages = page_table.shape[1]
    page = pool_k.shape[2]
    ex = jnp.asarray(_nsa_gate_expand())
    n_win = wk.shape[2]
    rows = NSA_KV_HEADS * NSA_GROUP * nq
    n_keys = PAGES_PER_STEP * page
    assert past % SEL_BLOCK + nq <= SEL_BLOCK and (LANES * SEL_BLOCK) % n_keys == 0
    steps_per_tile = LANES * SEL_BLOCK // n_keys
    xp = np.zeros((steps_per_tile, LANES, n_keys), np.float32)
    keys = np.arange(n_keys)
    for u in range(steps_per_tile):
        xp[u, u * (n_keys // SEL_BLOCK) + keys // SEL_BLOCK, keys] = 1.0
    xp = jnp.asarray(xp, BF16)
    per_b = lambda shape: pl.BlockSpec((None,) + shape, lambda bi, j, pt: (bi,) + (0,) * len(shape))
    q_rows = lambda width: pl.BlockSpec((nq, width), lambda bi, j, pt: (bi, 0))
    new_rows = ks_new.shape[1]
    return pl.pallas_call(
        functools.partial(_nsa_decode_attend_kernel, past),
        out_shape=jax.ShapeDtypeStruct((db * nq, NSA_QD), F32),
        grid_spec=pltpu.PrefetchScalarGridSpec(
            num_scalar_prefetch=1,
            grid=(db, n_pages // PAGES_PER_STEP),
            in_specs=[q_rows(NSA_QD), per_b(sel.shape[1:]), q_rows(LANES), q_rows(NSA_QD),
                      per_b((new_rows, NSA_KVD)), per_b((new_rows, NSA_KVD)),
                      per_b((NSA_KVD, n_win)), per_b((NSA_KVD, n_win)),
                      per_b((new_rows, NSA_KVD)), per_b((new_rows, NSA_KVD)),
                      pl.BlockSpec(ex.shape, lambda bi, j, pt: (0, 0)),
                      pl.BlockSpec(xp.shape, lambda bi, j, pt: (0, 0, 0))]
            + _page_specs(PAGES_PER_STEP, page) * 2,
            out_specs=q_rows(NSA_QD),
            scratch_shapes=[pltpu.VMEM((rows, NSA_KVD), BF16), pltpu.VMEM((rows, 1), F32),
                            pltpu.VMEM((rows, 1), F32), pltpu.VMEM((rows, NSA_KVD), F32)],
        ),
        compiler_params=_params("parallel", "arbitrary"),
        name="nsa_decode_attend",
    )(page_table, q, sel, gates, o_c, ks_new, vs_new, wk, wv, kw_new, vw_new, ex, xp,
      *([pool_k] * PAGES_PER_STEP), *([pool_v] * PAGES_PER_STEP))


def _nsa_weights(w_in, w_out):
    perm = _nsa_q_perm()
    return jnp.concatenate([w_in[:, perm], w_in[:, NSA_QD:]], axis=1), w_out[perm]


def nsa_prompt(x, scale, shift, w_in_p, pe_k, pe_v, b, s):
    p, _ = project(x, scale, shift, w_in_p, b, s)
    cos, up, dn = _rope_tables(jnp.arange(s, dtype=jnp.int32))
    q, kc, vc, ks, vs, kw, vw, ksb, vsb, kwb, vwb, gates = nsa_prep(p, cos, up, dn, q_dtype=BF16)
    kcmp = nsa_compress(kc, pe_k, b, s)
    vcmp = nsa_compress(vc, pe_v, b, s)
    o = nsa_attn_prompt(q, gates, kcmp, vcmp, ksb, vsb, kwb, vwb, b, s)
    nw = min(WINDOW, s)
    rs = lambda a: a.reshape(b, s, NSA_KV_HEADS, NSA_HD)
    return o, (rs(kc), rs(vc), rs(ks), rs(vs), rs(kw)[:, s - nw:], rs(vw)[:, s - nw:])


def nsa_sample(x, scale, shift, ck, cv, sk, sv, wk, wv, page_table, w_in_p, pe_k, pe_v, db, nq):
    past = page_table.shape[1] * ck.shape[1]
    p, _ = project(x, scale, shift, w_in_p, db, nq)
    pos = past + jnp.tile(jnp.arange(nq, dtype=jnp.int32), db)
    cos, up, dn = _rope_tables(pos)
    q, kc, vc, ks, vs, kw, vw, ksb, vsb, kwb, vwb, gates = nsa_prep(p, cos, up, dn, q_dtype=F32)
    sums = nsa_page_sums(page_table, _feature_major(ck), _feature_major(cv), pe_k, pe_v)
    sel, o_c = nsa_decode_select(q, sums, kc, vc, pe_k, pe_v, db, nq, past)
    new_pad = lambda a: jnp.pad(a.reshape(db, nq, NSA_KVD), ((0, 0), (0, LANES - nq), (0, 0)))
    o = nsa_decode_attend(page_table, q, sel, gates, o_c, new_pad(ksb), new_pad(vsb),
                          _feature_major(wk), _feature_major(wv), new_pad(kwb), new_pad(vwb),
                          _feature_major(sk), _feature_major(sv), db, nq, past)
    rs = lambda a: a.reshape(db, nq, NSA_KV_HEADS, NSA_HD)
    slide = lambda cache, new: jnp.concatenate([cache, rs(new).astype(cache.dtype)], 1)[:, nq:]
    return o, (rs(kc), rs(vc), rs(ks), rs(vs), slide(wk, kw), slide(wv, vw))


def _top_values(s, k):
    vals = []
    for _ in range(k):
        m = jnp.max(s, axis=0, keepdims=True)
        vals.append(m)
        s = jnp.where(s >= m, -jnp.inf, s)
    return vals


def _peer_route_kernel(qv_ref, keys_ref, s1_ref, s2_ref, e1_ref, e2_ref, th_ref):
    for h in range(PEER_HEADS):
        q1 = qv_ref[:, (2 * h) * PEER_DKEY:(2 * h + 1) * PEER_DKEY]
        q2 = qv_ref[:, (2 * h + 1) * PEER_DKEY:(2 * h + 2) * PEER_DKEY]
        s1 = lax.dot_general(keys_ref[h, 0], q1, NT, preferred_element_type=F32)
        s2 = lax.dot_general(keys_ref[h, 1], q2, NT, preferred_element_type=F32)
        top1 = _top_values(s1, PEER_TOPK)
        top2 = _top_values(s2, PEER_TOPK)
        t1 = jnp.concatenate(top1, axis=0)
        t2 = jnp.concatenate(top2, axis=0)
        row8 = lax.broadcasted_iota(jnp.int32, (SUBLANES, 1), 0)
        pairs = [top1[0] + t2, top1[1] + t2[:SUBLANES], t1[SUBLANES:] + top2[0]]
        for i in range(2, SUBLANES):
            pairs.append(jnp.where(row8 < PEER_TOPK // (i + 1), top1[i] + t2[:SUBLANES], -jnp.inf))
        best = _top_values(jnp.concatenate(pairs, axis=0), PEER_TOPK)
        z = best[0] * 0.0
        for v in best:
            z = z + jnp.exp(v - best[0])
        e1 = jnp.exp(s1 - top1[0])
        e2 = jnp.exp(s2 - top2[0]) / z
        for c in range(qv_ref.shape[0] // LANES):
            cols = slice(c * LANES, (c + 1) * LANES)
            s1_ref[h, c] = s1[:, cols]
            s2_ref[h, c] = s2[:, cols]
            e1_ref[h, c] = e1[:, cols]
            e2_ref[h, c] = e2[:, cols]
            th_ref[h, c] = best[PEER_TOPK - 1][:, cols]


def peer_route(qv, keys, *, tm):
    t = qv.shape[0]
    nc = tm // LANES
    big = jax.ShapeDtypeStruct((PEER_HEADS, t // LANES, PEER_NKEYS, LANES), F32)
    big_spec = pl.BlockSpec((PEER_HEADS, nc, PEER_NKEYS, LANES), lambda i: (0, i, 0, 0))
    return pl.pallas_call(
        _peer_route_kernel,
        out_shape=(big, big, big, big, jax.ShapeDtypeStruct((PEER_HEADS, t // LANES, 1, LANES), F32)),
        grid=(t // tm,),
        in_specs=[
            pl.BlockSpec((tm, 2 * PEER_HEADS * PEER_DKEY), lambda i: (i, 0)),
            pl.BlockSpec(keys.shape, lambda i: (0, 0, 0, 0)),
        ],
        out_specs=(big_spec, big_spec, big_spec, big_spec,
                   pl.BlockSpec((PEER_HEADS, nc, 1, LANES), lambda i: (0, i, 0, 0))),
        compiler_params=_params("parallel"),
        name="peer_route",
    )(qv, keys)


PEER_A_PER_STEP = SUBLANES
PEER_ROUTE_TILE = 256


def _gelu_tanh(x):
    return 0.5 * x * (1.0 + jnp.tanh(math.sqrt(2.0 / math.pi) * (x + 0.044715 * (x * x * x))))


def _peer_dense_kernel(h_ref, s1_ref, e1_ref, s2_ref, e2_ref, th_ref, u_ref, vt_ref, o_ref,
                       act_ref, g_ref, acc_ref):
    j = pl.program_id(1)
    tm = h_ref.shape[0]

    @pl.when(j == 0)
    def _():
        acc_ref[...] = jnp.zeros_like(acc_ref)

    nc = tm // LANES
    act = lax.dot_general(u_ref[...], h_ref[...], NT, preferred_element_type=F32)
    for c in range(nc):
        act_ref[c] = act[:, c * LANES:(c + 1) * LANES]

    def tile(idx, carry):
        a, c = idx // nc, idx % nc
        rows = pl.ds(pl.multiple_of(a * PEER_NKEYS, PEER_NKEYS), PEER_NKEYS)
        w = jnp.zeros((PEER_NKEYS, LANES), F32)
        for h in range(PEER_HEADS):
            cand = s2_ref[h, c] + s1_ref[h, c, pl.ds(a, 1), :]
            w = w + e1_ref[h, c, pl.ds(a, 1), :] * jnp.where(cand >= th_ref[h, c], e2_ref[h, c], 0.0)
        g_ref[c, rows, :] = (w * _gelu_tanh(act_ref[c, rows, :])).astype(BF16)
        return carry

    lax.fori_loop(0, PEER_A_PER_STEP * nc, tile, 0)
    g = jnp.concatenate([g_ref[c] for c in range(nc)], axis=1)
    acc_ref[...] += jnp.dot(vt_ref[...], g, preferred_element_type=F32)

    @pl.when(j == pl.num_programs(1) - 1)
    def _():
        o_ref[...] = acc_ref[...].T


def peer_dense(h, s1, s2, e1, e2, th, u, vt, *, tm):
    t, d = h.shape
    ne = PEER_A_PER_STEP * PEER_NKEYS
    nc = tm // LANES
    tok = pl.BlockSpec((PEER_HEADS, nc, PEER_NKEYS, LANES), lambda i, j: (0, i, 0, 0))
    arow = pl.BlockSpec((PEER_HEADS, nc, PEER_A_PER_STEP, LANES), lambda i, j: (0, i, j, 0))
    return pl.pallas_call(
        _peer_dense_kernel,
        out_shape=jax.ShapeDtypeStruct((t, d), F32),
        grid=(t // tm, PEER_NKEYS // PEER_A_PER_STEP),
        in_specs=[
            pl.BlockSpec((tm, d), lambda i, j: (i, 0)),
            arow, arow, tok, tok,
            pl.BlockSpec((PEER_HEADS, nc, 1, LANES), lambda i, j: (0, i, 0, 0)),
            pl.BlockSpec((ne, d), lambda i, j: (j, 0)),
            pl.BlockSpec((d, ne), lambda i, j: (0, j)),
        ],
        out_specs=pl.BlockSpec((tm, d), lambda i, j: (i, 0)),
        scratch_shapes=[pltpu.VMEM((nc, ne, LANES), F32), pltpu.VMEM((nc, ne, LANES), BF16),
                        pltpu.VMEM((d, tm), F32)],
        compiler_params=_params("parallel", "arbitrary"),
        name="peer_dense",
    )(h, s1, e1, s2, e2, th, u, vt)


def _peer_tables(u, v):
    return u.astype(BF16), v.astype(BF16).T


def peer_ffn(x, scale, shift, wq, keys, u_bf, vt_bf, b, s):
    qv, h = project(x, scale, shift, wq, b, s)
    tm = _row_tile(b * s)
    s1, s2, e1, e2, th = peer_route(qv, keys, tm=min(tm, PEER_ROUTE_TILE))
    return peer_dense(h, s1, s2, e1, e2, th, u_bf, vt_bf, tm=tm)


def kernel(x_prompt, x_sample, state_gla, cache_cmp_k, cache_cmp_v, cache_sel_k, cache_sel_v, cache_win_k, cache_win_v, page_table, c_prompt, c_sample, ada_w, ada_b, ln_g, ln_b, gla_w_in, gla_w_a2, gla_b_a2, gla_gn, gla_w_out, nsa_w_in, nsa_pe_k, nsa_pe_v, nsa_w_out, peer_wq, peer_keys, peer_u, peer_v):
    bp, sp, d = x_prompt.shape
    bs, ss, _ = x_sample.shape
    groups = ((bp, sp), (bs, ss))
    c_all = jnp.concatenate([c_prompt, c_sample], axis=0)
    c_rows = _round_up(bp + bs, SUBLANES)
    mod = ada_mod(jnp.pad(c_all, ((0, c_rows - bp - bs), (0, 0))), ada_w, ada_b)
    ys = [x_prompt.reshape(bp * sp, d), x_sample.reshape(bs * ss, d)]
    gla_states, nsa_rows = ([], []), ([], [])
    for i in range(DEPTH):
        row0 = (0, bp)
        mods = [[mod[i, row0[n]:row0[n] + b, k * d:(k + 1) * d][:, None, :] for k in range(6)]
                for n, (b, _) in enumerate(groups)]
        j = i // N_MIXERS
        if i % N_MIXERS == 0:
            for n, (b, s) in enumerate(groups):
                shift, scale, gate = mods[n][:3]
                proj, _ = project(ys[n], scale, shift, gla_w_in[j], b, s)
                o, st = gla_recurrence(proj, gla_w_a2[j], gla_b_a2[j], None if n == 0 else state_gla[j], b, s)
                gla_states[n].append(st.astype(state_gla.dtype))
                ys[n] = post(ys[n], gate, ln_g[i, 0], ln_b[i, 0], b, s, sub=o, w_out=gla_w_out[j],
                             gla_proj=proj, gla_gn=gla_gn[j])
        else:
            w_in_p, w_out_p = _nsa_weights(nsa_w_in[j], nsa_w_out[j])
            for n, (b, s) in enumerate(groups):
                shift, scale, gate = mods[n][:3]
                if n == 0:
                    o, rows = nsa_prompt(ys[n], scale, shift, w_in_p, nsa_pe_k[j], nsa_pe_v[j], b, s)
                else:
                    o, rows = nsa_sample(ys[n], scale, shift, cache_cmp_k[j], cache_cmp_v[j], cache_sel_k[j],
                                         cache_sel_v[j], cache_win_k[j], cache_win_v[j], page_table,
                                         w_in_p, nsa_pe_k[j], nsa_pe_v[j], b, s)
                nsa_rows[n].append(rows)
                ys[n] = post(ys[n], gate, ln_g[i, 0], ln_b[i, 0], b, s, sub=o, w_out=w_out_p)
        u4, vt4 = _peer_tables(peer_u[i], peer_v[i])
        for n, (b, s) in enumerate(groups):
            shift, scale, gate = mods[n][3:]
            f = peer_ffn(ys[n], scale, shift, peer_wq[i], peer_keys[i], u4, vt4, b, s)
            ys[n] = post(ys[n], gate, ln_g[i, 1], ln_b[i, 1], b, s, sub=f)

    st = lambda ts, k: jnp.stack([t[k] for t in ts])
    return (ys[0].reshape(bp, sp, d), ys[1].reshape(bs, ss, d),
            jnp.stack(gla_states[0]), jnp.stack(gla_states[1]),
            *(st(nsa_rows[0], k) for k in range(6)), *(st(nsa_rows[1], k) for k in range(6)))
```

```python
import functools
import math

import jax
import jax.numpy as jnp
import numpy as np
from jax import lax
from jax.experimental import pallas as pl
from jax.experimental.pallas import tpu as pltpu

D_MODEL = 1024
DEPTH = 2
N_MIXERS = 2
DN_ALPHA = (2.0 * DEPTH) ** 0.25
LN_EPS = 1e-5
F32 = jnp.float32
BF16 = jnp.bfloat16
HIGHEST = lax.Precision.HIGHEST

GLA_HEADS = 4
GLA_DK = D_MODEL // 2 // GLA_HEADS
GLA_DV = D_MODEL // GLA_HEADS
GLA_TAU = 16.0
GLA_CHUNK = 64
GLA_HK = GLA_HEADS * GLA_DK
GLA_HV = GLA_HEADS * GLA_DV

NSA_HEADS = 16
NSA_KV_HEADS = 4
NSA_GROUP = NSA_HEADS // NSA_KV_HEADS
NSA_HD = D_MODEL // NSA_HEADS
NSA_QD = NSA_HEADS * NSA_HD
NSA_KVD = NSA_KV_HEADS * NSA_HD
CMP_LEN = 32
CMP_STRIDE = 16
SEL_BLOCK = 64
N_SEL = 16
WINDOW = 512
Q_BLOCK = 128
FORCE_SCORE = 1e6
ROT_DIM = NSA_HD // 4
ROPE_THETA = 500000.0

PEER_HEADS = 8
PEER_NKEYS = 128
PEER_DKEY = 128
PEER_TOPK = 16

LANES = 128
SUBLANES = 8
VMEM_LIMIT_BYTES = 56 * 1024 * 1024
ROW_TILE = 512

NT = (((1,), (1,)), ((), ()))
TN = (((0,), (0,)), ((), ()))


def _round_up(n, m):
    return -(-n // m) * m


def _params(*sem):
    return pltpu.CompilerParams(dimension_semantics=sem, vmem_limit_bytes=VMEM_LIMIT_BYTES)


def _row_tile(t):
    return ROW_TILE if t % ROW_TILE == 0 else t


def _mod_operand(m, b, s, tm):
    d = m.shape[-1]
    if s % tm == 0:
        return m, pl.BlockSpec((None, 1, d), lambda i, *_: (i * tm // s, 0, 0))
    rows = jnp.broadcast_to(m, (b, s, d)).reshape(b * s, d)
    return rows, pl.BlockSpec((tm, d), lambda i, *_: (i, 0))


def _ada_kernel(c_ref, w_ref, b_ref, o_ref):
    c = c_ref[...]
    act = (c * jax.nn.sigmoid(c)).astype(BF16)
    o_ref[...] = jnp.dot(act, w_ref[...].astype(BF16), preferred_element_type=F32) + b_ref[...]


def ada_mod(c, w, b):
    r, d = c.shape
    nl, _, n = w.shape
    tn = n // 4
    return pl.pallas_call(
        _ada_kernel,
        out_shape=jax.ShapeDtypeStruct((nl, r, n), F32),
        grid=(nl, n // tn),
        in_specs=[pl.BlockSpec((r, d), lambda l, j: (0, 0)),
                  pl.BlockSpec((None, d, tn), lambda l, j: (l, 0, j)),
                  pl.BlockSpec((None, 1, tn), lambda l, j: (l, 0, j))],
        out_specs=pl.BlockSpec((None, r, tn), lambda l, j: (l, 0, j)),
        compiler_params=_params("parallel", "parallel"),
        name="ada_mod",
    )(c, w, b.reshape(nl, 1, n))


def _mod_matmul_kernel(x_ref, sc_ref, sh_ref, w_ref, o_ref, h_ref):
    @pl.when(pl.program_id(1) == 0)
    def _():
        h = x_ref[...] * (1.0 + sc_ref[...]) + sh_ref[...]
        h_ref[...] = h.astype(BF16)

    o_ref[...] = jnp.dot(h_ref[...], w_ref[...], preferred_element_type=F32)


def project(x, scale, shift, w, b, s):
    t, d = x.shape
    npad = _round_up(w.shape[1], LANES)
    wp = jnp.pad(w, ((0, 0), (0, npad - w.shape[1]))).astype(BF16)
    tn = npad
    tm = _row_tile(t)
    sc, mod_spec = _mod_operand(scale, b, s, tm)
    sh, _ = _mod_operand(shift, b, s, tm)
    return pl.pallas_call(
        _mod_matmul_kernel,
        out_shape=(jax.ShapeDtypeStruct((t, npad), F32), jax.ShapeDtypeStruct((t, d), BF16)),
        grid=(t // tm, npad // tn),
        in_specs=[pl.BlockSpec((tm, d), lambda i, j: (i, 0)), mod_spec, mod_spec,
                  pl.BlockSpec((d, tn), lambda i, j: (0, j))],
        out_specs=(pl.BlockSpec((tm, tn), lambda i, j: (i, j)), pl.BlockSpec((tm, d), lambda i, j: (i, 0))),
        compiler_params=_params("parallel", "arbitrary"),
        name="mod_matmul",
    )(x, sc, sh, wp)


def _deepnorm(x, sub, gate, g, b):
    y = DN_ALPHA * x + (1.0 + gate) * sub
    mu = jnp.mean(y, axis=-1, keepdims=True)
    var = jnp.mean(jnp.square(y - mu), axis=-1, keepdims=True)
    return (y - mu) * lax.rsqrt(var + LN_EPS) * g + b


def _post_gla_kernel(o_ref, r_ref, x_ref, gate_ref, gn_ref, w_ref, g_ref, b_ref, y_ref):
    parts = []
    for h in range(GLA_HEADS):
        cols = slice(h * GLA_DV, (h + 1) * GLA_DV)
        o = o_ref[:, cols]
        mu = jnp.mean(o, axis=-1, keepdims=True)
        var = jnp.mean(jnp.square(o - mu), axis=-1, keepdims=True)
        parts.append((o - mu) * lax.rsqrt(var + LN_EPS) * gn_ref[:, cols])
    r = r_ref[...]
    f = (jnp.concatenate(parts, axis=1) * (r * jax.nn.sigmoid(r))).astype(BF16)
    sub = jnp.dot(f, w_ref[...], preferred_element_type=F32)
    y_ref[...] = _deepnorm(x_ref[...], sub, gate_ref[...], g_ref[...], b_ref[...])


def _post_matmul_kernel(o_ref, x_ref, gate_ref, w_ref, g_ref, b_ref, y_ref):
    sub = jnp.dot(o_ref[...].astype(BF16), w_ref[...], preferred_element_type=F32)
    y_ref[...] = _deepnorm(x_ref[...], sub, gate_ref[...], g_ref[...], b_ref[...])


def _post_plain_kernel(o_ref, x_ref, gate_ref, g_ref, b_ref, y_ref):
    y_ref[...] = _deepnorm(x_ref[...], o_ref[...], gate_ref[...], g_ref[...], b_ref[...])


def post(x, gate, ln_g, ln_b, b, s, *, sub=None, w_out=None, gla_proj=None, gla_gn=None):
    t, d = x.shape
    tm = _row_tile(t)
    gate_arr, gate_spec = _mod_operand(gate, b, s, tm)
    row = pl.BlockSpec((tm, d), lambda i: (i, 0))
    vec = pl.BlockSpec((1, d), lambda i: (0, 0))
    mat = pl.BlockSpec((d, d), lambda i: (0, 0))
    g2, b2 = ln_g.reshape(1, d), ln_b.reshape(1, d)
    if gla_proj is not None:
        r_spec = pl.BlockSpec((tm, GLA_HV), lambda i: (i, (2 * GLA_HK + GLA_HV) // GLA_HV))
        args = (sub, gla_proj, x, gate_arr, gla_gn.reshape(1, d), w_out.astype(BF16), g2, b2)
        specs = [row, r_spec, row, gate_spec, vec, mat, vec, vec]
        body = _post_gla_kernel
    elif w_out is not None:
        args = (sub, x, gate_arr, w_out.astype(BF16), g2, b2)
        specs = [row, row, gate_spec, mat, vec, vec]
        body = _post_matmul_kernel
    else:
        args = (sub, x, gate_arr, g2, b2)
        specs = [row, row, gate_spec, vec, vec]
        body = _post_plain_kernel
    return pl.pallas_call(
        body,
        out_shape=jax.ShapeDtypeStruct((t, d), F32),
        grid=(t // tm,),
        in_specs=specs,
        out_specs=row,
        compiler_params=_params("parallel"),
        name="post",
    )(*args)


GLA_MIN_ROWS = 64


def _cumsum_rows(x):
    n = x.shape[0]
    row = lax.broadcasted_iota(jnp.int32, (n, 1), 0)
    shift = 1
    while shift < n:
        x = x + jnp.where(row >= shift, pltpu.roll(x, shift, 0), 0.0)
        shift *= 2
    return x


def _gla_kernel(has_s0, q_ref, k_ref, v_ref, a_ref, wa_ref, ba_ref, *rest):
    if has_s0:
        s0_ref, o_ref, st_ref, s_scr = rest
    else:
        o_ref, st_ref, s_scr = rest
    c = pl.program_id(1)
    n_rows = q_ref.shape[0]
    rows = max(n_rows, GLA_MIN_ROWS)

    @pl.when(c == 0)
    def _():
        s_scr[...] = s0_ref[...] if has_s0 else jnp.zeros_like(s_scr)

    def padded(x):
        if rows == n_rows:
            return x
        return jnp.concatenate([x, jnp.zeros((rows - n_rows, x.shape[1]), x.dtype)], axis=0)

    z = jnp.dot(a_ref[...], wa_ref[...], precision=HIGHEST, preferred_element_type=F32) + ba_ref[...]
    logg = (jnp.minimum(z, 0.0) - jnp.log1p(jnp.exp(-jnp.abs(z)))) / GLA_TAU
    b = _cumsum_rows(padded(logg))
    q, k, v = padded(q_ref[...]), padded(k_ref[...]), padded(v_ref[...])
    qe = q * (GLA_DK ** -0.5) * jnp.exp(b)
    ke = k * jnp.exp(-b)
    b_last = b[rows - 1:rows, :]
    kd = k * jnp.exp(b_last - b)
    e_last = jnp.exp(b_last)
    causal = lax.broadcasted_iota(jnp.int32, (rows, 1), 0) >= lax.broadcasted_iota(jnp.int32, (1, rows), 1)
    eye = lax.broadcasted_iota(jnp.int32, (GLA_DK, 1), 0) == lax.broadcasted_iota(jnp.int32, (1, GLA_DK), 1)
    for h in range(GLA_HEADS):
        ck = slice(h * GLA_DK, (h + 1) * GLA_DK)
        cv = slice(h * GLA_DV, (h + 1) * GLA_DV)
        att = lax.dot_general(qe[:, ck], ke[:, ck], NT, preferred_element_type=F32)
        att = jnp.where(causal, att, 0.0)
        s_h = s_scr[h]
        o = (jnp.dot(att, v[:, cv], preferred_element_type=F32)
             + jnp.dot(qe[:, ck], s_h, preferred_element_type=F32))
        o_ref[:, cv] = o[:n_rows]
        e_col = jnp.sum(jnp.where(eye, e_last[:, ck], 0.0), axis=1, keepdims=True)
        s_scr[h] = e_col * s_h + lax.dot_general(kd[:, ck], v[:, cv], TN, preferred_element_type=F32)

    @pl.when(c == pl.num_programs(1) - 1)
    def _():
        st_ref[...] = s_scr[...]


def gla_recurrence(proj, w_a2, b_a2, s0, b, s):
    chunk = GLA_CHUNK if s % GLA_CHUNK == 0 else s
    nc = s // chunk
    wa = jnp.pad(w_a2, ((0, LANES - w_a2.shape[0]), (0, 0)))
    a_block = (2 * GLA_HK + 2 * GLA_HV) // LANES
    row = lambda width, blk: pl.BlockSpec((chunk, width), lambda bi, c: (bi * nc + c, blk))
    st_spec = pl.BlockSpec((None, GLA_HEADS, GLA_DK, GLA_DV), lambda bi, c: (bi, 0, 0, 0))
    in_specs = [row(GLA_HK, 0), row(GLA_HK, 1), row(GLA_HV, 2 * GLA_HK // GLA_HV), row(LANES, a_block),
                pl.BlockSpec(wa.shape, lambda bi, c: (0, 0)), pl.BlockSpec((1, GLA_HK), lambda bi, c: (0, 0))]
    args = [proj, proj, proj, proj, wa, b_a2.reshape(1, GLA_HK)]
    if s0 is not None:
        in_specs.append(st_spec)
        args.append(s0)
    return pl.pallas_call(
        functools.partial(_gla_kernel, s0 is not None),
        out_shape=(jax.ShapeDtypeStruct((b * s, GLA_HV), F32),
                   jax.ShapeDtypeStruct((b, GLA_HEADS, GLA_DK, GLA_DV), F32)),
        grid=(b, nc),
        in_specs=in_specs,
        out_specs=(pl.BlockSpec((chunk, GLA_HV), lambda bi, c: (bi * nc + c, 0)), st_spec),
        scratch_shapes=[pltpu.VMEM((GLA_HEADS, GLA_DK, GLA_DV), F32)],
        compiler_params=_params("parallel", "arbitrary"),
        name="gla_recurrence",
    )(*args)


def _nsa_q_perm():
    cols = []
    for m in range(NSA_KV_HEADS // 2):
        for r in range(NSA_GROUP):
            for half in range(2):
                head = (2 * m + half) * NSA_GROUP + r
                cols.append(np.arange(NSA_HD) + head * NSA_HD)
    return np.concatenate(cols)


def _nsa_gate_expand():
    perm = _nsa_q_perm()
    ex = np.zeros((LANES, 3 * NSA_QD), np.float32)
    for br in range(3):
        for col in range(NSA_QD):
            ex[br * NSA_HEADS + perm[col] // NSA_HD, br * NSA_QD + col] = 1.0
    return ex


def _nsa_slc_matrix(nc_pad, nsb_pad):
    ratio = SEL_BLOCK // CMP_STRIDE
    m = np.zeros((nc_pad, nsb_pad), np.float32)
    for j in range(nsb_pad):
        for o in range(CMP_LEN // CMP_STRIDE):
            for i in range(ratio):
                n = ratio * j + i + o - (CMP_LEN // CMP_STRIDE - 1)
                if 0 <= n < nc_pad:
                    m[n, j] += 1.0
    return m


def _rope_tables(pos):
    half = ROT_DIM // 2
    inv = ROPE_THETA ** (-jnp.arange(half, dtype=F32) * 2.0 / ROT_DIM)
    ang = pos.astype(F32)[:, None] * inv[None, :]
    cos, sin = jnp.cos(ang), jnp.sin(ang)
    t = pos.shape[0]
    ones = jnp.ones((t, NSA_HD - ROT_DIM), F32)
    zeros = jnp.zeros((t, NSA_HD - ROT_DIM), F32)
    z8 = jnp.zeros((t, half), F32)
    c = jnp.concatenate([cos, cos, ones], 1)
    up = jnp.concatenate([-sin, z8, zeros], 1)
    dn = jnp.concatenate([z8, sin, zeros], 1)
    two = lambda a: jnp.concatenate([a, a], 1)
    return two(c), two(up), two(dn)


def _nsa_prep_kernel(p_ref, cos_ref, up_ref, dn_ref, q_ref, kc_ref, vc_ref, ks_ref, vs_ref, kw_ref, vw_ref,
                     ksb_ref, vsb_ref, kwb_ref, vwb_ref, g_ref):
    def rope(x):
        reps = x.shape[1] // LANES
        tile = lambda a: jnp.concatenate([a] * reps, axis=1)
        w = x.shape[1]
        return (x * tile(cos_ref[...]) + pltpu.roll(x, w - ROT_DIM // 2, 1) * tile(up_ref[...])
                + pltpu.roll(x, ROT_DIM // 2, 1) * tile(dn_ref[...]))

    q_ref[...] = (rope(p_ref[:, :NSA_QD]) * (NSA_HD ** -0.5)).astype(q_ref.dtype)
    kv = lambda k: p_ref[:, NSA_QD + k * NSA_KVD:NSA_QD + (k + 1) * NSA_KVD]
    kc_ref[...] = rope(kv(0))
    vc_ref[...] = kv(1)
    ks = rope(kv(2))
    ks_ref[...] = ks
    ksb_ref[...] = ks.astype(BF16)
    vs_ref[...] = kv(3)
    vsb_ref[...] = kv(3).astype(BF16)
    kw = rope(kv(4))
    kw_ref[...] = kw
    kwb_ref[...] = kw.astype(BF16)
    vw_ref[...] = kv(5)
    vwb_ref[...] = kv(5).astype(BF16)
    g0 = NSA_QD + 6 * NSA_KVD
    g_ref[...] = jax.nn.sigmoid(p_ref[:, g0:g0 + LANES])


def nsa_prep(p, cos, up, dn, *, q_dtype):
    t = p.shape[0]
    tm = _row_tile(t)
    period = cos.shape[0] // tm
    tab = pl.BlockSpec((tm, LANES), lambda i: (i % period, 0))
    kv32 = jax.ShapeDtypeStruct((t, NSA_KVD), F32)
    kv16 = jax.ShapeDtypeStruct((t, NSA_KVD), BF16)
    kvs = pl.BlockSpec((tm, NSA_KVD), lambda i: (i, 0))
    return pl.pallas_call(
        _nsa_prep_kernel,
        out_shape=(jax.ShapeDtypeStruct((t, NSA_QD), q_dtype),) + (kv32,) * 6 + (kv16,) * 4
        + (jax.ShapeDtypeStruct((t, LANES), F32),),
        grid=(t // tm,),
        in_specs=[pl.BlockSpec((tm, p.shape[1]), lambda i: (i, 0)), tab, tab, tab],
        out_specs=(pl.BlockSpec((tm, NSA_QD), lambda i: (i, 0)),) + (kvs,) * 10
        + (pl.BlockSpec((tm, LANES), lambda i: (i, 0)),),
        compiler_params=_params("parallel"),
        name="nsa_prep",
    )(p, cos, up, dn)


def _block_sums(x, w):
    n_sub = x.shape[0] // CMP_STRIDE
    x = x.reshape(n_sub, CMP_STRIDE, NSA_KVD)
    w = 1.0 + w
    return jnp.sum(x * w[None, :CMP_STRIDE], axis=1), jnp.sum(x * w[None, CMP_STRIDE:], axis=1)


def _nsa_compress_kernel(x_ref, w_ref, o_ref):
    first, second = _block_sums(x_ref[...], w_ref[...])
    n_sub = first.shape[0]
    nxt = pltpu.roll(second, n_sub - 1, 0)
    row = lax.broadcasted_iota(jnp.int32, (n_sub, 1), 0)
    o_ref[...] = jnp.where(row < n_sub - 1, (first + nxt) / CMP_LEN, 0.0).astype(o_ref.dtype)


def nsa_compress(x, pe, b, s):
    n_sub = s // CMP_STRIDE
    return pl.pallas_call(
        _nsa_compress_kernel,
        out_shape=jax.ShapeDtypeStruct((b, n_sub, NSA_KVD), BF16),
        grid=(b,),
        in_specs=[pl.BlockSpec((s, NSA_KVD), lambda i: (i, 0)),
                  pl.BlockSpec((CMP_LEN, NSA_KVD), lambda i: (0, 0))],
        out_specs=pl.BlockSpec((None, n_sub, NSA_KVD), lambda i: (i, 0, 0)),
        compiler_params=_params("parallel"),
        name="nsa_compress",
    )(x, pe.reshape(CMP_LEN, NSA_KVD))


NSA_KV_TILE = 512


def _mask_rows(s, allowed):
    if allowed.shape[0] == s.shape[0]:
        return jnp.where(allowed, s, -jnp.inf)
    reps = s.shape[0] // allowed.shape[0]
    s3 = s.reshape(reps, allowed.shape[0], s.shape[1])
    return jnp.where(allowed[None], s3, -jnp.inf).reshape(s.shape)


def _softmax_rows(s, allowed):
    s = _mask_rows(s, allowed)
    m = jnp.max(s, axis=1, keepdims=True)
    m = jnp.where(m == -jnp.inf, 0.0, m)
    e = jnp.exp(s - m)
    return e / jnp.maximum(jnp.sum(e, axis=1, keepdims=True), 1e-30)


def _select_blocks(score, n_sel):
    nb = score.shape[1]
    lane = lax.broadcasted_iota(jnp.int32, score.shape, 1).astype(F32)
    sel = jnp.zeros(score.shape, F32)
    for _ in range(n_sel):
        m = jnp.max(score, axis=1, keepdims=True)
        first = jnp.min(jnp.where(score == m, lane, float(nb)), axis=1, keepdims=True)
        pick = lane == first
        sel = jnp.where(pick, 1.0, sel)
        score = jnp.where(pick, -jnp.inf, score)
    return sel


def _block_scores(p_slc, pos_q):
    jb = lax.broadcasted_iota(jnp.int32, (1, p_slc.shape[1]), 1)
    cur = pos_q // SEL_BLOCK
    valid = jb * SEL_BLOCK <= pos_q
    forced = (jb == 0) | (jb == cur) | (jb == cur - 1)
    return jnp.where(forced, FORCE_SCORE, jnp.where(valid, p_slc, -1.0))


def _select_blocks_t(score, n_sel):
    st = score.T
    nb = st.shape[0]
    blk = lax.broadcasted_iota(jnp.int32, st.shape, 0).astype(F32)
    sel = jnp.zeros(st.shape, F32)
    for _ in range(n_sel):
        m = jnp.max(st, axis=0, keepdims=True)
        first = jnp.min(jnp.where(st == m, blk, float(nb)), axis=0, keepdims=True)
        pick = blk == first
        sel = jnp.where(pick, 1.0, sel)
        st = jnp.where(pick, -jnp.inf, st)
    return sel.T


def _online_softmax_step(s, allowed, v, m_old, l_old, acc_old, feature_major=False):
    s = _mask_rows(s, allowed)
    m_new = jnp.maximum(m_old, jnp.max(s, axis=1, keepdims=True))
    m_use = jnp.where(m_new == -jnp.inf, 0.0, m_new)
    alpha = jnp.exp(m_old - m_use)
    p = jnp.exp(s - m_use)
    l_new = alpha * l_old + jnp.sum(p, axis=1, keepdims=True)
    if feature_major:
        pv = lax.dot_general(p.astype(BF16), v, NT, preferred_element_type=F32)
    else:
        pv = jnp.dot(p.astype(BF16), v, preferred_element_type=F32)
    return m_new, l_new, alpha * acc_old + pv


def _padded_query(tile, m, in_half):
    sel = jnp.where(in_half, tile, jnp.zeros_like(tile))
    z = jnp.zeros_like(sel)
    return jnp.concatenate([sel, z] if m == 0 else [z, sel], axis=1)


def _nsa_attn_kernel(q_ref, g_ref, kcmp_ref, vcmp_ref, ks_ref, vs_ref, kw_ref, vw_ref, slc_ref, ex_ref,
                     o_ref, qz_ref, sel_ref, oc_ref, m_ref, l_ref, acc_ref):
    i = pl.program_id(1)
    start = i * Q_BLOCK
    pos_q = start + lax.broadcasted_iota(jnp.int32, (Q_BLOCK, 1), 0)
    lane128 = lax.broadcasted_iota(jnp.int32, (1, LANES), 1)
    n_cmp = kcmp_ref.shape[1]
    cmp_end = CMP_STRIDE * lax.broadcasted_iota(jnp.int32, (1, n_cmp), 1) + (CMP_LEN - 1)
    cmp_ok = cmp_end <= pos_q
    gexp = jnp.dot(g_ref[...], ex_ref[...], precision=HIGHEST, preferred_element_type=F32)
    o_ref[...] = jnp.zeros_like(o_ref)

    win_base = pl.multiple_of(jnp.maximum(start - WINDOW, 0), Q_BLOCK)
    band = WINDOW + Q_BLOCK
    kw_t = jnp.concatenate([kw_ref[win_base // Q_BLOCK + u] for u in range(band // Q_BLOCK)], axis=1)
    vw_t = vw_ref[pl.ds(win_base, band), :]
    dpos = pos_q - (win_base + lax.broadcasted_iota(jnp.int32, (1, band), 1))
    win_ok = (dpos >= 0) & (dpos < WINDOW)

    for g in range(NSA_KV_HEADS):
        m, half = divmod(g, 2)
        in_half = (lane128 // NSA_HD) == half
        qz = jnp.concatenate(
            [_padded_query(q_ref[:, (m * NSA_GROUP + r) * LANES:(m * NSA_GROUP + r + 1) * LANES], m, in_half)
             for r in range(NSA_GROUP)], axis=0)
        qz_ref[g] = qz
        s = jnp.dot(qz, kcmp_ref[...], preferred_element_type=F32)
        p = _softmax_rows(s, cmp_ok)
        oc_ref[g] = jnp.dot(p.astype(BF16), vcmp_ref[...], preferred_element_type=F32)[:, m * LANES:(m + 1) * LANES]
        imp = p[0:Q_BLOCK]
        for r in range(1, NSA_GROUP):
            imp = imp + p[r * Q_BLOCK:(r + 1) * Q_BLOCK]
        p_slc = jnp.dot(imp, slc_ref[...], precision=HIGHEST, preferred_element_type=F32)
        sel_ref[g] = _select_blocks_t(_block_scores(p_slc, pos_q), N_SEL).astype(BF16)

    m_ref[...] = jnp.full_like(m_ref, -jnp.inf)
    l_ref[...] = jnp.zeros_like(l_ref)
    acc_ref[...] = jnp.zeros_like(acc_ref)

    def kv_step(t, carry):
        k0 = pl.multiple_of(t * NSA_KV_TILE, NSA_KV_TILE)
        k_t = ks_ref[t]
        v_t = vs_ref[pl.ds(k0, NSA_KV_TILE), :]
        key = k0 + lax.broadcasted_iota(jnp.int32, (1, NSA_KV_TILE), 1)
        blk = lax.broadcasted_iota(jnp.int32, (LANES, 1), 0)
        expand = jnp.where(blk == key // SEL_BLOCK, 1.0, 0.0).astype(BF16)
        causal = key <= pos_q
        for g in range(NSA_KV_HEADS):
            chosen = jnp.dot(sel_ref[g], expand, preferred_element_type=F32)
            s = jnp.dot(qz_ref[g], k_t, preferred_element_type=F32)
            m_ref[g], l_ref[g], acc_ref[g] = _online_softmax_step(s, (chosen > 0.5) & causal, v_t,
                                                                  m_ref[g], l_ref[g], acc_ref[g])
        return carry

    lax.fori_loop(0, (start + Q_BLOCK + NSA_KV_TILE - 1) // NSA_KV_TILE, kv_step, 0)

    for g in range(NSA_KV_HEADS):
        m, half = divmod(g, 2)
        in_half = (lane128 // NSA_HD) == half
        mcols = slice(m * LANES, (m + 1) * LANES)
        o_s = acc_ref[g][:, mcols] / jnp.maximum(l_ref[g], 1e-30)
        s = jnp.dot(qz_ref[g], kw_t, preferred_element_type=F32)
        o_w = jnp.dot(_softmax_rows(s, win_ok).astype(BF16), vw_t, preferred_element_type=F32)[:, mcols]
        o_c = oc_ref[g]
        for r in range(NSA_GROUP):
            t = m * NSA_GROUP + r
            cols = slice(t * LANES, (t + 1) * LANES)
            rows = slice(r * Q_BLOCK, (r + 1) * Q_BLOCK)
            comb = (gexp[:, cols] * o_c[rows] + gexp[:, NSA_QD + t * LANES:NSA_QD + (t + 1) * LANES] * o_s[rows]
                    + gexp[:, 2 * NSA_QD + t * LANES:2 * NSA_QD + (t + 1) * LANES] * o_w[rows])
            o_ref[:, cols] += jnp.where(in_half, comb, 0.0)


def nsa_attn_prompt(q, gates, kcmp, vcmp, ks, vs, kw, vw, b, s):
    nq = s // Q_BLOCK
    rows = NSA_GROUP * Q_BLOCK
    slc = jnp.asarray(_nsa_slc_matrix(s // CMP_STRIDE, LANES))
    ex = jnp.asarray(_nsa_gate_expand())
    seq = pl.BlockSpec((s, NSA_KVD), lambda bi, i: (bi, 0))
    n_cmp = s // CMP_STRIDE
    cmp_spec = pl.BlockSpec((None, n_cmp, NSA_KVD), lambda bi, i: (bi, 0, 0))
    cmp_t_spec = pl.BlockSpec((None, NSA_KVD, n_cmp), lambda bi, i: (bi, 0, 0))

    def key_tiles(k, tile):
        kt = k.reshape(b, s // tile, tile, NSA_KVD).transpose(0, 1, 3, 2)
        return kt, pl.BlockSpec((None, s // tile, NSA_KVD, tile), lambda bi, i: (bi, 0, 0, 0))

    ks, ks_spec = key_tiles(ks, NSA_KV_TILE)
    kw, kw_spec = key_tiles(kw, Q_BLOCK)
    kcmp = kcmp.transpose(0, 2, 1)
    return pl.pallas_call(
        _nsa_attn_kernel,
        out_shape=jax.ShapeDtypeStruct((b * s, NSA_QD), F32),
        grid=(b, nq),
        in_specs=[
            pl.BlockSpec((Q_BLOCK, NSA_QD), lambda bi, i: (bi * nq + i, 0)),
            pl.BlockSpec((Q_BLOCK, LANES), lambda bi, i: (bi * nq + i, 0)),
            cmp_t_spec, cmp_spec, ks_spec, seq, kw_spec, seq,
            pl.BlockSpec(slc.shape, lambda bi, i: (0, 0)),
            pl.BlockSpec(ex.shape, lambda bi, i: (0, 0)),
        ],
        out_specs=pl.BlockSpec((Q_BLOCK, NSA_QD), lambda bi, i: (bi * nq + i, 0)),
        scratch_shapes=[pltpu.VMEM((NSA_KV_HEADS, rows, NSA_KVD), BF16),
                        pltpu.VMEM((NSA_KV_HEADS, Q_BLOCK, LANES), BF16),
                        pltpu.VMEM((NSA_KV_HEADS, rows, LANES), F32),
                        pltpu.VMEM((NSA_KV_HEADS, rows, 1), F32),
                        pltpu.VMEM((NSA_KV_HEADS, rows, 1), F32),
                        pltpu.VMEM((NSA_KV_HEADS, rows, NSA_KVD), F32)],
        compiler_params=_params("parallel", "arbitrary"),
        name="nsa_attn_prompt",
    )(q, gates, kcmp, vcmp, ks, vs, kw, vw, slc, ex)


PAGES_PER_STEP = 8
SUM_PAGES_PER_STEP = 16


def _feature_major(cache):
    n, tokens = cache.shape[:2]
    return cache.transpose(0, 2, 3, 1).reshape(n, NSA_KVD, tokens)


def _page_specs(n, page):
    def one(u):
        return pl.BlockSpec((None, NSA_KVD, page), lambda bi, j, pt: (pt[bi, j * n + u], 0, 0))
    return [one(u) for u in range(n)]


def _nsa_page_sums_kernel(pt_ref, *refs):
    n = SUM_PAGES_PER_STEP
    kp, vp = refs[:n], refs[n:2 * n]
    wk1_ref, wk2_ref, wv1_ref, wv2_ref, grp_ref, fk_ref, sk_ref, fv_ref, sv_ref = refs[2 * n:]

    def sums(pages, w_ref, o_ref):
        xw = jnp.concatenate([(p[...] * w_ref[...]).astype(BF16) for p in pages], axis=1)
        o_ref[...] = jnp.dot(xw, grp_ref[...], preferred_element_type=F32)

    sums(kp, wk1_ref, fk_ref)
    sums(kp, wk2_ref, sk_ref)
    sums(vp, wv1_ref, fv_ref)
    sums(vp, wv2_ref, sv_ref)


def nsa_page_sums(page_table, pool_k, pool_v, pe_k, pe_v):
    db, n_pages = page_table.shape
    page = pool_k.shape[2]
    n = SUM_PAGES_PER_STEP
    per_page = page // CMP_STRIDE
    assert n * per_page == LANES
    out = jax.ShapeDtypeStruct((db, NSA_KVD, n_pages * per_page), F32)
    out_spec = pl.BlockSpec((None, NSA_KVD, LANES), lambda bi, j, pt: (bi, 0, j))
    w_spec = pl.BlockSpec((NSA_KVD, page), lambda bi, j, pt: (0, 0))
    halves = lambda pe: [jnp.tile((1.0 + pe.reshape(CMP_LEN, NSA_KVD)[o:o + CMP_STRIDE]).T, (1, per_page))
                         for o in (0, CMP_STRIDE)]
    grp = np.zeros((n * page, LANES), np.float32)
    tok = np.arange(n * page)
    grp[tok, tok // CMP_STRIDE] = 1.0
    grp = jnp.asarray(grp, BF16)
    return pl.pallas_call(
        _nsa_page_sums_kernel,
        out_shape=(out,) * 4,
        grid_spec=pltpu.PrefetchScalarGridSpec(
            num_scalar_prefetch=1,
            grid=(db, n_pages // n),
            in_specs=_page_specs(n, page) * 2 + [w_spec] * 4 + [pl.BlockSpec(grp.shape, lambda bi, j, pt: (0, 0))],
            out_specs=(out_spec,) * 4,
        ),
        compiler_params=_params("parallel", "arbitrary"),
        name="nsa_page_sums",
    )(page_table, *([pool_k] * n), *([pool_v] * n), *halves(pe_k), *halves(pe_v), grp)


def _decode_queries(q_ref, g):
    m, half = divmod(g, 2)
    in_half = (lax.broadcasted_iota(jnp.int32, (1, LANES), 1) // NSA_HD) == half
    tiles = [_padded_query(q_ref[:, (m * NSA_GROUP + r) * LANES:(m * NSA_GROUP + r + 1) * LANES], m, in_half)
             for r in range(NSA_GROUP)]
    return jnp.concatenate(tiles, axis=0).astype(BF16)


def _group_rows(x):
    return jnp.concatenate([x] * NSA_GROUP, axis=0)


def _nsa_decode_select_kernel(past, q_ref, fk_ref, sk_ref, fv_ref, sv_ref, kn_ref, vn_ref, wk_ref, wv_ref,
                              slc_ref, sel_ref, oc_ref):
    nq = q_ref.shape[0]
    n_cmp = fk_ref.shape[1]
    col = lax.broadcasted_iota(jnp.int32, (1, n_cmp), 1)
    eye = (lax.broadcasted_iota(jnp.int32, (NSA_KVD, 1), 0)
           == lax.broadcasted_iota(jnp.int32, (1, NSA_KVD), 1))

    def summaries(f_ref, s_ref, new_ref, w_ref):
        w = 1.0 + w_ref[CMP_STRIDE:CMP_STRIDE + nq, :]
        second_new = jnp.sum(new_ref[...] * w, axis=0, keepdims=True)
        new_col = jnp.sum(jnp.where(eye, second_new, 0.0), axis=1, keepdims=True)
        nxt = jnp.where(col == n_cmp - 1, new_col, pltpu.roll(s_ref[...], n_cmp - 1, 1))
        return ((f_ref[...] + nxt) / CMP_LEN).astype(BF16)

    kcmp_t = summaries(fk_ref, sk_ref, kn_ref, wk_ref)
    vcmp_t = summaries(fv_ref, sv_ref, vn_ref, wv_ref)
    pos_q = past + lax.broadcasted_iota(jnp.int32, (nq, 1), 0)
    cmp_end = CMP_STRIDE * lax.broadcasted_iota(jnp.int32, (1, n_cmp), 1) + (CMP_LEN - 1)
    cmp_ok = _group_rows(cmp_end <= pos_q)
    lane128 = lax.broadcasted_iota(jnp.int32, (1, LANES), 1)
    oc_ref[...] = jnp.zeros_like(oc_ref)
    for g in range(NSA_KV_HEADS):
        m, half = divmod(g, 2)
        in_half = (lane128 // NSA_HD) == half
        s = jnp.dot(_decode_queries(q_ref, g), kcmp_t, preferred_element_type=F32)
        p = _softmax_rows(s, cmp_ok)
        o_c = lax.dot_general(p.astype(BF16), vcmp_t, NT,
                              preferred_element_type=F32)[:, m * LANES:(m + 1) * LANES]
        imp = p[0:nq]
        for r in range(1, NSA_GROUP):
            imp = imp + p[r * nq:(r + 1) * nq]
        for r in range(NSA_GROUP):
            cols = slice((m * NSA_GROUP + r) * LANES, (m * NSA_GROUP + r + 1) * LANES)
            oc_ref[:, cols] += jnp.where(in_half, o_c[r * nq:(r + 1) * nq], 0.0)
        p_slc = jnp.dot(imp, slc_ref[...], precision=HIGHEST, preferred_element_type=F32)
        sel = _group_rows(_select_blocks(_block_scores(p_slc, pos_q), N_SEL)).astype(BF16)
        rows = slice(g * NSA_GROUP * nq, (g + 1) * NSA_GROUP * nq)
        for lt in range(sel_ref.shape[0]):
            sel_ref[lt, rows, :] = sel[:, lt * LANES:(lt + 1) * LANES]


def nsa_decode_select(q, sums, kc_new, vc_new, pe_k, pe_v, db, nq, past):
    n_cmp = sums[0].shape[2]
    nsb_pad = _round_up(-(-(past + nq) // SEL_BLOCK), LANES)
    sel_rows = NSA_KV_HEADS * NSA_GROUP * nq
    slc = jnp.asarray(_nsa_slc_matrix(n_cmp, nsb_pad))
    cmp_spec = pl.BlockSpec((None, NSA_KVD, n_cmp), lambda bi: (bi, 0, 0))
    new_spec = pl.BlockSpec((nq, NSA_KVD), lambda bi: (bi, 0))
    pe_spec = pl.BlockSpec((CMP_LEN, NSA_KVD), lambda bi: (0, 0))
    return pl.pallas_call(
        functools.partial(_nsa_decode_select_kernel, past),
        out_shape=(jax.ShapeDtypeStruct((db, nsb_pad // LANES, sel_rows, LANES), BF16),
                   jax.ShapeDtypeStruct((db * nq, NSA_QD), F32)),
        grid=(db,),
        in_specs=[pl.BlockSpec((nq, NSA_QD), lambda bi: (bi, 0)), cmp_spec, cmp_spec, cmp_spec, cmp_spec,
                  new_spec, new_spec, pe_spec, pe_spec, pl.BlockSpec(slc.shape, lambda bi: (0, 0))],
        out_specs=(pl.BlockSpec((None, nsb_pad // LANES, sel_rows, LANES), lambda bi: (bi, 0, 0, 0)),
                   pl.BlockSpec((nq, NSA_QD), lambda bi: (bi, 0))),
        compiler_params=_params("parallel"),
        name="nsa_decode_select",
    )(q, *sums, kc_new, vc_new, pe_k.reshape(CMP_LEN, NSA_KVD), pe_v.reshape(CMP_LEN, NSA_KVD), slc)


def _nsa_decode_attend_kernel(past, pt_ref, q_ref, sel_ref, g_ref, oc_ref, ksn_ref, vsn_ref, wk_ref, wv_ref,
                              kwn_ref, vwn_ref, ex_ref, xp_ref, *refs):
    n = PAGES_PER_STEP
    kp, vp = refs[:n], refs[n:2 * n]
    o_ref, qz_scr, m_scr, l_scr, acc_scr = refs[2 * n:]
    j = pl.program_id(1)
    nq = q_ref.shape[0]
    n_keys = n * kp[0].shape[1]
    steps_per_tile = xp_ref.shape[0]
    pos_q = jnp.concatenate([past + lax.broadcasted_iota(jnp.int32, (nq, 1), 0)] * (NSA_KV_HEADS * NSA_GROUP),
                            axis=0)

    @pl.when(j == 0)
    def _():
        m_scr[...] = jnp.full_like(m_scr, -jnp.inf)
        l_scr[...] = jnp.zeros_like(l_scr)
        acc_scr[...] = jnp.zeros_like(acc_scr)
        qz_scr[...] = jnp.concatenate([_decode_queries(q_ref, g) for g in range(NSA_KV_HEADS)], axis=0)

    def attend(s, allowed, v, feature_major):
        m_scr[...], l_scr[...], acc_scr[...] = _online_softmax_step(
            s, allowed, v, m_scr[...], l_scr[...], acc_scr[...], feature_major)

    k_t = jnp.concatenate([r[...].astype(BF16) for r in kp], axis=1)
    v_t = jnp.concatenate([r[...].astype(BF16) for r in vp], axis=1)
    key = j * n_keys + lax.broadcasted_iota(jnp.int32, (1, n_keys), 1)
    chosen = jnp.dot(sel_ref[j // steps_per_tile], xp_ref[j % steps_per_tile], preferred_element_type=F32)
    attend(jnp.dot(qz_scr[...], k_t, preferred_element_type=F32), (chosen > 0.5) & (key <= pos_q), v_t, True)

    @pl.when(j == pl.num_programs(1) - 1)
    def _():
        qz = qz_scr[...]
        lane128 = lax.broadcasted_iota(jnp.int32, (1, LANES), 1)
        n_new = ksn_ref.shape[0]
        new_idx = lax.broadcasted_iota(jnp.int32, (1, n_new), 1)
        new_key = past + new_idx
        new_blk = past // SEL_BLOCK
        chosen_new = sel_ref[new_blk // LANES][:, new_blk % LANES:new_blk % LANES + 1].astype(F32) > 0.5
        attend(lax.dot_general(qz, ksn_ref[...], NT, preferred_element_type=F32),
               chosen_new & (new_key <= pos_q) & (new_idx < nq), vsn_ref[...], False)
        o_s = acc_scr[...] / jnp.maximum(l_scr[...], 1e-30)
        n_win = wk_ref.shape[1]
        k_wc, v_wc = wk_ref[...].astype(BF16), wv_ref[...].astype(BF16)
        pos_wc = past - n_win + lax.broadcasted_iota(jnp.int32, (1, n_win), 1)
        ok_wc = (pos_q - pos_wc >= 0) & (pos_q - pos_wc < WINDOW) & (pos_wc >= 0)
        ok_wn = (pos_q - new_key >= 0) & (pos_q - new_key < WINDOW) & (new_idx < nq)
        win = (jnp.full(m_scr.shape, -jnp.inf, F32), jnp.zeros(l_scr.shape, F32), jnp.zeros(acc_scr.shape, F32))
        win = _online_softmax_step(jnp.dot(qz, k_wc, preferred_element_type=F32), ok_wc, v_wc, *win, True)
        win = _online_softmax_step(lax.dot_general(qz, kwn_ref[...], NT, preferred_element_type=F32), ok_wn,
                                   vwn_ref[...], *win, False)
        o_w = win[2] / jnp.maximum(win[1], 1e-30)
        gexp = jnp.dot(g_ref[...], ex_ref[...], precision=HIGHEST, preferred_element_type=F32)
        o_ref[...] = gexp[:, :NSA_QD] * oc_ref[...]
        for g in range(NSA_KV_HEADS):
            m, half = divmod(g, 2)
            in_half = (lane128 // NSA_HD) == half
            mcols = slice(m * LANES, (m + 1) * LANES)
            for r in range(NSA_GROUP):
                t = m * NSA_GROUP + r
                cols = slice(t * LANES, (t + 1) * LANES)
                rows = slice((g * NSA_GROUP + r) * nq, (g * NSA_GROUP + r + 1) * nq)
                comb = (gexp[:, NSA_QD + t * LANES:NSA_QD + (t + 1) * LANES] * o_s[rows, mcols]
                        + gexp[:, 2 * NSA_QD + t * LANES:2 * NSA_QD + (t + 1) * LANES] * o_w[rows, mcols])
                o_ref[:, cols] += jnp.where(in_half, comb, 0.0)


def nsa_decode_attend(page_table, q, sel, gates, o_c, ks_new, vs_new, wk, wv, kw_new, vw_new, pool_k, pool_v,
                      db, nq, past):
    n_pages = page_table.shape[1]
    page = pool_k.shape[2]
    ex = jnp.asarray(_nsa_gate_expand())
    n_win = wk.shape[2]
    rows = NSA_KV_HEADS * NSA_GROUP * nq
    n_keys = PAGES_PER_STEP * page
    assert past % SEL_BLOCK + nq <= SEL_BLOCK and (LANES * SEL_BLOCK) % n_keys == 0
    steps_per_tile = LANES * SEL_BLOCK // n_keys
    xp = np.zeros((steps_per_tile, LANES, n_keys), np.float32)
    keys = np.arange(n_keys)
    for u in range(steps_per_tile):
        xp[u, u * (n_keys // SEL_BLOCK) + keys // SEL_BLOCK, keys] = 1.0
    xp = jnp.asarray(xp, BF16)
    per_b = lambda shape: pl.BlockSpec((None,) + shape, lambda bi, j, pt: (bi,) + (0,) * len(shape))
    q_rows = lambda width: pl.BlockSpec((nq, width), lambda bi, j, pt: (bi, 0))
    new_rows = ks_new.shape[1]
    return pl.pallas_call(
        functools.partial(_nsa_decode_attend_kernel, past),
        out_shape=jax.ShapeDtypeStruct((db * nq, NSA_QD), F32),
        grid_spec=pltpu.PrefetchScalarGridSpec(
            num_scalar_prefetch=1,
            grid=(db, n_pages // PAGES_PER_STEP),
            in_specs=[q_rows(NSA_QD), per_b(sel.shape[1:]), q_rows(LANES), q_rows(NSA_QD),
                      per_b((new_rows, NSA_KVD)), per_b((new_rows, NSA_KVD)),
                      per_b((NSA_KVD, n_win)), per_b((NSA_KVD, n_win)),
                      per_b((new_rows, NSA_KVD)), per_b((new_rows, NSA_KVD)),
                      pl.BlockSpec(ex.shape, lambda bi, j, pt: (0, 0)),
                      pl.BlockSpec(xp.shape, lambda bi, j, pt: (0, 0, 0))]
            + _page_specs(PAGES_PER_STEP, page) * 2,
            out_specs=q_rows(NSA_QD),
            scratch_shapes=[pltpu.VMEM((rows, NSA_KVD), BF16), pltpu.VMEM((rows, 1), F32),
                            pltpu.VMEM((rows, 1), F32), pltpu.VMEM((rows, NSA_KVD), F32)],
        ),
        compiler_params=_params("parallel", "arbitrary"),
        name="nsa_decode_attend",
    )(page_table, q, sel, gates, o_c, ks_new, vs_new, wk, wv, kw_new, vw_new, ex, xp,
      *([pool_k] * PAGES_PER_STEP), *([pool_v] * PAGES_PER_STEP))


def _nsa_weights(w_in, w_out):
    perm = _nsa_q_perm()
    return jnp.concatenate([w_in[:, perm], w_in[:, NSA_QD:]], axis=1), w_out[perm]


def nsa_prompt(x, scale, shift, w_in_p, pe_k, pe_v, b, s):
    p, _ = project(x, scale, shift, w_in_p, b, s)
    cos, up, dn = _rope_tables(jnp.arange(s, dtype=jnp.int32))
    q, kc, vc, ks, vs, kw, vw, ksb, vsb, kwb, vwb, gates = nsa_prep(p, cos, up, dn, q_dtype=BF16)
    kcmp = nsa_compress(kc, pe_k, b, s)
    vcmp = nsa_compress(vc, pe_v, b, s)
    o = nsa_attn_prompt(q, gates, kcmp, vcmp, ksb, vsb, kwb, vwb, b, s)
    nw = min(WINDOW, s)
    rs = lambda a: a.reshape(b, s, NSA_KV_HEADS, NSA_HD)
    return o, (rs(kc), rs(vc), rs(ks), rs(vs), rs(kw)[:, s - nw:], rs(vw)[:, s - nw:])


def nsa_sample(x, scale, shift, ck, cv, sk, sv, wk, wv, page_table, w_in_p, pe_k, pe_v, db, nq):
    past = page_table.shape[1] * ck.shape[1]
    p, _ = project(x, scale, shift, w_in_p, db, nq)
    pos = past + jnp.tile(jnp.arange(nq, dtype=jnp.int32), db)
    cos, up, dn = _rope_tables(pos)
    q, kc, vc, ks, vs, kw, vw, ksb, vsb, kwb, vwb, gates = nsa_prep(p, cos, up, dn, q_dtype=F32)
    sums = nsa_page_sums(page_table, _feature_major(ck), _feature_major(cv), pe_k, pe_v)
    sel, o_c = nsa_decode_select(q, sums, kc, vc, pe_k, pe_v, db, nq, past)
    new_pad = lambda a: jnp.pad(a.reshape(db, nq, NSA_KVD), ((0, 0), (0, LANES - nq), (0, 0)))
    o = nsa_decode_attend(page_table, q, sel, gates, o_c, new_pad(ksb), new_pad(vsb),
                          _feature_major(wk), _feature_major(wv), new_pad(kwb), new_pad(vwb),
                          _feature_major(sk), _feature_major(sv), db, nq, past)
    rs = lambda a: a.reshape(db, nq, NSA_KV_HEADS, NSA_HD)
    slide = lambda cache, new: jnp.concatenate([cache, rs(new).astype(cache.dtype)], 1)[:, nq:]
    return o, (rs(kc), rs(vc), rs(ks), rs(vs), slide(wk, kw), slide(wv, vw))


def _top_values(s, k):
    vals = []
    for _ in range(k):
        m = jnp.max(s, axis=0, keepdims=True)
        vals.append(m)
        s = jnp.where(s >= m, -jnp.inf, s)
    return vals


def _peer_route_kernel(qv_ref, keys_ref, s1_ref, s2_ref, e1_ref, e2_ref, th_ref):
    for h in range(PEER_HEADS):
        q1 = qv_ref[:, (2 * h) * PEER_DKEY:(2 * h + 1) * PEER_DKEY]
        q2 = qv_ref[:, (2 * h + 1) * PEER_DKEY:(2 * h + 2) * PEER_DKEY]
        s1 = lax.dot_general(keys_ref[h, 0], q1, NT, preferred_element_type=F32)
        s2 = lax.dot_general(keys_ref[h, 1], q2, NT, preferred_element_type=F32)
        n_rank = PEER_TOPK + 1
        top1 = _top_values(s1, n_rank)
        top2 = _top_values(s2, n_rank)
        t1 = jnp.concatenate(top1[:PEER_TOPK], axis=0)
        t2 = jnp.concatenate(top2[:PEER_TOPK], axis=0)
        row8 = lax.broadcasted_iota(jnp.int32, (SUBLANES, 1), 0)
        ends = jnp.where(row8 == 0, top1[0] + top2[PEER_TOPK],
                         jnp.where(row8 == 1, top1[PEER_TOPK] + top2[0], -jnp.inf))
        pairs = [top1[0] + t2, top1[1] + t2[:SUBLANES], t1[SUBLANES:] + top2[0], ends]
        for i in range(2, SUBLANES):
            pairs.append(jnp.where(row8 < n_rank // (i + 1), top1[i] + t2[:SUBLANES], -jnp.inf))
        best = _top_values(jnp.concatenate(pairs, axis=0), n_rank)
        z = best[0] * 0.0
        for v in best[:PEER_TOPK]:
            z = z + jnp.exp(v - best[0])
        cut = 0.5 * (best[PEER_TOPK - 1] + best[PEER_TOPK])
        e1 = jnp.exp(s1 - top1[0])
        e2 = jnp.exp(s2 - top2[0]) / z
        for c in range(qv_ref.shape[0] // LANES):
            cols = slice(c * LANES, (c + 1) * LANES)
            s1_ref[h, c] = s1[:, cols]
            s2_ref[h, c] = s2[:, cols]
            e1_ref[h, c] = e1[:, cols]
            e2_ref[h, c] = e2[:, cols]
            th_ref[h, c] = cut[:, cols]


def peer_route(qv, keys, *, tm):
    t = qv.shape[0]
    nc = tm // LANES
    big = jax.ShapeDtypeStruct((PEER_HEADS, t // LANES, PEER_NKEYS, LANES), F32)
    big_spec = pl.BlockSpec((PEER_HEADS, nc, PEER_NKEYS, LANES), lambda i: (0, i, 0, 0))
    return pl.pallas_call(
        _peer_route_kernel,
        out_shape=(big, big, big, big, jax.ShapeDtypeStruct((PEER_HEADS, t // LANES, 1, LANES), F32)),
        grid=(t // tm,),
        in_specs=[
            pl.BlockSpec((tm, 2 * PEER_HEADS * PEER_DKEY), lambda i: (i, 0)),
            pl.BlockSpec(keys.shape, lambda i: (0, 0, 0, 0)),
        ],
        out_specs=(big_spec, big_spec, big_spec, big_spec,
                   pl.BlockSpec((PEER_HEADS, nc, 1, LANES), lambda i: (0, i, 0, 0))),
        compiler_params=_params("parallel"),
        name="peer_route",
    )(qv, keys)


PEER_A_PER_STEP = 2 * SUBLANES
PEER_ROUTE_TILE = 256


def _gelu_tanh(x):
    return 0.5 * x * (1.0 + jnp.tanh(math.sqrt(2.0 / math.pi) * (x + 0.044715 * (x * x * x))))


def _peer_dense_kernel(h_ref, s1_ref, e1_ref, s2_ref, e2_ref, th_ref, u_ref, vt_ref, o_ref,
                       act_ref, g_ref, acc_ref):
    j = pl.program_id(1)
    tm = h_ref.shape[0]

    @pl.when(j == 0)
    def _():
        acc_ref[...] = jnp.zeros_like(acc_ref)

    nc = tm // LANES
    act = lax.dot_general(u_ref[...], h_ref[...], NT, preferred_element_type=F32)
    for c in range(nc):
        act_ref[c] = act[:, c * LANES:(c + 1) * LANES]

    def tile(idx, carry):
        a, c = idx // nc, idx % nc
        rows = pl.ds(pl.multiple_of(a * PEER_NKEYS, PEER_NKEYS), PEER_NKEYS)
        w = jnp.zeros((PEER_NKEYS, LANES), F32)
        for h in range(PEER_HEADS):
            need = th_ref[h, c] - s1_ref[h, c, pl.ds(a, 1), :]
            w = w + e1_ref[h, c, pl.ds(a, 1), :] * jnp.where(s2_ref[h, c] >= need, e2_ref[h, c], 0.0)
        g_ref[c, rows, :] = (w * _gelu_tanh(act_ref[c, rows, :])).astype(BF16)
        return carry

    lax.fori_loop(0, PEER_A_PER_STEP * nc, tile, 0)
    g = jnp.concatenate([g_ref[c] for c in range(nc)], axis=1)
    acc_ref[...] += jnp.dot(vt_ref[...], g, preferred_element_type=F32)

    @pl.when(j == pl.num_programs(1) - 1)
    def _():
        o_ref[...] = acc_ref[...].T


def peer_dense(h, s1, s2, e1, e2, th, u, vt, *, tm):
    t, d = h.shape
    ne = PEER_A_PER_STEP * PEER_NKEYS
    nc = tm // LANES
    tok = pl.BlockSpec((PEER_HEADS, nc, PEER_NKEYS, LANES), lambda i, j: (0, i, 0, 0))
    arow = pl.BlockSpec((PEER_HEADS, nc, PEER_A_PER_STEP, LANES), lambda i, j: (0, i, j, 0))
    return pl.pallas_call(
        _peer_dense_kernel,
        out_shape=jax.ShapeDtypeStruct((t, d), F32),
        grid=(t // tm, PEER_NKEYS // PEER_A_PER_STEP),
        in_specs=[
            pl.BlockSpec((tm, d), lambda i, j: (i, 0)),
            arow, arow, tok, tok,
            pl.BlockSpec((PEER_HEADS, nc, 1, LANES), lambda i, j: (0, i, 0, 0)),
            pl.BlockSpec((ne, d), lambda i, j: (j, 0)),
            pl.BlockSpec((d, ne), lambda i, j: (0, j)),
        ],
        out_specs=pl.BlockSpec((tm, d), lambda i, j: (i, 0)),
        scratch_shapes=[pltpu.VMEM((nc, ne, LANES), F32), pltpu.VMEM((nc, ne, LANES), BF16),
                        pltpu.VMEM((d, tm), F32)],
        compiler_params=_params("parallel", "arbitrary"),
        name="peer_dense",
    )(h, s1, e1, s2, e2, th, u, vt)


def _peer_tables(u, v):
    return u.astype(BF16), v.astype(BF16).T


def peer_ffn(x, scale, shift, wq, keys, u_bf, vt_bf, b, s):
    qv, h = project(x, scale, shift, wq, b, s)
    tm = _row_tile(b * s)
    s1, s2, e1, e2, th = peer_route(qv, keys, tm=min(tm, PEER_ROUTE_TILE))
    return peer_dense(h, s1, s2, e1, e2, th, u_bf, vt_bf, tm=tm)


def kernel(x_prompt, x_sample, state_gla, cache_cmp_k, cache_cmp_v, cache_sel_k, cache_sel_v, cache_win_k, cache_win_v, page_table, c_prompt, c_sample, ada_w, ada_b, ln_g, ln_b, gla_w_in, gla_w_a2, gla_b_a2, gla_gn, gla_w_out, nsa_w_in, nsa_pe_k, nsa_pe_v, nsa_w_out, peer_wq, peer_keys, peer_u, peer_v):
    bp, sp, d = x_prompt.shape
    bs, ss, _ = x_sample.shape
    groups = ((bp, sp), (bs, ss))
    c_all = jnp.concatenate([c_prompt, c_sample], axis=0)
    c_rows = _round_up(bp + bs, SUBLANES)
    mod = ada_mod(jnp.pad(c_all, ((0, c_rows - bp - bs), (0, 0))), ada_w, ada_b)
    ys = [x_prompt.reshape(bp * sp, d), x_sample.reshape(bs * ss, d)]
    gla_states, nsa_rows = ([], []), ([], [])
    for i in range(DEPTH):
        row0 = (0, bp)
        mods = [[mod[i, row0[n]:row0[n] + b, k * d:(k + 1) * d][:, None, :] for k in range(6)]
                for n, (b, _) in enumerate(groups)]
        j = i // N_MIXERS
        if i % N_MIXERS == 0:
            for n, (b, s) in enumerate(groups):
                shift, scale, gate = mods[n][:3]
                proj, _ = project(ys[n], scale, shift, gla_w_in[j], b, s)
                o, st = gla_recurrence(proj, gla_w_a2[j], gla_b_a2[j], None if n == 0 else state_gla[j], b, s)
                gla_states[n].append(st.astype(state_gla.dtype))
                ys[n] = post(ys[n], gate, ln_g[i, 0], ln_b[i, 0], b, s, sub=o, w_out=gla_w_out[j],
                             gla_proj=proj, gla_gn=gla_gn[j])
        else:
            w_in_p, w_out_p = _nsa_weights(nsa_w_in[j], nsa_w_out[j])
            for n, (b, s) in enumerate(groups):
                shift, scale, gate = mods[n][:3]
                if n == 0:
                    o, rows = nsa_prompt(ys[n], scale, shift, w_in_p, nsa_pe_k[j], nsa_pe_v[j], b, s)
                else:
                    o, rows = nsa_sample(ys[n], scale, shift, cache_cmp_k[j], cache_cmp_v[j], cache_sel_k[j],
                                         cache_sel_v[j], cache_win_k[j], cache_win_v[j], page_table,
                                         w_in_p, nsa_pe_k[j], nsa_pe_v[j], b, s)
                nsa_rows[n].append(rows)
                ys[n] = post(ys[n], gate, ln_g[i, 0], ln_b[i, 0], b, s, sub=o, w_out=w_out_p)
        u4, vt4 = _peer_tables(peer_u[i], peer_v[i])
        for n, (b, s) in enumerate(groups):
            shift, scale, gate = mods[n][3:]
            f = peer_ffn(ys[n], scale, shift, peer_wq[i], peer_keys[i], u4, vt4, b, s)
            ys[n] = post(ys[n], gate, ln_g[i, 1], ln_b[i, 1], b, s, sub=f)

    st = lambda ts, k: jnp.stack([t[k] for t in ts])
    return (ys[0].reshape(bp, sp, d), ys[1].reshape(bs, ss, d),
            jnp.stack(gla_states[0]), jnp.stack(gla_states[1]),
            *(st(nsa_rows[0], k) for k in range(6)), *(st(nsa_rows[1], k) for k in range(6)))
```

```python
import functools
import math

import jax
import jax.numpy as jnp
import numpy as np
from jax import lax
from jax.experimental import pallas as pl
from jax.experimental.pallas import tpu as pltpu

D_MODEL = 1024
DEPTH = 2
N_MIXERS = 2
DN_ALPHA = (2.0 * DEPTH) ** 0.25
LN_EPS = 1e-5
F32 = jnp.float32
BF16 = jnp.bfloat16
HIGHEST = lax.Precision.HIGHEST

GLA_HEADS = 4
GLA_DK = D_MODEL // 2 // GLA_HEADS
GLA_DV = D_MODEL // GLA_HEADS
GLA_TAU = 16.0
GLA_CHUNK = 64
GLA_HK = GLA_HEADS * GLA_DK
GLA_HV = GLA_HEADS * GLA_DV

NSA_HEADS = 16
NSA_KV_HEADS = 4
NSA_GROUP = NSA_HEADS // NSA_KV_HEADS
NSA_HD = D_MODEL // NSA_HEADS
NSA_QD = NSA_HEADS * NSA_HD
NSA_KVD = NSA_KV_HEADS * NSA_HD
CMP_LEN = 32
CMP_STRIDE = 16
SEL_BLOCK = 64
N_SEL = 16
WINDOW = 512
Q_BLOCK = 128
FORCE_SCORE = 1e6
ROT_DIM = NSA_HD // 4
ROPE_THETA = 500000.0

PEER_HEADS = 8
PEER_NKEYS = 128
PEER_DKEY = 128
PEER_TOPK = 16

LANES = 128
SUBLANES = 8
VMEM_LIMIT_BYTES = 56 * 1024 * 1024
ROW_TILE = 512

NT = (((1,), (1,)), ((), ()))
TN = (((0,), (0,)), ((), ()))


def _round_up(n, m):
    return -(-n // m) * m


def _params(*sem):
    return pltpu.CompilerParams(dimension_semantics=sem, vmem_limit_bytes=VMEM_LIMIT_BYTES)


def _row_tile(t):
    return ROW_TILE if t % ROW_TILE == 0 else t


def _mod_operand(m, b, s, tm):
    d = m.shape[-1]
    if s % tm == 0:
        return m, pl.BlockSpec((None, 1, d), lambda i, *_: (i * tm // s, 0, 0))
    rows = jnp.broadcast_to(m, (b, s, d)).reshape(b * s, d)
    return rows, pl.BlockSpec((tm, d), lambda i, *_: (i, 0))


def _ada_kernel(c_ref, w_ref, b_ref, o_ref):
    c = c_ref[...]
    act = (c * jax.nn.sigmoid(c)).astype(BF16)
    o_ref[...] = jnp.dot(act, w_ref[...].astype(BF16), preferred_element_type=F32) + b_ref[...]


def ada_mod(c, w, b):
    r, d = c.shape
    nl, _, n = w.shape
    tn = n // 4
    return pl.pallas_call(
        _ada_kernel,
        out_shape=jax.ShapeDtypeStruct((nl, r, n), F32),
        grid=(nl, n // tn),
        in_specs=[pl.BlockSpec((r, d), lambda l, j: (0, 0)),
                  pl.BlockSpec((None, d, tn), lambda l, j: (l, 0, j)),
                  pl.BlockSpec((None, 1, tn), lambda l, j: (l, 0, j))],
        out_specs=pl.BlockSpec((None, r, tn), lambda l, j: (l, 0, j)),
        compiler_params=_params("parallel", "parallel"),
        name="ada_mod",
    )(c, w, b.reshape(nl, 1, n))


def _mod_matmul_kernel(x_ref, sc_ref, sh_ref, w_ref, o_ref, h_ref):
    @pl.when(pl.program_id(1) == 0)
    def _():
        h = x_ref[...] * (1.0 + sc_ref[...]) + sh_ref[...]
        h_ref[...] = h.astype(BF16)

    o_ref[...] = jnp.dot(h_ref[...], w_ref[...], preferred_element_type=F32)


def project(x, scale, shift, w, b, s):
    t, d = x.shape
    npad = _round_up(w.shape[1], LANES)
    wp = jnp.pad(w, ((0, 0), (0, npad - w.shape[1]))).astype(BF16)
    tn = npad
    tm = _row_tile(t)
    sc, mod_spec = _mod_operand(scale, b, s, tm)
    sh, _ = _mod_operand(shift, b, s, tm)
    return pl.pallas_call(
        _mod_matmul_kernel,
        out_shape=(jax.ShapeDtypeStruct((t, npad), F32), jax.ShapeDtypeStruct((t, d), BF16)),
        grid=(t // tm, npad // tn),
        in_specs=[pl.BlockSpec((tm, d), lambda i, j: (i, 0)), mod_spec, mod_spec,
                  pl.BlockSpec((d, tn), lambda i, j: (0, j))],
        out_specs=(pl.BlockSpec((tm, tn), lambda i, j: (i, j)), pl.BlockSpec((tm, d), lambda i, j: (i, 0))),
        compiler_params=_params("parallel", "arbitrary"),
        name="mod_matmul",
    )(x, sc, sh, wp)


def _deepnorm(x, sub, gate, g, b):
    y = DN_ALPHA * x + (1.0 + gate) * sub
    mu = jnp.mean(y, axis=-1, keepdims=True)
    var = jnp.mean(jnp.square(y - mu), axis=-1, keepdims=True)
    return (y - mu) * lax.rsqrt(var + LN_EPS) * g + b


def _post_gla_kernel(o_ref, r_ref, x_ref, gate_ref, gn_ref, w_ref, g_ref, b_ref, y_ref):
    parts = []
    for h in range(GLA_HEADS):
        cols = slice(h * GLA_DV, (h + 1) * GLA_DV)
        o = o_ref[:, cols]
        mu = jnp.mean(o, axis=-1, keepdims=True)
        var = jnp.mean(jnp.square(o - mu), axis=-1, keepdims=True)
        parts.append((o - mu) * lax.rsqrt(var + LN_EPS) * gn_ref[:, cols])
    r = r_ref[...]
    f = (jnp.concatenate(parts, axis=1) * (r * jax.nn.sigmoid(r))).astype(BF16)
    sub = jnp.dot(f, w_ref[...], preferred_element_type=F32)
    y_ref[...] = _deepnorm(x_ref[...], sub, gate_ref[...], g_ref[...], b_ref[...])


def _post_matmul_kernel(o_ref, x_ref, gate_ref, w_ref, g_ref, b_ref, y_ref):
    sub = jnp.dot(o_ref[...].astype(BF16), w_ref[...], preferred_element_type=F32)
    y_ref[...] = _deepnorm(x_ref[...], sub, gate_ref[...], g_ref[...], b_ref[...])


def _post_plain_kernel(o_ref, x_ref, gate_ref, g_ref, b_ref, y_ref):
    y_ref[...] = _deepnorm(x_ref[...], o_ref[...], gate_ref[...], g_ref[...], b_ref[...])


def post(x, gate, ln_g, ln_b, b, s, *, sub=None, w_out=None, gla_proj=None, gla_gn=None):
    t, d = x.shape
    tm = _row_tile(t)
    gate_arr, gate_spec = _mod_operand(gate, b, s, tm)
    row = pl.BlockSpec((tm, d), lambda i: (i, 0))
    vec = pl.BlockSpec((1, d), lambda i: (0, 0))
    mat = pl.BlockSpec((d, d), lambda i: (0, 0))
    g2, b2 = ln_g.reshape(1, d), ln_b.reshape(1, d)
    if gla_proj is not None:
        r_spec = pl.BlockSpec((tm, GLA_HV), lambda i: (i, (2 * GLA_HK + GLA_HV) // GLA_HV))
        args = (sub, gla_proj, x, gate_arr, gla_gn.reshape(1, d), w_out.astype(BF16), g2, b2)
        specs = [row, r_spec, row, gate_spec, vec, mat, vec, vec]
        body = _post_gla_kernel
    elif w_out is not None:
        args = (sub, x, gate_arr, w_out.astype(BF16), g2, b2)
        specs = [row, row, gate_spec, mat, vec, vec]
        body = _post_matmul_kernel
    else:
        args = (sub, x, gate_arr, g2, b2)
        specs = [row, row, gate_spec, vec, vec]
        body = _post_plain_kernel
    return pl.pallas_call(
        body,
        out_shape=jax.ShapeDtypeStruct((t, d), F32),
        grid=(t // tm,),
        in_specs=specs,
        out_specs=row,
        compiler_params=_params("parallel"),
        name="post",
    )(*args)


GLA_MIN_ROWS = 64


def _cumsum_rows(x):
    n = x.shape[0]
    row = lax.broadcasted_iota(jnp.int32, (n, 1), 0)
    shift = 1
    while shift < n:
        x = x + jnp.where(row >= shift, pltpu.roll(x, shift, 0), 0.0)
        shift *= 2
    return x


def _gla_kernel(has_s0, q_ref, k_ref, v_ref, a_ref, wa_ref, ba_ref, *rest):
    if has_s0:
        s0_ref, o_ref, st_ref, s_scr = rest
    else:
        o_ref, st_ref, s_scr = rest
    c = pl.program_id(1)
    n_rows = q_ref.shape[0]
    rows = max(n_rows, GLA_MIN_ROWS)

    @pl.when(c == 0)
    def _():
        s_scr[...] = s0_ref[...] if has_s0 else jnp.zeros_like(s_scr)

    def padded(x):
        if rows == n_rows:
            return x
        return jnp.concatenate([x, jnp.zeros((rows - n_rows, x.shape[1]), x.dtype)], axis=0)

    z = jnp.dot(a_ref[...], wa_ref[...], precision=HIGHEST, preferred_element_type=F32) + ba_ref[...]
    logg = (jnp.minimum(z, 0.0) - jnp.log1p(jnp.exp(-jnp.abs(z)))) / GLA_TAU
    b = _cumsum_rows(padded(logg))
    q, k, v = padded(q_ref[...]), padded(k_ref[...]), padded(v_ref[...])
    qe = q * (GLA_DK ** -0.5) * jnp.exp(b)
    ke = k * jnp.exp(-b)
    b_last = b[rows - 1:rows, :]
    kd = k * jnp.exp(b_last - b)
    e_last = jnp.exp(b_last)
    causal = lax.broadcasted_iota(jnp.int32, (rows, 1), 0) >= lax.broadcasted_iota(jnp.int32, (1, rows), 1)
    eye = lax.broadcasted_iota(jnp.int32, (GLA_DK, 1), 0) == lax.broadcasted_iota(jnp.int32, (1, GLA_DK), 1)
    for h in range(GLA_HEADS):
        ck = slice(h * GLA_DK, (h + 1) * GLA_DK)
        cv = slice(h * GLA_DV, (h + 1) * GLA_DV)
        att = lax.dot_general(qe[:, ck], ke[:, ck], NT, preferred_element_type=F32)
        att = jnp.where(causal, att, 0.0)
        s_h = s_scr[h]
        o = (jnp.dot(att, v[:, cv], preferred_element_type=F32)
             + jnp.dot(qe[:, ck], s_h, preferred_element_type=F32))
        o_ref[:, cv] = o[:n_rows]
        e_col = jnp.sum(jnp.where(eye, e_last[:, ck], 0.0), axis=1, keepdims=True)
        s_scr[h] = e_col * s_h + lax.dot_general(kd[:, ck], v[:, cv], TN, preferred_element_type=F32)

    @pl.when(c == pl.num_programs(1) - 1)
    def _():
        st_ref[...] = s_scr[...]


def gla_recurrence(proj, w_a2, b_a2, s0, b, s):
    chunk = GLA_CHUNK if s % GLA_CHUNK == 0 else s
    nc = s // chunk
    wa = jnp.pad(w_a2, ((0, LANES - w_a2.shape[0]), (0, 0)))
    a_block = (2 * GLA_HK + 2 * GLA_HV) // LANES
    row = lambda width, blk: pl.BlockSpec((chunk, width), lambda bi, c: (bi * nc + c, blk))
    st_spec = pl.BlockSpec((None, GLA_HEADS, GLA_DK, GLA_DV), lambda bi, c: (bi, 0, 0, 0))
    in_specs = [row(GLA_HK, 0), row(GLA_HK, 1), row(GLA_HV, 2 * GLA_HK // GLA_HV), row(LANES, a_block),
                pl.BlockSpec(wa.shape, lambda bi, c: (0, 0)), pl.BlockSpec((1, GLA_HK), lambda bi, c: (0, 0))]
    args = [proj, proj, proj, proj, wa, b_a2.reshape(1, GLA_HK)]
    if s0 is not None:
        in_specs.append(st_spec)
        args.append(s0)
    return pl.pallas_call(
        functools.partial(_gla_kernel, s0 is not None),
        out_shape=(jax.ShapeDtypeStruct((b * s, GLA_HV), F32),
                   jax.ShapeDtypeStruct((b, GLA_HEADS, GLA_DK, GLA_DV), F32)),
        grid=(b, nc),
        in_specs=in_specs,
        out_specs=(pl.BlockSpec((chunk, GLA_HV), lambda bi, c: (bi * nc + c, 0)), st_spec),
        scratch_shapes=[pltpu.VMEM((GLA_HEADS, GLA_DK, GLA_DV), F32)],
        compiler_params=_params("parallel", "arbitrary"),
        name="gla_recurrence",
    )(*args)


def _nsa_q_perm():
    cols = []
    for m in range(NSA_KV_HEADS // 2):
        for r in range(NSA_GROUP):
            for half in range(2):
                head = (2 * m + half) * NSA_GROUP + r
                cols.append(np.arange(NSA_HD) + head * NSA_HD)
    return np.concatenate(cols)


def _nsa_gate_expand():
    perm = _nsa_q_perm()
    ex = np.zeros((LANES, 3 * NSA_QD), np.float32)
    for br in range(3):
        for col in range(NSA_QD):
            ex[br * NSA_HEADS + perm[col] // NSA_HD, br * NSA_QD + col] = 1.0
    return ex


def _nsa_slc_matrix(nc_pad, nsb_pad):
    ratio = SEL_BLOCK // CMP_STRIDE
    m = np.zeros((nc_pad, nsb_pad), np.float32)
    for j in range(nsb_pad):
        for o in range(CMP_LEN // CMP_STRIDE):
            for i in range(ratio):
                n = ratio * j + i + o - (CMP_LEN // CMP_STRIDE - 1)
                if 0 <= n < nc_pad:
                    m[n, j] += 1.0
    return m


def _rope_tables(pos):
    half = ROT_DIM // 2
    inv = ROPE_THETA ** (-jnp.arange(half, dtype=F32) * 2.0 / ROT_DIM)
    ang = pos.astype(F32)[:, None] * inv[None, :]
    cos, sin = jnp.cos(ang), jnp.sin(ang)
    t = pos.shape[0]
    ones = jnp.ones((t, NSA_HD - ROT_DIM), F32)
    zeros = jnp.zeros((t, NSA_HD - ROT_DIM), F32)
    z8 = jnp.zeros((t, half), F32)
    c = jnp.concatenate([cos, cos, ones], 1)
    up = jnp.concatenate([-sin, z8, zeros], 1)
    dn = jnp.concatenate([z8, sin, zeros], 1)
    two = lambda a: jnp.concatenate([a, a], 1)
    return two(c), two(up), two(dn)


def _nsa_prep_kernel(p_ref, cos_ref, up_ref, dn_ref, q_ref, kc_ref, vc_ref, ks_ref, vs_ref, kw_ref, vw_ref,
                     ksb_ref, vsb_ref, kwb_ref, vwb_ref, g_ref):
    def rope(x):
        reps = x.shape[1] // LANES
        tile = lambda a: jnp.concatenate([a] * reps, axis=1)
        w = x.shape[1]
        return (x * tile(cos_ref[...]) + pltpu.roll(x, w - ROT_DIM // 2, 1) * tile(up_ref[...])
                + pltpu.roll(x, ROT_DIM // 2, 1) * tile(dn_ref[...]))

    q_ref[...] = (rope(p_ref[:, :NSA_QD]) * (NSA_HD ** -0.5)).astype(q_ref.dtype)
    kv = lambda k: p_ref[:, NSA_QD + k * NSA_KVD:NSA_QD + (k + 1) * NSA_KVD]
    kc_ref[...] = rope(kv(0))
    vc_ref[...] = kv(1)
    ks = rope(kv(2))
    ks_ref[...] = ks
    ksb_ref[...] = ks.astype(BF16)
    vs_ref[...] = kv(3)
    vsb_ref[...] = kv(3).astype(BF16)
    kw = rope(kv(4))
    kw_ref[...] = kw
    kwb_ref[...] = kw.astype(BF16)
    vw_ref[...] = kv(5)
    vwb_ref[...] = kv(5).astype(BF16)
    g0 = NSA_QD + 6 * NSA_KVD
    g_ref[...] = jax.nn.sigmoid(p_ref[:, g0:g0 + LANES])


def nsa_prep(p, cos, up, dn, *, q_dtype):
    t = p.shape[0]
    tm = _row_tile(t)
    period = cos.shape[0] // tm
    tab = pl.BlockSpec((tm, LANES), lambda i: (i % period, 0))
    kv32 = jax.ShapeDtypeStruct((t, NSA_KVD), F32)
    kv16 = jax.ShapeDtypeStruct((t, NSA_KVD), BF16)
    kvs = pl.BlockSpec((tm, NSA_KVD), lambda i: (i, 0))
    return pl.pallas_call(
        _nsa_prep_kernel,
        out_shape=(jax.ShapeDtypeStruct((t, NSA_QD), q_dtype),) + (kv32,) * 6 + (kv16,) * 4
        + (jax.ShapeDtypeStruct((t, LANES), F32),),
        grid=(t // tm,),
        in_specs=[pl.BlockSpec((tm, p.shape[1]), lambda i: (i, 0)), tab, tab, tab],
        out_specs=(pl.BlockSpec((tm, NSA_QD), lambda i: (i, 0)),) + (kvs,) * 10
        + (pl.BlockSpec((tm, LANES), lambda i: (i, 0)),),
        compiler_params=_params("parallel"),
        name="nsa_prep",
    )(p, cos, up, dn)


def _block_sums(x, w):
    n_sub = x.shape[0] // CMP_STRIDE
    x = x.reshape(n_sub, CMP_STRIDE, NSA_KVD)
    w = 1.0 + w
    return jnp.sum(x * w[None, :CMP_STRIDE], axis=1), jnp.sum(x * w[None, CMP_STRIDE:], axis=1)


def _nsa_compress_kernel(x_ref, w_ref, o_ref):
    first, second = _block_sums(x_ref[...], w_ref[...])
    n_sub = first.shape[0]
    nxt = pltpu.roll(second, n_sub - 1, 0)
    row = lax.broadcasted_iota(jnp.int32, (n_sub, 1), 0)
    o_ref[...] = jnp.where(row < n_sub - 1, (first + nxt) / CMP_LEN, 0.0).astype(o_ref.dtype)


def nsa_compress(x, pe, b, s):
    n_sub = s // CMP_STRIDE
    return pl.pallas_call(
        _nsa_compress_kernel,
        out_shape=jax.ShapeDtypeStruct((b, n_sub, NSA_KVD), BF16),
        grid=(b,),
        in_specs=[pl.BlockSpec((s, NSA_KVD), lambda i: (i, 0)),
                  pl.BlockSpec((CMP_LEN, NSA_KVD), lambda i: (0, 0))],
        out_specs=pl.BlockSpec((None, n_sub, NSA_KVD), lambda i: (i, 0, 0)),
        compiler_params=_params("parallel"),
        name="nsa_compress",
    )(x, pe.reshape(CMP_LEN, NSA_KVD))


NSA_KV_TILE = 512


def _mask_rows(s, allowed):
    if allowed.shape[0] == s.shape[0]:
        return jnp.where(allowed, s, -jnp.inf)
    reps = s.shape[0] // allowed.shape[0]
    s3 = s.reshape(reps, allowed.shape[0], s.shape[1])
    return jnp.where(allowed[None], s3, -jnp.inf).reshape(s.shape)


def _softmax_rows(s, allowed):
    s = _mask_rows(s, allowed)
    m = jnp.max(s, axis=1, keepdims=True)
    m = jnp.where(m == -jnp.inf, 0.0, m)
    e = jnp.exp(s - m)
    return e / jnp.maximum(jnp.sum(e, axis=1, keepdims=True), 1e-30)


def _select_blocks(score, n_sel):
    nb = score.shape[1]
    lane = lax.broadcasted_iota(jnp.int32, score.shape, 1).astype(F32)
    sel = jnp.zeros(score.shape, F32)
    for _ in range(n_sel):
        m = jnp.max(score, axis=1, keepdims=True)
        first = jnp.min(jnp.where(score == m, lane, float(nb)), axis=1, keepdims=True)
        pick = lane == first
        sel = jnp.where(pick, 1.0, sel)
        score = jnp.where(pick, -jnp.inf, score)
    return sel


def _block_scores(p_slc, pos_q):
    jb = lax.broadcasted_iota(jnp.int32, (1, p_slc.shape[1]), 1)
    cur = pos_q // SEL_BLOCK
    valid = jb * SEL_BLOCK <= pos_q
    forced = (jb == 0) | (jb == cur) | (jb == cur - 1)
    return jnp.where(forced, FORCE_SCORE, jnp.where(valid, p_slc, -1.0))


def _select_blocks_t(score, n_sel):
    st = score.T
    nb = st.shape[0]
    blk = lax.broadcasted_iota(jnp.int32, st.shape, 0).astype(F32)
    sel = jnp.zeros(st.shape, F32)
    for _ in range(n_sel):
        m = jnp.max(st, axis=0, keepdims=True)
        first = jnp.min(jnp.where(st == m, blk, float(nb)), axis=0, keepdims=True)
        pick = blk == first
        sel = jnp.where(pick, 1.0, sel)
        st = jnp.where(pick, -jnp.inf, st)
    return sel.T


def _online_softmax_step(s, allowed, v, m_old, l_old, acc_old, feature_major=False):
    s = _mask_rows(s, allowed)
    m_new = jnp.maximum(m_old, jnp.max(s, axis=1, keepdims=True))
    m_use = jnp.where(m_new == -jnp.inf, 0.0, m_new)
    alpha = jnp.exp(m_old - m_use)
    p = jnp.exp(s - m_use)
    l_new = alpha * l_old + jnp.sum(p, axis=1, keepdims=True)
    if feature_major:
        pv = lax.dot_general(p.astype(BF16), v, NT, preferred_element_type=F32)
    else:
        pv = jnp.dot(p.astype(BF16), v, preferred_element_type=F32)
    return m_new, l_new, alpha * acc_old + pv


def _padded_query(tile, m, in_half):
    sel = jnp.where(in_half, tile, jnp.zeros_like(tile))
    z = jnp.zeros_like(sel)
    return jnp.concatenate([sel, z] if m == 0 else [z, sel], axis=1)


def _nsa_attn_kernel(q_ref, g_ref, kcmp_ref, vcmp_ref, ks_ref, vs_ref, kw_ref, vw_ref, slc_ref, ex_ref,
                     o_ref, qz_ref, sel_ref, oc_ref, m_ref, l_ref, acc_ref):
    i = pl.program_id(1)
    start = i * Q_BLOCK
    pos_q = start + lax.broadcasted_iota(jnp.int32, (Q_BLOCK, 1), 0)
    lane128 = lax.broadcasted_iota(jnp.int32, (1, LANES), 1)
    n_cmp = kcmp_ref.shape[1]
    cmp_end = CMP_STRIDE * lax.broadcasted_iota(jnp.int32, (1, n_cmp), 1) + (CMP_LEN - 1)
    cmp_ok = cmp_end <= pos_q
    gexp = jnp.dot(g_ref[...], ex_ref[...], precision=HIGHEST, preferred_element_type=F32)
    o_ref[...] = jnp.zeros_like(o_ref)

    win_base = pl.multiple_of(jnp.maximum(start - WINDOW, 0), Q_BLOCK)
    band = WINDOW + Q_BLOCK
    kw_t = jnp.concatenate([kw_ref[win_base // Q_BLOCK + u] for u in range(band // Q_BLOCK)], axis=1)
    vw_t = vw_ref[pl.ds(win_base, band), :]
    dpos = pos_q - (win_base + lax.broadcasted_iota(jnp.int32, (1, band), 1))
    win_ok = (dpos >= 0) & (dpos < WINDOW)

    for g in range(NSA_KV_HEADS):
        m, half = divmod(g, 2)
        in_half = (lane128 // NSA_HD) == half
        qz = jnp.concatenate(
            [_padded_query(q_ref[:, (m * NSA_GROUP + r) * LANES:(m * NSA_GROUP + r + 1) * LANES], m, in_half)
             for r in range(NSA_GROUP)], axis=0)
        qz_ref[g] = qz
        s = jnp.dot(qz, kcmp_ref[...], preferred_element_type=F32)
        p = _softmax_rows(s, cmp_ok)
        oc_ref[g] = jnp.dot(p.astype(BF16), vcmp_ref[...], preferred_element_type=F32)[:, m * LANES:(m + 1) * LANES]
        imp = p[0:Q_BLOCK]
        for r in range(1, NSA_GROUP):
            imp = imp + p[r * Q_BLOCK:(r + 1) * Q_BLOCK]
        p_slc = jnp.dot(imp, slc_ref[...], precision=HIGHEST, preferred_element_type=F32)
        sel_ref[g] = _select_blocks_t(_block_scores(p_slc, pos_q), N_SEL).astype(BF16)

    m_ref[...] = jnp.full_like(m_ref, -jnp.inf)
    l_ref[...] = jnp.zeros_like(l_ref)
    acc_ref[...] = jnp.zeros_like(acc_ref)

    def kv_step(t, carry):
        k0 = pl.multiple_of(t * NSA_KV_TILE, NSA_KV_TILE)
        k_t = ks_ref[t]
        v_t = vs_ref[pl.ds(k0, NSA_KV_TILE), :]
        key = k0 + lax.broadcasted_iota(jnp.int32, (1, NSA_KV_TILE), 1)
        blk = lax.broadcasted_iota(jnp.int32, (LANES, 1), 0)
        expand = jnp.where(blk == key // SEL_BLOCK, 1.0, 0.0).astype(BF16)
        causal = key <= pos_q
        for g in range(NSA_KV_HEADS):
            chosen = jnp.dot(sel_ref[g], expand, preferred_element_type=F32)
            s = jnp.dot(qz_ref[g], k_t, preferred_element_type=F32)
            m_ref[g], l_ref[g], acc_ref[g] = _online_softmax_step(s, (chosen > 0.5) & causal, v_t,
                                                                  m_ref[g], l_ref[g], acc_ref[g])
        return carry

    lax.fori_loop(0, (start + Q_BLOCK + NSA_KV_TILE - 1) // NSA_KV_TILE, kv_step, 0)

    for g in range(NSA_KV_HEADS):
        m, half = divmod(g, 2)
        in_half = (lane128 // NSA_HD) == half
        mcols = slice(m * LANES, (m + 1) * LANES)
        o_s = acc_ref[g][:, mcols] / jnp.maximum(l_ref[g], 1e-30)
        s = jnp.dot(qz_ref[g], kw_t, preferred_element_type=F32)
        o_w = jnp.dot(_softmax_rows(s, win_ok).astype(BF16), vw_t, preferred_element_type=F32)[:, mcols]
        o_c = oc_ref[g]
        for r in range(NSA_GROUP):
            t = m * NSA_GROUP + r
            cols = slice(t * LANES, (t + 1) * LANES)
            rows = slice(r * Q_BLOCK, (r + 1) * Q_BLOCK)
            comb = (gexp[:, cols] * o_c[rows] + gexp[:, NSA_QD + t * LANES:NSA_QD + (t + 1) * LANES] * o_s[rows]
                    + gexp[:, 2 * NSA_QD + t * LANES:2 * NSA_QD + (t + 1) * LANES] * o_w[rows])
            o_ref[:, cols] += jnp.where(in_half, comb, 0.0)


def nsa_attn_prompt(q, gates, kcmp, vcmp, ks, vs, kw, vw, b, s):
    nq = s // Q_BLOCK
    rows = NSA_GROUP * Q_BLOCK
    slc = jnp.asarray(_nsa_slc_matrix(s // CMP_STRIDE, LANES))
    ex = jnp.asarray(_nsa_gate_expand())
    seq = pl.BlockSpec((s, NSA_KVD), lambda bi, i: (bi, 0))
    n_cmp = s // CMP_STRIDE
    cmp_spec = pl.BlockSpec((None, n_cmp, NSA_KVD), lambda bi, i: (bi, 0, 0))
    cmp_t_spec = pl.BlockSpec((None, NSA_KVD, n_cmp), lambda bi, i: (bi, 0, 0))

    def key_tiles(k, tile):
        kt = k.reshape(b, s // tile, tile, NSA_KVD).transpose(0, 1, 3, 2)
        return kt, pl.BlockSpec((None, s // tile, NSA_KVD, tile), lambda bi, i: (bi, 0, 0, 0))

    ks, ks_spec = key_tiles(ks, NSA_KV_TILE)
    kw, kw_spec = key_tiles(kw, Q_BLOCK)
    kcmp = kcmp.transpose(0, 2, 1)
    return pl.pallas_call(
        _nsa_attn_kernel,
        out_shape=jax.ShapeDtypeStruct((b * s, NSA_QD), F32),
        grid=(b, nq),
        in_specs=[
            pl.BlockSpec((Q_BLOCK, NSA_QD), lambda bi, i: (bi * nq + i, 0)),
            pl.BlockSpec((Q_BLOCK, LANES), lambda bi, i: (bi * nq + i, 0)),
            cmp_t_spec, cmp_spec, ks_spec, seq, kw_spec, seq,
            pl.BlockSpec(slc.shape, lambda bi, i: (0, 0)),
            pl.BlockSpec(ex.shape, lambda bi, i: (0, 0)),
        ],
        out_specs=pl.BlockSpec((Q_BLOCK, NSA_QD), lambda bi, i: (bi * nq + i, 0)),
        scratch_shapes=[pltpu.VMEM((NSA_KV_HEADS, rows, NSA_KVD), BF16),
                        pltpu.VMEM((NSA_KV_HEADS, Q_BLOCK, LANES), BF16),
                        pltpu.VMEM((NSA_KV_HEADS, rows, LANES), F32),
                        pltpu.VMEM((NSA_KV_HEADS, rows, 1), F32),
                        pltpu.VMEM((NSA_KV_HEADS, rows, 1), F32),
                        pltpu.VMEM((NSA_KV_HEADS, rows, NSA_KVD), F32)],
        compiler_params=_params("parallel", "arbitrary"),
        name="nsa_attn_prompt",
    )(q, gates, kcmp, vcmp, ks, vs, kw, vw, slc, ex)


PAGES_PER_STEP = 8
SUM_PAGES_PER_STEP = 16


def _feature_major(cache):
    n, tokens = cache.shape[:2]
    return cache.transpose(0, 2, 3, 1).reshape(n, NSA_KVD, tokens)


def _page_specs(n, page):
    def one(u):
        return pl.BlockSpec((None, NSA_KVD, page), lambda bi, j, pt: (pt[bi, j * n + u], 0, 0))
    return [one(u) for u in range(n)]


def _nsa_page_sums_kernel(pt_ref, *refs):
    n = SUM_PAGES_PER_STEP
    kp, vp = refs[:n], refs[n:2 * n]
    wk1_ref, wk2_ref, wv1_ref, wv2_ref, grp_ref, fk_ref, sk_ref, fv_ref, sv_ref = refs[2 * n:]

    def sums(pages, w_ref, o_ref):
        xw = jnp.concatenate([(p[...] * w_ref[...]).astype(BF16) for p in pages], axis=1)
        o_ref[...] = jnp.dot(xw, grp_ref[...], preferred_element_type=F32)

    sums(kp, wk1_ref, fk_ref)
    sums(kp, wk2_ref, sk_ref)
    sums(vp, wv1_ref, fv_ref)
    sums(vp, wv2_ref, sv_ref)


def nsa_page_sums(page_table, pool_k, pool_v, pe_k, pe_v):
    db, n_pages = page_table.shape
    page = pool_k.shape[2]
    n = SUM_PAGES_PER_STEP
    per_page = page // CMP_STRIDE
    assert n * per_page == LANES
    out = jax.ShapeDtypeStruct((db, NSA_KVD, n_pages * per_page), F32)
    out_spec = pl.BlockSpec((None, NSA_KVD, LANES), lambda bi, j, pt: (bi, 0, j))
    w_spec = pl.BlockSpec((NSA_KVD, page), lambda bi, j, pt: (0, 0))
    halves = lambda pe: [jnp.tile((1.0 + pe.reshape(CMP_LEN, NSA_KVD)[o:o + CMP_STRIDE]).T, (1, per_page))
                         for o in (0, CMP_STRIDE)]
    grp = np.zeros((n * page, LANES), np.float32)
    tok = np.arange(n * page)
    grp[tok, tok // CMP_STRIDE] = 1.0
    grp = jnp.asarray(grp, BF16)
    return pl.pallas_call(
        _nsa_page_sums_kernel,
        out_shape=(out,) * 4,
        grid_spec=pltpu.PrefetchScalarGridSpec(
            num_scalar_prefetch=1,
            grid=(db, n_pages // n),
            in_specs=_page_specs(n, page) * 2 + [w_spec] * 4 + [pl.BlockSpec(grp.shape, lambda bi, j, pt: (0, 0))],
            out_specs=(out_spec,) * 4,
        ),
        compiler_params=_params("parallel", "arbitrary"),
        name="nsa_page_sums",
    )(page_table, *([pool_k] * n), *([pool_v] * n), *halves(pe_k), *halves(pe_v), grp)


def _decode_queries(q_ref, g):
    m, half = divmod(g, 2)
    in_half = (lax.broadcasted_iota(jnp.int32, (1, LANES), 1) // NSA_HD) == half
    tiles = [_padded_query(q_ref[:, (m * NSA_GROUP + r) * LANES:(m * NSA_GROUP + r + 1) * LANES], m, in_half)
             for r in range(NSA_GROUP)]
    return jnp.concatenate(tiles, axis=0).astype(BF16)


def _group_rows(x):
    return jnp.concatenate([x] * NSA_GROUP, axis=0)


def _nsa_decode_select_kernel(past, q_ref, fk_ref, sk_ref, fv_ref, sv_ref, kn_ref, vn_ref, wk_ref, wv_ref,
                              slc_ref, sel_ref, oc_ref):
    nq = q_ref.shape[0]
    n_cmp = fk_ref.shape[1]
    col = lax.broadcasted_iota(jnp.int32, (1, n_cmp), 1)
    eye = (lax.broadcasted_iota(jnp.int32, (NSA_KVD, 1), 0)
           == lax.broadcasted_iota(jnp.int32, (1, NSA_KVD), 1))

    def summaries(f_ref, s_ref, new_ref, w_ref):
        w = 1.0 + w_ref[CMP_STRIDE:CMP_STRIDE + nq, :]
        second_new = jnp.sum(new_ref[...] * w, axis=0, keepdims=True)
        new_col = jnp.sum(jnp.where(eye, second_new, 0.0), axis=1, keepdims=True)
        nxt = jnp.where(col == n_cmp - 1, new_col, pltpu.roll(s_ref[...], n_cmp - 1, 1))
        return ((f_ref[...] + nxt) / CMP_LEN).astype(BF16)

    kcmp_t = summaries(fk_ref, sk_ref, kn_ref, wk_ref)
    vcmp_t = summaries(fv_ref, sv_ref, vn_ref, wv_ref)
    pos_q = past + lax.broadcasted_iota(jnp.int32, (nq, 1), 0)
    cmp_end = CMP_STRIDE * lax.broadcasted_iota(jnp.int32, (1, n_cmp), 1) + (CMP_LEN - 1)
    cmp_ok = _group_rows(cmp_end <= pos_q)
    lane128 = lax.broadcasted_iota(jnp.int32, (1, LANES), 1)
    oc_ref[...] = jnp.zeros_like(oc_ref)
    for g in range(NSA_KV_HEADS):
        m, half = divmod(g, 2)
        in_half = (lane128 // NSA_HD) == half
        s = jnp.dot(_decode_queries(q_ref, g), kcmp_t, preferred_element_type=F32)
        p = _softmax_rows(s, cmp_ok)
        o_c = lax.dot_general(p.astype(BF16), vcmp_t, NT,
                              preferred_element_type=F32)[:, m * LANES:(m + 1) * LANES]
        imp = p[0:nq]
        for r in range(1, NSA_GROUP):
            imp = imp + p[r * nq:(r + 1) * nq]
        for r in range(NSA_GROUP):
            cols = slice((m * NSA_GROUP + r) * LANES, (m * NSA_GROUP + r + 1) * LANES)
            oc_ref[:, cols] += jnp.where(in_half, o_c[r * nq:(r + 1) * nq], 0.0)
        p_slc = jnp.dot(imp, slc_ref[...], precision=HIGHEST, preferred_element_type=F32)
        sel = _group_rows(_select_blocks(_block_scores(p_slc, pos_q), N_SEL)).astype(BF16)
        rows = slice(g * NSA_GROUP * nq, (g + 1) * NSA_GROUP * nq)
        for lt in range(sel_ref.shape[0]):
            sel_ref[lt, rows, :] = sel[:, lt * LANES:(lt + 1) * LANES]


def nsa_decode_select(q, sums, kc_new, vc_new, pe_k, pe_v, db, nq, past):
    n_cmp = sums[0].shape[2]
    nsb_pad = _round_up(-(-(past + nq) // SEL_BLOCK), LANES)
    sel_rows = NSA_KV_HEADS * NSA_GROUP * nq
    slc = jnp.asarray(_nsa_slc_matrix(n_cmp, nsb_pad))
    cmp_spec = pl.BlockSpec((None, NSA_KVD, n_cmp), lambda bi: (bi, 0, 0))
    new_spec = pl.BlockSpec((nq, NSA_KVD), lambda bi: (bi, 0))
    pe_spec = pl.BlockSpec((CMP_LEN, NSA_KVD), lambda bi: (0, 0))
    return pl.pallas_call(
        functools.partial(_nsa_decode_select_kernel, past),
        out_shape=(jax.ShapeDtypeStruct((db, nsb_pad // LANES, sel_rows, LANES), BF16),
                   jax.ShapeDtypeStruct((db * nq, NSA_QD), F32)),
        grid=(db,),
        in_specs=[pl.BlockSpec((nq, NSA_QD), lambda bi: (bi, 0)), cmp_spec, cmp_spec, cmp_spec, cmp_spec,
                  new_spec, new_spec, pe_spec, pe_spec, pl.BlockSpec(slc.shape, lambda bi: (0, 0))],
        out_specs=(pl.BlockSpec((None, nsb_pad // LANES, sel_rows, LANES), lambda bi: (bi, 0, 0, 0)),
                   pl.BlockSpec((nq, NSA_QD), lambda bi: (bi, 0))),
        compiler_params=_params("parallel"),
        name="nsa_decode_select",
    )(q, *sums, kc_new, vc_new, pe_k.reshape(CMP_LEN, NSA_KVD), pe_v.reshape(CMP_LEN, NSA_KVD), slc)


def _nsa_decode_attend_kernel(past, pt_ref, q_ref, sel_ref, g_ref, oc_ref, ksn_ref, vsn_ref, wk_ref, wv_ref,
                              kwn_ref, vwn_ref, ex_ref, xp_ref, *refs):
    n = PAGES_PER_STEP
    kp, vp = refs[:n], refs[n:2 * n]
    o_ref, qz_scr, m_scr, l_scr, acc_scr = refs[2 * n:]
    j = pl.program_id(1)
    nq = q_ref.shape[0]
    n_keys = n * kp[0].shape[1]
    steps_per_tile = xp_ref.shape[0]
    pos_q = jnp.concatenate([past + lax.broadcasted_iota(jnp.int32, (nq, 1), 0)] * (NSA_KV_HEADS * NSA_GROUP),
                            axis=0)

    @pl.when(j == 0)
    def _():
        m_scr[...] = jnp.full_like(m_scr, -jnp.inf)
        l_scr[...] = jnp.zeros_like(l_scr)
        acc_scr[...] = jnp.zeros_like(acc_scr)
        qz_scr[...] = jnp.concatenate([_decode_queries(q_ref, g) for g in range(NSA_KV_HEADS)], axis=0)

    def attend(s, allowed, v, feature_major):
        m_scr[...], l_scr[...], acc_scr[...] = _online_softmax_step(
            s, allowed, v, m_scr[...], l_scr[...], acc_scr[...], feature_major)

    k_t = jnp.concatenate([r[...].astype(BF16) for r in kp], axis=1)
    v_t = jnp.concatenate([r[...].astype(BF16) for r in vp], axis=1)
    key = j * n_keys + lax.broadcasted_iota(jnp.int32, (1, n_keys), 1)
    chosen = jnp.dot(sel_ref[j // steps_per_tile], xp_ref[j % steps_per_tile], preferred_element_type=F32)
    attend(jnp.dot(qz_scr[...], k_t, preferred_element_type=F32), (chosen > 0.5) & (key <= pos_q), v_t, True)

    @pl.when(j == pl.num_programs(1) - 1)
    def _():
        qz = qz_scr[...]
        lane128 = lax.broadcasted_iota(jnp.int32, (1, LANES), 1)
        n_new = ksn_ref.shape[0]
        new_idx = lax.broadcasted_iota(jnp.int32, (1, n_new), 1)
        new_key = past + new_idx
        new_blk = past // SEL_BLOCK
        chosen_new = sel_ref[new_blk // LANES][:, new_blk % LANES:new_blk % LANES + 1].astype(F32) > 0.5
        attend(lax.dot_general(qz, ksn_ref[...], NT, preferred_element_type=F32),
               chosen_new & (new_key <= pos_q) & (new_idx < nq), vsn_ref[...], False)
        o_s = acc_scr[...] / jnp.maximum(l_scr[...], 1e-30)
        n_win = wk_ref.shape[1]
        k_wc, v_wc = wk_ref[...].astype(BF16), wv_ref[...].astype(BF16)
        pos_wc = past - n_win + lax.broadcasted_iota(jnp.int32, (1, n_win), 1)
        ok_wc = (pos_q - pos_wc >= 0) & (pos_q - pos_wc < WINDOW) & (pos_wc >= 0)
        ok_wn = (pos_q - new_key >= 0) & (pos_q - new_key < WINDOW) & (new_idx < nq)
        win = (jnp.full(m_scr.shape, -jnp.inf, F32), jnp.zeros(l_scr.shape, F32), jnp.zeros(acc_scr.shape, F32))
        win = _online_softmax_step(jnp.dot(qz, k_wc, preferred_element_type=F32), ok_wc, v_wc, *win, True)
        win = _online_softmax_step(lax.dot_general(qz, kwn_ref[...], NT, preferred_element_type=F32), ok_wn,
                                   vwn_ref[...], *win, False)
        o_w = win[2] / jnp.maximum(win[1], 1e-30)
        gexp = jnp.dot(g_ref[...], ex_ref[...], precision=HIGHEST, preferred_element_type=F32)
        o_ref[...] = gexp[:, :NSA_QD] * oc_ref[...]
        for g in range(NSA_KV_HEADS):
            m, half = divmod(g, 2)
            in_half = (lane128 // NSA_HD) == half
            mcols = slice(m * LANES, (m + 1) * LANES)
            for r in range(NSA_GROUP):
                t = m * NSA_GROUP + r
                cols = slice(t * LANES, (t + 1) * LANES)
                rows = slice((g * NSA_GROUP + r) * nq, (g * NSA_GROUP + r + 1) * nq)
                comb = (gexp[:, NSA_QD + t * LANES:NSA_QD + (t + 1) * LANES] * o_s[rows, mcols]
                        + gexp[:, 2 * NSA_QD + t * LANES:2 * NSA_QD + (t + 1) * LANES] * o_w[rows, mcols])
                o_ref[:, cols] += jnp.where(in_half, comb, 0.0)


def nsa_decode_attend(page_table, q, sel, gates, o_c, ks_new, vs_new, wk, wv, kw_new, vw_new, pool_k, pool_v,
                      db, nq, past):
    n_pages = page_table.shape[1]
    page = pool_k.shape[2]
    ex = jnp.asarray(_nsa_gate_expand())
    n_win = wk.shape[2]
    rows = NSA_KV_HEADS * NSA_GROUP * nq
    n_keys = PAGES_PER_STEP * page
    assert past % SEL_BLOCK + nq <= SEL_BLOCK and (LANES * SEL_BLOCK) % n_keys == 0
    steps_per_tile = LANES * SEL_BLOCK // n_keys
    xp = np.zeros((steps_per_tile, LANES, n_keys), np.float32)
    keys = np.arange(n_keys)
    for u in range(steps_per_tile):
        xp[u, u * (n_keys // SEL_BLOCK) + keys // SEL_BLOCK, keys] = 1.0
    xp = jnp.asarray(xp, BF16)
    per_b = lambda shape: pl.BlockSpec((None,) + shape, lambda bi, j, pt: (bi,) + (0,) * len(shape))
    q_rows = lambda width: pl.BlockSpec((nq, width), lambda bi, j, pt: (bi, 0))
    new_rows = ks_new.shape[1]
    return pl.pallas_call(
        functools.partial(_nsa_decode_attend_kernel, past),
        out_shape=jax.ShapeDtypeStruct((db * nq, NSA_QD), F32),
        grid_spec=pltpu.PrefetchScalarGridSpec(
            num_scalar_prefetch=1,
            grid=(db, n_pages // PAGES_PER_STEP),
            in_specs=[q_rows(NSA_QD), per_b(sel.shape[1:]), q_rows(LANES), q_rows(NSA_QD),
                      per_b((new_rows, NSA_KVD)), per_b((new_rows, NSA_KVD)),
                      per_b((NSA_KVD, n_win)), per_b((NSA_KVD, n_win)),
                      per_b((new_rows, NSA_KVD)), per_b((new_rows, NSA_KVD)),
                      pl.BlockSpec(ex.shape, lambda bi, j, pt: (0, 0)),
                      pl.BlockSpec(xp.shape, lambda bi, j, pt: (0, 0, 0))]
            + _page_specs(PAGES_PER_STEP, page) * 2,
            out_specs=q_rows(NSA_QD),
            scratch_shapes=[pltpu.VMEM((rows, NSA_KVD), BF16), pltpu.VMEM((rows, 1), F32),
                            pltpu.VMEM((rows, 1), F32), pltpu.VMEM((rows, NSA_KVD), F32)],
        ),
        compiler_params=_params("parallel", "arbitrary"),
        name="nsa_decode_attend",
    )(page_table, q, sel, gates, o_c, ks_new, vs_new, wk, wv, kw_new, vw_new, ex, xp,
      *([pool_k] * PAGES_PER_STEP), *([pool_v] * PAGES_PER_STEP))


def _nsa_weights(w_in, w_out):
    perm = _nsa_q_perm()
    return jnp.concatenate([w_in[:, perm], w_in[:, NSA_QD:]], axis=1), w_out[perm]


def nsa_prompt(x, scale, shift, w_in_p, pe_k, pe_v, b, s):
    p, _ = project(x, scale, shift, w_in_p, b, s)
    cos, up, dn = _rope_tables(jnp.arange(s, dtype=jnp.int32))
    q, kc, vc, ks, vs, kw, vw, ksb, vsb, kwb, vwb, gates = nsa_prep(p, cos, up, dn, q_dtype=BF16)
    kcmp = nsa_compress(kc, pe_k, b, s)
    vcmp = nsa_compress(vc, pe_v, b, s)
    o = nsa_attn_prompt(q, gates, kcmp, vcmp, ksb, vsb, kwb, vwb, b, s)
    nw = min(WINDOW, s)
    rs = lambda a: a.reshape(b, s, NSA_KV_HEADS, NSA_HD)
    return o, (rs(kc), rs(vc), rs(ks), rs(vs), rs(kw)[:, s - nw:], rs(vw)[:, s - nw:])


def nsa_sample(x, scale, shift, ck, cv, sk, sv, wk, wv, page_table, w_in_p, pe_k, pe_v, db, nq):
    past = page_table.shape[1] * ck.shape[1]
    p, _ = project(x, scale, shift, w_in_p, db, nq)
    pos = past + jnp.tile(jnp.arange(nq, dtype=jnp.int32), db)
    cos, up, dn = _rope_tables(pos)
    q, kc, vc, ks, vs, kw, vw, ksb, vsb, kwb, vwb, gates = nsa_prep(p, cos, up, dn, q_dtype=F32)
    sums = nsa_page_sums(page_table, _feature_major(ck), _feature_major(cv), pe_k, pe_v)
    sel, o_c = nsa_decode_select(q, sums, kc, vc, pe_k, pe_v, db, nq, past)
    new_pad = lambda a: jnp.pad(a.reshape(db, nq, NSA_KVD), ((0, 0), (0, LANES - nq), (0, 0)))
    o = nsa_decode_attend(page_table, q, sel, gates, o_c, new_pad(ksb), new_pad(vsb),
                          _feature_major(wk), _feature_major(wv), new_pad(kwb), new_pad(vwb),
                          _feature_major(sk), _feature_major(sv), db, nq, past)
    rs = lambda a: a.reshape(db, nq, NSA_KV_HEADS, NSA_HD)
    slide = lambda cache, new: jnp.concatenate([cache, rs(new).astype(cache.dtype)], 1)[:, nq:]
    return o, (rs(kc), rs(vc), rs(ks), rs(vs), slide(wk, kw), slide(wv, vw))


def _top_values(s, k):
    n = s.shape[0]
    row = lax.broadcasted_iota(jnp.int32, (n, 1), 0).astype(F32)
    vals = []
    for _ in range(k):
        m = jnp.max(s, axis=0, keepdims=True)
        first = jnp.min(jnp.where(s == m, row, float(n)), axis=0, keepdims=True)
        vals.append(m)
        s = jnp.where(row == first, -jnp.inf, s)
    return vals


def _peer_route_kernel(qv_ref, keys_ref, s1_ref, s2_ref, e1_ref, e2_ref, th_ref):
    for h in range(PEER_HEADS):
        q1 = qv_ref[:, (2 * h) * PEER_DKEY:(2 * h + 1) * PEER_DKEY]
        q2 = qv_ref[:, (2 * h + 1) * PEER_DKEY:(2 * h + 2) * PEER_DKEY]
        s1 = lax.dot_general(keys_ref[h, 0], q1, NT, preferred_element_type=F32)
        s2 = lax.dot_general(keys_ref[h, 1], q2, NT, preferred_element_type=F32)
        n_rank = PEER_TOPK + 1
        top1 = _top_values(s1, n_rank)
        top2 = _top_values(s2, n_rank)
        t1 = jnp.concatenate(top1[:PEER_TOPK], axis=0)
        t2 = jnp.concatenate(top2[:PEER_TOPK], axis=0)
        row8 = lax.broadcasted_iota(jnp.int32, (SUBLANES, 1), 0)
        ends = jnp.where(row8 == 0, top1[0] + top2[PEER_TOPK],
                         jnp.where(row8 == 1, top1[PEER_TOPK] + top2[0], -jnp.inf))
        pairs = [top1[0] + t2, top1[1] + t2[:SUBLANES], t1[SUBLANES:] + top2[0], ends]
        for i in range(2, SUBLANES):
            pairs.append(jnp.where(row8 < n_rank // (i + 1), top1[i] + t2[:SUBLANES], -jnp.inf))
        best = _top_values(jnp.concatenate(pairs, axis=0), n_rank)
        z = best[0] * 0.0
        for v in best[:PEER_TOPK]:
            z = z + jnp.exp(v - best[0])
        cut = 0.5 * (best[PEER_TOPK - 1] + best[PEER_TOPK])
        e1 = jnp.exp(s1 - top1[0])
        e2 = jnp.exp(s2 - top2[0]) / z
        for c in range(qv_ref.shape[0] // LANES):
            cols = slice(c * LANES, (c + 1) * LANES)
            s1_ref[h, c] = s1[:, cols]
            s2_ref[h, c] = s2[:, cols]
            e1_ref[h, c] = e1[:, cols]
            e2_ref[h, c] = e2[:, cols]
            th_ref[h, c] = cut[:, cols]


def peer_route(qv, keys, *, tm):
    t = qv.shape[0]
    nc = tm // LANES
    big = jax.ShapeDtypeStruct((PEER_HEADS, t // LANES, PEER_NKEYS, LANES), F32)
    big_spec = pl.BlockSpec((PEER_HEADS, nc, PEER_NKEYS, LANES), lambda i: (0, i, 0, 0))
    return pl.pallas_call(
        _peer_route_kernel,
        out_shape=(big, big, big, big, jax.ShapeDtypeStruct((PEER_HEADS, t // LANES, 1, LANES), F32)),
        grid=(t // tm,),
        in_specs=[
            pl.BlockSpec((tm, 2 * PEER_HEADS * PEER_DKEY), lambda i: (i, 0)),
            pl.BlockSpec(keys.shape, lambda i: (0, 0, 0, 0)),
        ],
        out_specs=(big_spec, big_spec, big_spec, big_spec,
                   pl.BlockSpec((PEER_HEADS, nc, 1, LANES), lambda i: (0, i, 0, 0))),
        compiler_params=_params("parallel"),
        name="peer_route",
    )(qv, keys)


PEER_A_PER_STEP = 2 * SUBLANES
PEER_ROUTE_TILE = 256


def _gelu_tanh(x):
    return 0.5 * x * (1.0 + jnp.tanh(math.sqrt(2.0 / math.pi) * (x + 0.044715 * (x * x * x))))


def _peer_dense_kernel(h_ref, s1_ref, e1_ref, s2_ref, e2_ref, th_ref, u_ref, vt_ref, o_ref,
                       act_ref, g_ref, acc_ref):
    j = pl.program_id(1)
    tm = h_ref.shape[0]

    @pl.when(j == 0)
    def _():
        acc_ref[...] = jnp.zeros_like(acc_ref)

    nc = tm // LANES
    act = lax.dot_general(u_ref[...], h_ref[...], NT, preferred_element_type=F32)
    for c in range(nc):
        act_ref[c] = act[:, c * LANES:(c + 1) * LANES]

    def tile(idx, carry):
        a, c = idx // nc, idx % nc
        rows = pl.ds(pl.multiple_of(a * PEER_NKEYS, PEER_NKEYS), PEER_NKEYS)
        w = jnp.zeros((PEER_NKEYS, LANES), F32)
        for h in range(PEER_HEADS):
            need = th_ref[h, c] - s1_ref[h, c, pl.ds(a, 1), :]
            w = w + e1_ref[h, c, pl.ds(a, 1), :] * jnp.where(s2_ref[h, c] >= need, e2_ref[h, c], 0.0)
        g_ref[c, rows, :] = (w * _gelu_tanh(act_ref[c, rows, :])).astype(BF16)
        return carry

    lax.fori_loop(0, PEER_A_PER_STEP * nc, tile, 0)
    g = jnp.concatenate([g_ref[c] for c in range(nc)], axis=1)
    acc_ref[...] += jnp.dot(vt_ref[...], g, preferred_element_type=F32)

    @pl.when(j == pl.num_programs(1) - 1)
    def _():
        o_ref[...] = acc_ref[...].T


def peer_dense(h, s1, s2, e1, e2, th, u, vt, *, tm):
    t, d = h.shape
    ne = PEER_A_PER_STEP * PEER_NKEYS
    nc = tm // LANES
    tok = pl.BlockSpec((PEER_HEADS, nc, PEER_NKEYS, LANES), lambda i, j: (0, i, 0, 0))
    arow = pl.BlockSpec((PEER_HEADS, nc, PEER_A_PER_STEP, LANES), lambda i, j: (0, i, j, 0))
    return pl.pallas_call(
        _peer_dense_kernel,
        out_shape=jax.ShapeDtypeStruct((t, d), F32),
        grid=(t // tm, PEER_NKEYS // PEER_A_PER_STEP),
        in_specs=[
            pl.BlockSpec((tm, d), lambda i, j: (i, 0)),
            arow, arow, tok, tok,
            pl.BlockSpec((PEER_HEADS, nc, 1, LANES), lambda i, j: (0, i, 0, 0)),
            pl.BlockSpec((ne, d), lambda i, j: (j, 0)),
            pl.BlockSpec((d, ne), lambda i, j: (0, j)),
        ],
        out_specs=pl.BlockSpec((tm, d), lambda i, j: (i, 0)),
        scratch_shapes=[pltpu.VMEM((nc, ne, LANES), F32), pltpu.VMEM((nc, ne, LANES), BF16),
                        pltpu.VMEM((d, tm), F32)],
        compiler_params=_params("parallel", "arbitrary"),
        name="peer_dense",
    )(h, s1, e1, s2, e2, th, u, vt)


def _peer_tables(u, v):
    return u.astype(BF16), v.astype(BF16).T


def peer_ffn(x, scale, shift, wq, keys, u_bf, vt_bf, b, s):
    qv, h = project(x, scale, shift, wq, b, s)
    tm = _row_tile(b * s)
    s1, s2, e1, e2, th = peer_route(qv, keys, tm=min(tm, PEER_ROUTE_TILE))
    return peer_dense(h, s1, s2, e1, e2, th, u_bf, vt_bf, tm=tm)


def kernel(x_prompt, x_sample, state_gla, cache_cmp_k, cache_cmp_v, cache_sel_k, cache_sel_v, cache_win_k, cache_win_v, page_table, c_prompt, c_sample, ada_w, ada_b, ln_g, ln_b, gla_w_in, gla_w_a2, gla_b_a2, gla_gn, gla_w_out, nsa_w_in, nsa_pe_k, nsa_pe_v, nsa_w_out, peer_wq, peer_keys, peer_u, peer_v):
    bp, sp, d = x_prompt.shape
    bs, ss, _ = x_sample.shape
    groups = ((bp, sp), (bs, ss))
    c_all = jnp.concatenate([c_prompt, c_sample], axis=0)
    c_rows = _round_up(bp + bs, SUBLANES)
    mod = ada_mod(jnp.pad(c_all, ((0, c_rows - bp - bs), (0, 0))), ada_w, ada_b)
    ys = [x_prompt.reshape(bp * sp, d), x_sample.reshape(bs * ss, d)]
    gla_states, nsa_rows = ([], []), ([], [])
    for i in range(DEPTH):
        row0 = (0, bp)
        mods = [[mod[i, row0[n]:row0[n] + b, k * d:(k + 1) * d][:, None, :] for k in range(6)]
                for n, (b, _) in enumerate(groups)]
        j = i // N_MIXERS
        if i % N_MIXERS == 0:
            for n, (b, s) in enumerate(groups):
                shift, scale, gate = mods[n][:3]
                proj, _ = project(ys[n], scale, shift, gla_w_in[j], b, s)
                o, st = gla_recurrence(proj, gla_w_a2[j], gla_b_a2[j], None if n == 0 else state_gla[j], b, s)
                gla_states[n].append(st.astype(state_gla.dtype))
                ys[n] = post(ys[n], gate, ln_g[i, 0], ln_b[i, 0], b, s, sub=o, w_out=gla_w_out[j],
                             gla_proj=proj, gla_gn=gla_gn[j])
        else:
            w_in_p, w_out_p = _nsa_weights(nsa_w_in[j], nsa_w_out[j])
            for n, (b, s) in enumerate(groups):
                shift, scale, gate = mods[n][:3]
                if n == 0:
                    o, rows = nsa_prompt(ys[n], scale, shift, w_in_p, nsa_pe_k[j], nsa_pe_v[j], b, s)
                else:
                    o, rows = nsa_sample(ys[n], scale, shift, cache_cmp_k[j], cache_cmp_v[j], cache_sel_k[j],
                                         cache_sel_v[j], cache_win_k[j], cache_win_v[j], page_table,
                                         w_in_p, nsa_pe_k[j], nsa_pe_v[j], b, s)
                nsa_rows[n].append(rows)
                ys[n] = post(ys[n], gate, ln_g[i, 0], ln_b[i, 0], b, s, sub=o, w_out=w_out_p)
        u4, vt4 = _peer_tables(peer_u[i], peer_v[i])
        for n, (b, s) in enumerate(groups):
            shift, scale, gate = mods[n][3:]
            f = peer_ffn(ys[n], scale, shift, peer_wq[i], peer_keys[i], u4, vt4, b, s)
            ys[n] = post(ys[n], gate, ln_g[i, 1], ln_b[i, 1], b, s, sub=f)

    st = lambda ts, k: jnp.stack([t[k] for t in ts])
    return (ys[0].reshape(bp, sp, d), ys[1].reshape(bs, ss, d),
            jnp.stack(gla_states[0]), jnp.stack(gla_states[1]),
            *(st(nsa_rows[0], k) for k in range(6)), *(st(nsa_rows[1], k) for k in range(6)))
```

```python
import functools
import math

import jax
import jax.numpy as jnp
import numpy as np
from jax import lax
from jax.experimental import pallas as pl
from jax.experimental.pallas import tpu as pltpu

D_MODEL = 1024
DEPTH = 2
N_MIXERS = 2
DN_ALPHA = (2.0 * DEPTH) ** 0.25
LN_EPS = 1e-5
F32 = jnp.float32
BF16 = jnp.bfloat16
HIGHEST = lax.Precision.HIGHEST

GLA_HEADS = 4
GLA_DK = D_MODEL // 2 // GLA_HEADS
GLA_DV = D_MODEL // GLA_HEADS
GLA_TAU = 16.0
GLA_CHUNK = 64
GLA_HK = GLA_HEADS * GLA_DK
GLA_HV = GLA_HEADS * GLA_DV

NSA_HEADS = 16
NSA_KV_HEADS = 4
NSA_GROUP = NSA_HEADS // NSA_KV_HEADS
NSA_HD = D_MODEL // NSA_HEADS
NSA_QD = NSA_HEADS * NSA_HD
NSA_KVD = NSA_KV_HEADS * NSA_HD
CMP_LEN = 32
CMP_STRIDE = 16
SEL_BLOCK = 64
N_SEL = 16
WINDOW = 512
Q_BLOCK = 128
FORCE_SCORE = 1e6
ROT_DIM = NSA_HD // 4
ROPE_THETA = 500000.0

PEER_HEADS = 8
PEER_NKEYS = 128
PEER_DKEY = 128
PEER_TOPK = 16

LANES = 128
SUBLANES = 8
VMEM_LIMIT_BYTES = 56 * 1024 * 1024
ROW_TILE = 512

NT = (((1,), (1,)), ((), ()))
TN = (((0,), (0,)), ((), ()))


def _round_up(n, m):
    return -(-n // m) * m


def _params(*sem):
    return pltpu.CompilerParams(dimension_semantics=sem, vmem_limit_bytes=VMEM_LIMIT_BYTES)


def _row_tile(t):
    return ROW_TILE if t % ROW_TILE == 0 else t


def _mod_operand(m, b, s, tm):
    d = m.shape[-1]
    if s % tm == 0:
        return m, pl.BlockSpec((None, 1, d), lambda i, *_: (i * tm // s, 0, 0))
    rows = jnp.broadcast_to(m, (b, s, d)).reshape(b * s, d)
    return rows, pl.BlockSpec((tm, d), lambda i, *_: (i, 0))


def _ada_kernel(c_ref, w_ref, b_ref, o_ref):
    c = c_ref[...]
    act = (c * jax.nn.sigmoid(c)).astype(BF16)
    o_ref[...] = jnp.dot(act, w_ref[...].astype(BF16), preferred_element_type=F32) + b_ref[...]


def ada_mod(c, w, b):
    r, d = c.shape
    nl, _, n = w.shape
    tn = n // 4
    return pl.pallas_call(
        _ada_kernel,
        out_shape=jax.ShapeDtypeStruct((nl, r, n), F32),
        grid=(nl, n // tn),
        in_specs=[pl.BlockSpec((r, d), lambda l, j: (0, 0)),
                  pl.BlockSpec((None, d, tn), lambda l, j: (l, 0, j)),
                  pl.BlockSpec((None, 1, tn), lambda l, j: (l, 0, j))],
        out_specs=pl.BlockSpec((None, r, tn), lambda l, j: (l, 0, j)),
        compiler_params=_params("parallel", "parallel"),
        name="ada_mod",
    )(c, w, b.reshape(nl, 1, n))


def _mod_matmul_kernel(x_ref, sc_ref, sh_ref, w_ref, o_ref, h_ref):
    @pl.when(pl.program_id(1) == 0)
    def _():
        h = x_ref[...] * (1.0 + sc_ref[...]) + sh_ref[...]
        h_ref[...] = h.astype(BF16)

    o_ref[...] = jnp.dot(h_ref[...], w_ref[...], preferred_element_type=F32)


def project(x, scale, shift, w, b, s):
    t, d = x.shape
    npad = _round_up(w.shape[1], LANES)
    wp = jnp.pad(w, ((0, 0), (0, npad - w.shape[1]))).astype(BF16)
    tn = npad
    tm = _row_tile(t)
    sc, mod_spec = _mod_operand(scale, b, s, tm)
    sh, _ = _mod_operand(shift, b, s, tm)
    return pl.pallas_call(
        _mod_matmul_kernel,
        out_shape=(jax.ShapeDtypeStruct((t, npad), F32), jax.ShapeDtypeStruct((t, d), BF16)),
        grid=(t // tm, npad // tn),
        in_specs=[pl.BlockSpec((tm, d), lambda i, j: (i, 0)), mod_spec, mod_spec,
                  pl.BlockSpec((d, tn), lambda i, j: (0, j))],
        out_specs=(pl.BlockSpec((tm, tn), lambda i, j: (i, j)), pl.BlockSpec((tm, d), lambda i, j: (i, 0))),
        compiler_params=_params("parallel", "arbitrary"),
        name="mod_matmul",
    )(x, sc, sh, wp)


def _deepnorm(x, sub, gate, g, b):
    y = DN_ALPHA * x + (1.0 + gate) * sub
    mu = jnp.mean(y, axis=-1, keepdims=True)
    var = jnp.mean(jnp.square(y - mu), axis=-1, keepdims=True)
    return (y - mu) * lax.rsqrt(var + LN_EPS) * g + b


def _post_gla_kernel(o_ref, r_ref, x_ref, gate_ref, gn_ref, w_ref, g_ref, b_ref, y_ref):
    parts = []
    for h in range(GLA_HEADS):
        cols = slice(h * GLA_DV, (h + 1) * GLA_DV)
        o = o_ref[:, cols]
        mu = jnp.mean(o, axis=-1, keepdims=True)
        var = jnp.mean(jnp.square(o - mu), axis=-1, keepdims=True)
        parts.append((o - mu) * lax.rsqrt(var + LN_EPS) * gn_ref[:, cols])
    r = r_ref[...]
    f = (jnp.concatenate(parts, axis=1) * (r * jax.nn.sigmoid(r))).astype(BF16)
    sub = jnp.dot(f, w_ref[...], preferred_element_type=F32)
    y_ref[...] = _deepnorm(x_ref[...], sub, gate_ref[...], g_ref[...], b_ref[...])


def _post_matmul_kernel(o_ref, x_ref, gate_ref, w_ref, g_ref, b_ref, y_ref):
    sub = jnp.dot(o_ref[...].astype(BF16), w_ref[...], preferred_element_type=F32)
    y_ref[...] = _deepnorm(x_ref[...], sub, gate_ref[...], g_ref[...], b_ref[...])


def _post_plain_kernel(o_ref, x_ref, gate_ref, g_ref, b_ref, y_ref):
    y_ref[...] = _deepnorm(x_ref[...], o_ref[...], gate_ref[...], g_ref[...], b_ref[...])


def post(x, gate, ln_g, ln_b, b, s, *, sub=None, w_out=None, gla_proj=None, gla_gn=None):
    t, d = x.shape
    tm = _row_tile(t)
    gate_arr, gate_spec = _mod_operand(gate, b, s, tm)
    row = pl.BlockSpec((tm, d), lambda i: (i, 0))
    vec = pl.BlockSpec((1, d), lambda i: (0, 0))
    mat = pl.BlockSpec((d, d), lambda i: (0, 0))
    g2, b2 = ln_g.reshape(1, d), ln_b.reshape(1, d)
    if gla_proj is not None:
        r_spec = pl.BlockSpec((tm, GLA_HV), lambda i: (i, (2 * GLA_HK + GLA_HV) // GLA_HV))
        args = (sub, gla_proj, x, gate_arr, gla_gn.reshape(1, d), w_out.astype(BF16), g2, b2)
        specs = [row, r_spec, row, gate_spec, vec, mat, vec, vec]
        body = _post_gla_kernel
    elif w_out is not None:
        args = (sub, x, gate_arr, w_out.astype(BF16), g2, b2)
        specs = [row, row, gate_spec, mat, vec, vec]
        body = _post_matmul_kernel
    else:
        args = (sub, x, gate_arr, g2, b2)
        specs = [row, row, gate_spec, vec, vec]
        body = _post_plain_kernel
    return pl.pallas_call(
        body,
        out_shape=jax.ShapeDtypeStruct((t, d), F32),
        grid=(t // tm,),
        in_specs=specs,
        out_specs=row,
        compiler_params=_params("parallel"),
        name="post",
    )(*args)


GLA_MIN_ROWS = 64


def _cumsum_rows(x):
    n = x.shape[0]
    row = lax.broadcasted_iota(jnp.int32, (n, 1), 0)
    shift = 1
    while shift < n:
        x = x + jnp.where(row >= shift, pltpu.roll(x, shift, 0), 0.0)
        shift *= 2
    return x


def _gla_kernel(has_s0, q_ref, k_ref, v_ref, a_ref, wa_ref, ba_ref, *rest):
    if has_s0:
        s0_ref, o_ref, st_ref, s_scr = rest
    else:
        o_ref, st_ref, s_scr = rest
    c = pl.program_id(1)
    n_rows = q_ref.shape[0]
    rows = max(n_rows, GLA_MIN_ROWS)

    @pl.when(c == 0)
    def _():
        s_scr[...] = s0_ref[...] if has_s0 else jnp.zeros_like(s_scr)

    def padded(x):
        if rows == n_rows:
            return x
        return jnp.concatenate([x, jnp.zeros((rows - n_rows, x.shape[1]), x.dtype)], axis=0)

    z = jnp.dot(a_ref[...], wa_ref[...], precision=HIGHEST, preferred_element_type=F32) + ba_ref[...]
    logg = (jnp.minimum(z, 0.0) - jnp.log1p(jnp.exp(-jnp.abs(z)))) / GLA_TAU
    b = _cumsum_rows(padded(logg))
    q, k, v = padded(q_ref[...]), padded(k_ref[...]), padded(v_ref[...])
    qe = q * (GLA_DK ** -0.5) * jnp.exp(b)
    ke = k * jnp.exp(-b)
    b_last = b[rows - 1:rows, :]
    kd = k * jnp.exp(b_last - b)
    e_last = jnp.exp(b_last)
    causal = lax.broadcasted_iota(jnp.int32, (rows, 1), 0) >= lax.broadcasted_iota(jnp.int32, (1, rows), 1)
    eye = lax.broadcasted_iota(jnp.int32, (GLA_DK, 1), 0) == lax.broadcasted_iota(jnp.int32, (1, GLA_DK), 1)
    for h in range(GLA_HEADS):
        ck = slice(h * GLA_DK, (h + 1) * GLA_DK)
        cv = slice(h * GLA_DV, (h + 1) * GLA_DV)
        att = lax.dot_general(qe[:, ck], ke[:, ck], NT, preferred_element_type=F32)
        att = jnp.where(causal, att, 0.0)
        s_h = s_scr[h]
        o = (jnp.dot(att, v[:, cv], preferred_element_type=F32)
             + jnp.dot(qe[:, ck], s_h, preferred_element_type=F32))
        o_ref[:, cv] = o[:n_rows]
        e_col = jnp.sum(jnp.where(eye, e_last[:, ck], 0.0), axis=1, keepdims=True)
        s_scr[h] = e_col * s_h + lax.dot_general(kd[:, ck], v[:, cv], TN, preferred_element_type=F32)

    @pl.when(c == pl.num_programs(1) - 1)
    def _():
        st_ref[...] = s_scr[...]


def gla_recurrence(proj, w_a2, b_a2, s0, b, s):
    chunk = GLA_CHUNK if s % GLA_CHUNK == 0 else s
    nc = s // chunk
    wa = jnp.pad(w_a2, ((0, LANES - w_a2.shape[0]), (0, 0)))
    a_block = (2 * GLA_HK + 2 * GLA_HV) // LANES
    row = lambda width, blk: pl.BlockSpec((chunk, width), lambda bi, c: (bi * nc + c, blk))
    st_spec = pl.BlockSpec((None, GLA_HEADS, GLA_DK, GLA_DV), lambda bi, c: (bi, 0, 0, 0))
    in_specs = [row(GLA_HK, 0), row(GLA_HK, 1), row(GLA_HV, 2 * GLA_HK // GLA_HV), row(LANES, a_block),
                pl.BlockSpec(wa.shape, lambda bi, c: (0, 0)), pl.BlockSpec((1, GLA_HK), lambda bi, c: (0, 0))]
    args = [proj, proj, proj, proj, wa, b_a2.reshape(1, GLA_HK)]
    if s0 is not None:
        in_specs.append(st_spec)
        args.append(s0)
    return pl.pallas_call(
        functools.partial(_gla_kernel, s0 is not None),
        out_shape=(jax.ShapeDtypeStruct((b * s, GLA_HV), F32),
                   jax.ShapeDtypeStruct((b, GLA_HEADS, GLA_DK, GLA_DV), F32)),
        grid=(b, nc),
        in_specs=in_specs,
        out_specs=(pl.BlockSpec((chunk, GLA_HV), lambda bi, c: (bi * nc + c, 0)), st_spec),
        scratch_shapes=[pltpu.VMEM((GLA_HEADS, GLA_DK, GLA_DV), F32)],
        compiler_params=_params("parallel", "arbitrary"),
        name="gla_recurrence",
    )(*args)


def _nsa_q_perm():
    cols = []
    for m in range(NSA_KV_HEADS // 2):
        for r in range(NSA_GROUP):
            for half in range(2):
                head = (2 * m + half) * NSA_GROUP + r
                cols.append(np.arange(NSA_HD) + head * NSA_HD)
    return np.concatenate(cols)


def _nsa_gate_expand():
    perm = _nsa_q_perm()
    ex = np.zeros((LANES, 3 * NSA_QD), np.float32)
    for br in range(3):
        for col in range(NSA_QD):
            ex[br * NSA_HEADS + perm[col] // NSA_HD, br * NSA_QD + col] = 1.0
    return ex


def _nsa_slc_matrix(nc_pad, nsb_pad):
    ratio = SEL_BLOCK // CMP_STRIDE
    m = np.zeros((nc_pad, nsb_pad), np.float32)
    for j in range(nsb_pad):
        for o in range(CMP_LEN // CMP_STRIDE):
            for i in range(ratio):
                n = ratio * j + i + o - (CMP_LEN // CMP_STRIDE - 1)
                if 0 <= n < nc_pad:
                    m[n, j] += 1.0
    return m


def _rope_tables(pos):
    half = ROT_DIM // 2
    inv = ROPE_THETA ** (-jnp.arange(half, dtype=F32) * 2.0 / ROT_DIM)
    ang = pos.astype(F32)[:, None] * inv[None, :]
    cos, sin = jnp.cos(ang), jnp.sin(ang)
    t = pos.shape[0]
    ones = jnp.ones((t, NSA_HD - ROT_DIM), F32)
    zeros = jnp.zeros((t, NSA_HD - ROT_DIM), F32)
    z8 = jnp.zeros((t, half), F32)
    c = jnp.concatenate([cos, cos, ones], 1)
    up = jnp.concatenate([-sin, z8, zeros], 1)
    dn = jnp.concatenate([z8, sin, zeros], 1)
    two = lambda a: jnp.concatenate([a, a], 1)
    return two(c), two(up), two(dn)


def _nsa_prep_kernel(p_ref, cos_ref, up_ref, dn_ref, q_ref, kc_ref, vc_ref, ks_ref, vs_ref, kw_ref, vw_ref,
                     ksb_ref, vsb_ref, kwb_ref, vwb_ref, g_ref):
    def rope(x):
        reps = x.shape[1] // LANES
        tile = lambda a: jnp.concatenate([a] * reps, axis=1)
        w = x.shape[1]
        return (x * tile(cos_ref[...]) + pltpu.roll(x, w - ROT_DIM // 2, 1) * tile(up_ref[...])
                + pltpu.roll(x, ROT_DIM // 2, 1) * tile(dn_ref[...]))

    q_ref[...] = (rope(p_ref[:, :NSA_QD]) * (NSA_HD ** -0.5)).astype(q_ref.dtype)
    kv = lambda k: p_ref[:, NSA_QD + k * NSA_KVD:NSA_QD + (k + 1) * NSA_KVD]
    kc_ref[...] = rope(kv(0))
    vc_ref[...] = kv(1)
    ks = rope(kv(2))
    ks_ref[...] = ks
    ksb_ref[...] = ks.astype(BF16)
    vs_ref[...] = kv(3)
    vsb_ref[...] = kv(3).astype(BF16)
    kw = rope(kv(4))
    kw_ref[...] = kw
    kwb_ref[...] = kw.astype(BF16)
    vw_ref[...] = kv(5)
    vwb_ref[...] = kv(5).astype(BF16)
    g0 = NSA_QD + 6 * NSA_KVD
    g_ref[...] = jax.nn.sigmoid(p_ref[:, g0:g0 + LANES])


def nsa_prep(p, cos, up, dn, *, q_dtype):
    t = p.shape[0]
    tm = _row_tile(t)
    period = cos.shape[0] // tm
    tab = pl.BlockSpec((tm, LANES), lambda i: (i % period, 0))
    kv32 = jax.ShapeDtypeStruct((t, NSA_KVD), F32)
    kv16 = jax.ShapeDtypeStruct((t, NSA_KVD), BF16)
    kvs = pl.BlockSpec((tm, NSA_KVD), lambda i: (i, 0))
    return pl.pallas_call(
        _nsa_prep_kernel,
        out_shape=(jax.ShapeDtypeStruct((t, NSA_QD), q_dtype),) + (kv32,) * 6 + (kv16,) * 4
        + (jax.ShapeDtypeStruct((t, LANES), F32),),
        grid=(t // tm,),
        in_specs=[pl.BlockSpec((tm, p.shape[1]), lambda i: (i, 0)), tab, tab, tab],
        out_specs=(pl.BlockSpec((tm, NSA_QD), lambda i: (i, 0)),) + (kvs,) * 10
        + (pl.BlockSpec((tm, LANES), lambda i: (i, 0)),),
        compiler_params=_params("parallel"),
        name="nsa_prep",
    )(p, cos, up, dn)


def _block_sums(x, w):
    n_sub = x.shape[0] // CMP_STRIDE
    x = x.reshape(n_sub, CMP_STRIDE, NSA_KVD)
    w = 1.0 + w
    return jnp.sum(x * w[None, :CMP_STRIDE], axis=1), jnp.sum(x * w[None, CMP_STRIDE:], axis=1)


def _nsa_compress_kernel(x_ref, w_ref, o_ref):
    first, second = _block_sums(x_ref[...], w_ref[...])
    n_sub = first.shape[0]
    nxt = pltpu.roll(second, n_sub - 1, 0)
    row = lax.broadcasted_iota(jnp.int32, (n_sub, 1), 0)
    o_ref[...] = jnp.where(row < n_sub - 1, (first + nxt) / CMP_LEN, 0.0).astype(o_ref.dtype)


def nsa_compress(x, pe, b, s):
    n_sub = s // CMP_STRIDE
    return pl.pallas_call(
        _nsa_compress_kernel,
        out_shape=jax.ShapeDtypeStruct((b, n_sub, NSA_KVD), BF16),
        grid=(b,),
        in_specs=[pl.BlockSpec((s, NSA_KVD), lambda i: (i, 0)),
                  pl.BlockSpec((CMP_LEN, NSA_KVD), lambda i: (0, 0))],
        out_specs=pl.BlockSpec((None, n_sub, NSA_KVD), lambda i: (i, 0, 0)),
        compiler_params=_params("parallel"),
        name="nsa_compress",
    )(x, pe.reshape(CMP_LEN, NSA_KVD))


NSA_KV_TILE = 512


def _mask_rows(s, allowed):
    if allowed.shape[0] == s.shape[0]:
        return jnp.where(allowed, s, -jnp.inf)
    reps = s.shape[0] // allowed.shape[0]
    s3 = s.reshape(reps, allowed.shape[0], s.shape[1])
    return jnp.where(allowed[None], s3, -jnp.inf).reshape(s.shape)


def _softmax_rows(s, allowed):
    s = _mask_rows(s, allowed)
    m = jnp.max(s, axis=1, keepdims=True)
    m = jnp.where(m == -jnp.inf, 0.0, m)
    e = jnp.exp(s - m)
    return e / jnp.maximum(jnp.sum(e, axis=1, keepdims=True), 1e-30)


def _select_blocks(score, n_sel):
    nb = score.shape[1]
    lane = lax.broadcasted_iota(jnp.int32, score.shape, 1).astype(F32)
    sel = jnp.zeros(score.shape, F32)
    for _ in range(n_sel):
        m = jnp.max(score, axis=1, keepdims=True)
        first = jnp.min(jnp.where(score == m, lane, float(nb)), axis=1, keepdims=True)
        pick = lane == first
        sel = jnp.where(pick, 1.0, sel)
        score = jnp.where(pick, -jnp.inf, score)
    return sel


def _block_scores(p_slc, pos_q):
    jb = lax.broadcasted_iota(jnp.int32, (1, p_slc.shape[1]), 1)
    cur = pos_q // SEL_BLOCK
    valid = jb * SEL_BLOCK <= pos_q
    forced = (jb == 0) | (jb == cur) | (jb == cur - 1)
    return jnp.where(forced, FORCE_SCORE, jnp.where(valid, p_slc, -1.0))


def _select_blocks_t(score, n_sel):
    st = score.T
    nb = st.shape[0]
    blk = lax.broadcasted_iota(jnp.int32, st.shape, 0).astype(F32)
    sel = jnp.zeros(st.shape, F32)
    for _ in range(n_sel):
        m = jnp.max(st, axis=0, keepdims=True)
        first = jnp.min(jnp.where(st == m, blk, float(nb)), axis=0, keepdims=True)
        pick = blk == first
        sel = jnp.where(pick, 1.0, sel)
        st = jnp.where(pick, -jnp.inf, st)
    return sel.T


def _online_softmax_step(s, allowed, v, m_old, l_old, acc_old, feature_major=False):
    s = _mask_rows(s, allowed)
    m_new = jnp.maximum(m_old, jnp.max(s, axis=1, keepdims=True))
    m_use = jnp.where(m_new == -jnp.inf, 0.0, m_new)
    alpha = jnp.exp(m_old - m_use)
    p = jnp.exp(s - m_use)
    l_new = alpha * l_old + jnp.sum(p, axis=1, keepdims=True)
    if feature_major:
        pv = lax.dot_general(p.astype(BF16), v, NT, preferred_element_type=F32)
    else:
        pv = jnp.dot(p.astype(BF16), v, preferred_element_type=F32)
    return m_new, l_new, alpha * acc_old + pv


def _padded_query(tile, m, in_half):
    sel = jnp.where(in_half, tile, jnp.zeros_like(tile))
    z = jnp.zeros_like(sel)
    return jnp.concatenate([sel, z] if m == 0 else [z, sel], axis=1)


def _nsa_attn_kernel(q_ref, g_ref, kcmp_ref, vcmp_ref, ks_ref, vs_ref, kw_ref, vw_ref, slc_ref, ex_ref,
                     o_ref, qz_ref, sel_ref, oc_ref, m_ref, l_ref, acc_ref):
    i = pl.program_id(1)
    start = i * Q_BLOCK
    pos_q = start + lax.broadcasted_iota(jnp.int32, (Q_BLOCK, 1), 0)
    lane128 = lax.broadcasted_iota(jnp.int32, (1, LANES), 1)
    n_cmp = kcmp_ref.shape[1]
    cmp_end = CMP_STRIDE * lax.broadcasted_iota(jnp.int32, (1, n_cmp), 1) + (CMP_LEN - 1)
    cmp_ok = cmp_end <= pos_q
    gexp = jnp.dot(g_ref[...], ex_ref[...], precision=HIGHEST, preferred_element_type=F32)
    o_ref[...] = jnp.zeros_like(o_ref)

    win_base = pl.multiple_of(jnp.maximum(start - WINDOW, 0), Q_BLOCK)
    band = WINDOW + Q_BLOCK
    kw_t = jnp.concatenate([kw_ref[win_base // Q_BLOCK + u] for u in range(band // Q_BLOCK)], axis=1)
    vw_t = vw_ref[pl.ds(win_base, band), :]
    dpos = pos_q - (win_base + lax.broadcasted_iota(jnp.int32, (1, band), 1))
    win_ok = (dpos >= 0) & (dpos < WINDOW)

    for g in range(NSA_KV_HEADS):
        m, half = divmod(g, 2)
        in_half = (lane128 // NSA_HD) == half
        qz = jnp.concatenate(
            [_padded_query(q_ref[:, (m * NSA_GROUP + r) * LANES:(m * NSA_GROUP + r + 1) * LANES], m, in_half)
             for r in range(NSA_GROUP)], axis=0)
        qz_ref[g] = qz
        s = jnp.dot(qz, kcmp_ref[...], preferred_element_type=F32)
        p = _softmax_rows(s, cmp_ok)
        oc_ref[g] = jnp.dot(p.astype(BF16), vcmp_ref[...], preferred_element_type=F32)[:, m * LANES:(m + 1) * LANES]
        imp = p[0:Q_BLOCK]
        for r in range(1, NSA_GROUP):
            imp = imp + p[r * Q_BLOCK:(r + 1) * Q_BLOCK]
        p_slc = jnp.dot(imp, slc_ref[...], precision=HIGHEST, preferred_element_type=F32)
        sel_ref[g] = _select_blocks_t(_block_scores(p_slc, pos_q), N_SEL).astype(BF16)

    m_ref[...] = jnp.full_like(m_ref, -jnp.inf)
    l_ref[...] = jnp.zeros_like(l_ref)
    acc_ref[...] = jnp.zeros_like(acc_ref)

    def kv_step(t, carry):
        k0 = pl.multiple_of(t * NSA_KV_TILE, NSA_KV_TILE)
        k_t = ks_ref[t]
        v_t = vs_ref[pl.ds(k0, NSA_KV_TILE), :]
        key = k0 + lax.broadcasted_iota(jnp.int32, (1, NSA_KV_TILE), 1)
        blk = lax.broadcasted_iota(jnp.int32, (LANES, 1), 0)
        expand = jnp.where(blk == key // SEL_BLOCK, 1.0, 0.0).astype(BF16)
        causal = key <= pos_q
        for g in range(NSA_KV_HEADS):
            chosen = jnp.dot(sel_ref[g], expand, preferred_element_type=F32)
            s = jnp.dot(qz_ref[g], k_t, preferred_element_type=F32)
            m_ref[g], l_ref[g], acc_ref[g] = _online_softmax_step(s, (chosen > 0.5) & causal, v_t,
                                                                  m_ref[g], l_ref[g], acc_ref[g])
        return carry

    lax.fori_loop(0, (start + Q_BLOCK + NSA_KV_TILE - 1) // NSA_KV_TILE, kv_step, 0)

    for g in range(NSA_KV_HEADS):
        m, half = divmod(g, 2)
        in_half = (lane128 // NSA_HD) == half
        mcols = slice(m * LANES, (m + 1) * LANES)
        o_s = acc_ref[g][:, mcols] / jnp.maximum(l_ref[g], 1e-30)
        s = jnp.dot(qz_ref[g], kw_t, preferred_element_type=F32)
        o_w = jnp.dot(_softmax_rows(s, win_ok).astype(BF16), vw_t, preferred_element_type=F32)[:, mcols]
        o_c = oc_ref[g]
        for r in range(NSA_GROUP):
            t = m * NSA_GROUP + r
            cols = slice(t * LANES, (t + 1) * LANES)
            rows = slice(r * Q_BLOCK, (r + 1) * Q_BLOCK)
            comb = (gexp[:, cols] * o_c[rows] + gexp[:, NSA_QD + t * LANES:NSA_QD + (t + 1) * LANES] * o_s[rows]
                    + gexp[:, 2 * NSA_QD + t * LANES:2 * NSA_QD + (t + 1) * LANES] * o_w[rows])
            o_ref[:, cols] += jnp.where(in_half, comb, 0.0)


def nsa_attn_prompt(q, gates, kcmp, vcmp, ks, vs, kw, vw, b, s):
    nq = s // Q_BLOCK
    rows = NSA_GROUP * Q_BLOCK
    slc = jnp.asarray(_nsa_slc_matrix(s // CMP_STRIDE, LANES))
    ex = jnp.asarray(_nsa_gate_expand())
    seq = pl.BlockSpec((s, NSA_KVD), lambda bi, i: (bi, 0))
    n_cmp = s // CMP_STRIDE
    cmp_spec = pl.BlockSpec((None, n_cmp, NSA_KVD), lambda bi, i: (bi, 0, 0))
    cmp_t_spec = pl.BlockSpec((None, NSA_KVD, n_cmp), lambda bi, i: (bi, 0, 0))

    def key_tiles(k, tile):
        kt = k.reshape(b, s // tile, tile, NSA_KVD).transpose(0, 1, 3, 2)
        return kt, pl.BlockSpec((None, s // tile, NSA_KVD, tile), lambda bi, i: (bi, 0, 0, 0))

    ks, ks_spec = key_tiles(ks, NSA_KV_TILE)
    kw, kw_spec = key_tiles(kw, Q_BLOCK)
    kcmp = kcmp.transpose(0, 2, 1)
    return pl.pallas_call(
        _nsa_attn_kernel,
        out_shape=jax.ShapeDtypeStruct((b * s, NSA_QD), F32),
        grid=(b, nq),
        in_specs=[
            pl.BlockSpec((Q_BLOCK, NSA_QD), lambda bi, i: (bi * nq + i, 0)),
            pl.BlockSpec((Q_BLOCK, LANES), lambda bi, i: (bi * nq + i, 0)),
            cmp_t_spec, cmp_spec, ks_spec, seq, kw_spec, seq,
            pl.BlockSpec(slc.shape, lambda bi, i: (0, 0)),
            pl.BlockSpec(ex.shape, lambda bi, i: (0, 0)),
        ],
        out_specs=pl.BlockSpec((Q_BLOCK, NSA_QD), lambda bi, i: (bi * nq + i, 0)),
        scratch_shapes=[pltpu.VMEM((NSA_KV_HEADS, rows, NSA_KVD), BF16),
                        pltpu.VMEM((NSA_KV_HEADS, Q_BLOCK, LANES), BF16),
                        pltpu.VMEM((NSA_KV_HEADS, rows, LANES), F32),
                        pltpu.VMEM((NSA_KV_HEADS, rows, 1), F32),
                        pltpu.VMEM((NSA_KV_HEADS, rows, 1), F32),
                        pltpu.VMEM((NSA_KV_HEADS, rows, NSA_KVD), F32)],
        compiler_params=_params("parallel", "arbitrary"),
        name="nsa_attn_prompt",
    )(q, gates, kcmp, vcmp, ks, vs, kw, vw, slc, ex)


PAGES_PER_STEP = 8
SUM_PAGES_PER_STEP = 16


def _feature_major(cache):
    n, tokens = cache.shape[:2]
    return cache.transpose(0, 2, 3, 1).reshape(n, NSA_KVD, tokens)


def _page_specs(n, page):
    def one(u):
        return pl.BlockSpec((None, NSA_KVD, page), lambda bi, j, pt: (pt[bi, j * n + u], 0, 0))
    return [one(u) for u in range(n)]


def _nsa_page_sums_kernel(pt_ref, *refs):
    n = SUM_PAGES_PER_STEP
    kp, vp = refs[:n], refs[n:2 * n]
    wk1_ref, wk2_ref, wv1_ref, wv2_ref, grp_ref, fk_ref, sk_ref, fv_ref, sv_ref = refs[2 * n:]

    def sums(pages, w_ref, o_ref):
        xw = jnp.concatenate([(p[...] * w_ref[...]).astype(BF16) for p in pages], axis=1)
        o_ref[...] = jnp.dot(xw, grp_ref[...], preferred_element_type=F32)

    sums(kp, wk1_ref, fk_ref)
    sums(kp, wk2_ref, sk_ref)
    sums(vp, wv1_ref, fv_ref)
    sums(vp, wv2_ref, sv_ref)


def nsa_page_sums(page_table, pool_k, pool_v, pe_k, pe_v):
    db, n_pages = page_table.shape
    page = pool_k.shape[2]
    n = SUM_PAGES_PER_STEP
    per_page = page // CMP_STRIDE
    assert n * per_page == LANES
    out = jax.ShapeDtypeStruct((db, NSA_KVD, n_pages * per_page), F32)
    out_spec = pl.BlockSpec((None, NSA_KVD, LANES), lambda bi, j, pt: (bi, 0, j))
    w_spec = pl.BlockSpec((NSA_KVD, page), lambda bi, j, pt: (0, 0))
    halves = lambda pe: [jnp.tile((1.0 + pe.reshape(CMP_LEN, NSA_KVD)[o:o + CMP_STRIDE]).T, (1, per_page))
                         for o in (0, CMP_STRIDE)]
    grp = np.zeros((n * page, LANES), np.float32)
    tok = np.arange(n * page)
    grp[tok, tok // CMP_STRIDE] = 1.0
    grp = jnp.asarray(grp, BF16)
    return pl.pallas_call(
        _nsa_page_sums_kernel,
        out_shape=(out,) * 4,
        grid_spec=pltpu.PrefetchScalarGridSpec(
            num_scalar_prefetch=1,
            grid=(db, n_pages // n),
            in_specs=_page_specs(n, page) * 2 + [w_spec] * 4 + [pl.BlockSpec(grp.shape, lambda bi, j, pt: (0, 0))],
            out_specs=(out_spec,) * 4,
        ),
        compiler_params=_params("parallel", "arbitrary"),
        name="nsa_page_sums",
    )(page_table, *([pool_k] * n), *([pool_v] * n), *halves(pe_k), *halves(pe_v), grp)


def _decode_queries(q_ref, g):
    m, half = divmod(g, 2)
    in_half = (lax.broadcasted_iota(jnp.int32, (1, LANES), 1) // NSA_HD) == half
    tiles = [_padded_query(q_ref[:, (m * NSA_GROUP + r) * LANES:(m * NSA_GROUP + r + 1) * LANES], m, in_half)
             for r in range(NSA_GROUP)]
    return jnp.concatenate(tiles, axis=0).astype(BF16)


def _group_rows(x):
    return jnp.concatenate([x] * NSA_GROUP, axis=0)


def _nsa_decode_select_kernel(past, q_ref, fk_ref, sk_ref, fv_ref, sv_ref, kn_ref, vn_ref, wk_ref, wv_ref,
                              slc_ref, sel_ref, oc_ref):
    nq = q_ref.shape[0]
    n_cmp = fk_ref.shape[1]
    col = lax.broadcasted_iota(jnp.int32, (1, n_cmp), 1)
    eye = (lax.broadcasted_iota(jnp.int32, (NSA_KVD, 1), 0)
           == lax.broadcasted_iota(jnp.int32, (1, NSA_KVD), 1))

    def summaries(f_ref, s_ref, new_ref, w_ref):
        w = 1.0 + w_ref[CMP_STRIDE:CMP_STRIDE + nq, :]
        second_new = jnp.sum(new_ref[...] * w, axis=0, keepdims=True)
        new_col = jnp.sum(jnp.where(eye, second_new, 0.0), axis=1, keepdims=True)
        nxt = jnp.where(col == n_cmp - 1, new_col, pltpu.roll(s_ref[...], n_cmp - 1, 1))
        return ((f_ref[...] + nxt) / CMP_LEN).astype(BF16)

    kcmp_t = summaries(fk_ref, sk_ref, kn_ref, wk_ref)
    vcmp_t = summaries(fv_ref, sv_ref, vn_ref, wv_ref)
    pos_q = past + lax.broadcasted_iota(jnp.int32, (nq, 1), 0)
    cmp_end = CMP_STRIDE * lax.broadcasted_iota(jnp.int32, (1, n_cmp), 1) + (CMP_LEN - 1)
    cmp_ok = _group_rows(cmp_end <= pos_q)
    lane128 = lax.broadcasted_iota(jnp.int32, (1, LANES), 1)
    oc_ref[...] = jnp.zeros_like(oc_ref)
    for g in range(NSA_KV_HEADS):
        m, half = divmod(g, 2)
        in_half = (lane128 // NSA_HD) == half
        s = jnp.dot(_decode_queries(q_ref, g), kcmp_t, preferred_element_type=F32)
        p = _softmax_rows(s, cmp_ok)
        o_c = lax.dot_general(p.astype(BF16), vcmp_t, NT,
                              preferred_element_type=F32)[:, m * LANES:(m + 1) * LANES]
        imp = p[0:nq]
        for r in range(1, NSA_GROUP):
            imp = imp + p[r * nq:(r + 1) * nq]
        for r in range(NSA_GROUP):
            cols = slice((m * NSA_GROUP + r) * LANES, (m * NSA_GROUP + r + 1) * LANES)
            oc_ref[:, cols] += jnp.where(in_half, o_c[r * nq:(r + 1) * nq], 0.0)
        p_slc = jnp.dot(imp, slc_ref[...], precision=HIGHEST, preferred_element_type=F32)
        sel = _group_rows(_select_blocks(_block_scores(p_slc, pos_q), N_SEL)).astype(BF16)
        rows = slice(g * NSA_GROUP * nq, (g + 1) * NSA_GROUP * nq)
        for lt in range(sel_ref.shape[0]):
            sel_ref[lt, rows, :] = sel[:, lt * LANES:(lt + 1) * LANES]


def nsa_decode_select(q, sums, kc_new, vc_new, pe_k, pe_v, db, nq, past):
    n_cmp = sums[0].shape[2]
    nsb_pad = _round_up(-(-(past + nq) // SEL_BLOCK), LANES)
    sel_rows = NSA_KV_HEADS * NSA_GROUP * nq
    slc = jnp.asarray(_nsa_slc_matrix(n_cmp, nsb_pad))
    cmp_spec = pl.BlockSpec((None, NSA_KVD, n_cmp), lambda bi: (bi, 0, 0))
    new_spec = pl.BlockSpec((nq, NSA_KVD), lambda bi: (bi, 0))
    pe_spec = pl.BlockSpec((CMP_LEN, NSA_KVD), lambda bi: (0, 0))
    return pl.pallas_call(
        functools.partial(_nsa_decode_select_kernel, past),
        out_shape=(jax.ShapeDtypeStruct((db, nsb_pad // LANES, sel_rows, LANES), BF16),
                   jax.ShapeDtypeStruct((db * nq, NSA_QD), F32)),
        grid=(db,),
        in_specs=[pl.BlockSpec((nq, NSA_QD), lambda bi: (bi, 0)), cmp_spec, cmp_spec, cmp_spec, cmp_spec,
                  new_spec, new_spec, pe_spec, pe_spec, pl.BlockSpec(slc.shape, lambda bi: (0, 0))],
        out_specs=(pl.BlockSpec((None, nsb_pad // LANES, sel_rows, LANES), lambda bi: (bi, 0, 0, 0)),
                   pl.BlockSpec((nq, NSA_QD), lambda bi: (bi, 0))),
        compiler_params=_params("parallel"),
        name="nsa_decode_select",
    )(q, *sums, kc_new, vc_new, pe_k.reshape(CMP_LEN, NSA_KVD), pe_v.reshape(CMP_LEN, NSA_KVD), slc)


def _nsa_decode_attend_kernel(past, pt_ref, q_ref, sel_ref, g_ref, oc_ref, ksn_ref, vsn_ref, wk_ref, wv_ref,
                              kwn_ref, vwn_ref, ex_ref, xp_ref, *refs):
    n = PAGES_PER_STEP
    kp, vp = refs[:n], refs[n:2 * n]
    o_ref, qz_scr, m_scr, l_scr, acc_scr = refs[2 * n:]
    j = pl.program_id(1)
    nq = q_ref.shape[0]
    n_keys = n * kp[0].shape[1]
    steps_per_tile = xp_ref.shape[0]
    pos_q = jnp.concatenate([past + lax.broadcasted_iota(jnp.int32, (nq, 1), 0)] * (NSA_KV_HEADS * NSA_GROUP),
                            axis=0)

    @pl.when(j == 0)
    def _():
        m_scr[...] = jnp.full_like(m_scr, -jnp.inf)
        l_scr[...] = jnp.zeros_like(l_scr)
        acc_scr[...] = jnp.zeros_like(acc_scr)
        qz_scr[...] = jnp.concatenate([_decode_queries(q_ref, g) for g in range(NSA_KV_HEADS)], axis=0)

    def attend(s, allowed, v, feature_major):
        m_scr[...], l_scr[...], acc_scr[...] = _online_softmax_step(
            s, allowed, v, m_scr[...], l_scr[...], acc_scr[...], feature_major)

    k_t = jnp.concatenate([r[...].astype(BF16) for r in kp], axis=1)
    v_t = jnp.concatenate([r[...].astype(BF16) for r in vp], axis=1)
    key = j * n_keys + lax.broadcasted_iota(jnp.int32, (1, n_keys), 1)
    chosen = jnp.dot(sel_ref[j // steps_per_tile], xp_ref[j % steps_per_tile], preferred_element_type=F32)
    attend(jnp.dot(qz_scr[...], k_t, preferred_element_type=F32), (chosen > 0.5) & (key <= pos_q), v_t, True)

    @pl.when(j == pl.num_programs(1) - 1)
    def _():
        qz = qz_scr[...]
        lane128 = lax.broadcasted_iota(jnp.int32, (1, LANES), 1)
        n_new = ksn_ref.shape[0]
        new_idx = lax.broadcasted_iota(jnp.int32, (1, n_new), 1)
        new_key = past + new_idx
        new_blk = past // SEL_BLOCK
        chosen_new = sel_ref[new_blk // LANES][:, new_blk % LANES:new_blk % LANES + 1].astype(F32) > 0.5
        attend(lax.dot_general(qz, ksn_ref[...], NT, preferred_element_type=F32),
               chosen_new & (new_key <= pos_q) & (new_idx < nq), vsn_ref[...], False)
        o_s = acc_scr[...] / jnp.maximum(l_scr[...], 1e-30)
        n_win = wk_ref.shape[1]
        k_wc, v_wc = wk_ref[...].astype(BF16), wv_ref[...].astype(BF16)
        pos_wc = past - n_win + lax.broadcasted_iota(jnp.int32, (1, n_win), 1)
        ok_wc = (pos_q - pos_wc >= 0) & (pos_q - pos_wc < WINDOW) & (pos_wc >= 0)
        ok_wn = (pos_q - new_key >= 0) & (pos_q - new_key < WINDOW) & (new_idx < nq)
        win = (jnp.full(m_scr.shape, -jnp.inf, F32), jnp.zeros(l_scr.shape, F32), jnp.zeros(acc_scr.shape, F32))
        win = _online_softmax_step(jnp.dot(qz, k_wc, preferred_element_type=F32), ok_wc, v_wc, *win, True)
        win = _online_softmax_step(lax.dot_general(qz, kwn_ref[...], NT, preferred_element_type=F32), ok_wn,
                                   vwn_ref[...], *win, False)
        o_w = win[2] / jnp.maximum(win[1], 1e-30)
        gexp = jnp.dot(g_ref[...], ex_ref[...], precision=HIGHEST, preferred_element_type=F32)
        o_ref[...] = gexp[:, :NSA_QD] * oc_ref[...]
        for g in range(NSA_KV_HEADS):
            m, half = divmod(g, 2)
            in_half = (lane128 // NSA_HD) == half
            mcols = slice(m * LANES, (m + 1) * LANES)
            for r in range(NSA_GROUP):
                t = m * NSA_GROUP + r
                cols = slice(t * LANES, (t + 1) * LANES)
                rows = slice((g * NSA_GROUP + r) * nq, (g * NSA_GROUP + r + 1) * nq)
                comb = (gexp[:, NSA_QD + t * LANES:NSA_QD + (t + 1) * LANES] * o_s[rows, mcols]
                        + gexp[:, 2 * NSA_QD + t * LANES:2 * NSA_QD + (t + 1) * LANES] * o_w[rows, mcols])
                o_ref[:, cols] += jnp.where(in_half, comb, 0.0)


def nsa_decode_attend(page_table, q, sel, gates, o_c, ks_new, vs_new, wk, wv, kw_new, vw_new, pool_k, pool_v,
                      db, nq, past):
    n_pages = page_table.shape[1]
    page = pool_k.shape[2]
    ex = jnp.asarray(_nsa_gate_expand())
    n_win = wk.shape[2]
    rows = NSA_KV_HEADS * NSA_GROUP * nq
    n_keys = PAGES_PER_STEP * page
    assert past % SEL_BLOCK + nq <= SEL_BLOCK and (LANES * SEL_BLOCK) % n_keys == 0
    steps_per_tile = LANES * SEL_BLOCK // n_keys
    xp = np.zeros((steps_per_tile, LANES, n_keys), np.float32)
    keys = np.arange(n_keys)
    for u in range(steps_per_tile):
        xp[u, u * (n_keys // SEL_BLOCK) + keys // SEL_BLOCK, keys] = 1.0
    xp = jnp.asarray(xp, BF16)
    per_b = lambda shape: pl.BlockSpec((None,) + shape, lambda bi, j, pt: (bi,) + (0,) * len(shape))
    q_rows = lambda width: pl.BlockSpec((nq, width), lambda bi, j, pt: (bi, 0))
    new_rows = ks_new.shape[1]
    return pl.pallas_call(
        functools.partial(_nsa_decode_attend_kernel, past),
        out_shape=jax.ShapeDtypeStruct((db * nq, NSA_QD), F32),
        grid_spec=pltpu.PrefetchScalarGridSpec(
            num_scalar_prefetch=1,
            grid=(db, n_pages // PAGES_PER_STEP),
            in_specs=[q_rows(NSA_QD), per_b(sel.shape[1:]), q_rows(LANES), q_rows(NSA_QD),
                      per_b((new_rows, NSA_KVD)), per_b((new_rows, NSA_KVD)),
                      per_b((NSA_KVD, n_win)), per_b((NSA_KVD, n_win)),
                      per_b((new_rows, NSA_KVD)), per_b((new_rows, NSA_KVD)),
                      pl.BlockSpec(ex.shape, lambda bi, j, pt: (0, 0)),
                      pl.BlockSpec(xp.shape, lambda bi, j, pt: (0, 0, 0))]
            + _page_specs(PAGES_PER_STEP, page) * 2,
            out_specs=q_rows(NSA_QD),
            scratch_shapes=[pltpu.VMEM((rows, NSA_KVD), BF16), pltpu.VMEM((rows, 1), F32),
                            pltpu.VMEM((rows, 1), F32), pltpu.VMEM((rows, NSA_KVD), F32)],
        ),
        compiler_params=_params("parallel", "arbitrary"),
        name="nsa_decode_attend",
    )(page_table, q, sel, gates, o_c, ks_new, vs_new, wk, wv, kw_new, vw_new, ex, xp,
      *([pool_k] * PAGES_PER_STEP), *([pool_v] * PAGES_PER_STEP))


def _nsa_weights(w_in, w_out):
    perm = _nsa_q_perm()
    return jnp.concatenate([w_in[:, perm], w_in[:, NSA_QD:]], axis=1), w_out[perm]


def nsa_prompt(x, scale, shift, w_in_p, pe_k, pe_v, b, s):
    p, _ = project(x, scale, shift, w_in_p, b, s)
    cos, up, dn = _rope_tables(jnp.arange(s, dtype=jnp.int32))
    q, kc, vc, ks, vs, kw, vw, ksb, vsb, kwb, vwb, gates = nsa_prep(p, cos, up, dn, q_dtype=BF16)
    kcmp = nsa_compress(kc, pe_k, b, s)
    vcmp = nsa_compress(vc, pe_v, b, s)
    o = nsa_attn_prompt(q, gates, kcmp, vcmp, ksb, vsb, kwb, vwb, b, s)
    nw = min(WINDOW, s)
    rs = lambda a: a.reshape(b, s, NSA_KV_HEADS, NSA_HD)
    return o, (rs(kc), rs(vc), rs(ks), rs(vs), rs(kw)[:, s - nw:], rs(vw)[:, s - nw:])


def nsa_sample(x, scale, shift, ck, cv, sk, sv, wk, wv, page_table, w_in_p, pe_k, pe_v, db, nq):
    past = page_table.shape[1] * ck.shape[1]
    p, _ = project(x, scale, shift, w_in_p, db, nq)
    pos = past + jnp.tile(jnp.arange(nq, dtype=jnp.int32), db)
    cos, up, dn = _rope_tables(pos)
    q, kc, vc, ks, vs, kw, vw, ksb, vsb, kwb, vwb, gates = nsa_prep(p, cos, up, dn, q_dtype=F32)
    sums = nsa_page_sums(page_table, _feature_major(ck), _feature_major(cv), pe_k, pe_v)
    sel, o_c = nsa_decode_select(q, sums, kc, vc, pe_k, pe_v, db, nq, past)
    new_pad = lambda a: jnp.pad(a.reshape(db, nq, NSA_KVD), ((0, 0), (0, LANES - nq), (0, 0)))
    o = nsa_decode_attend(page_table, q, sel, gates, o_c, new_pad(ksb), new_pad(vsb),
                          _feature_major(wk), _feature_major(wv), new_pad(kwb), new_pad(vwb),
                          _feature_major(sk), _feature_major(sv), db, nq, past)
    rs = lambda a: a.reshape(db, nq, NSA_KV_HEADS, NSA_HD)
    slide = lambda cache, new: jnp.concatenate([cache, rs(new).astype(cache.dtype)], 1)[:, nq:]
    return o, (rs(kc), rs(vc), rs(ks), rs(vs), slide(wk, kw), slide(wv, vw))


def _top_values(s, k):
    n = s.shape[0]
    row = lax.broadcasted_iota(jnp.int32, (n, 1), 0).astype(F32)
    vals = []
    for _ in range(k):
        m = jnp.max(s, axis=0, keepdims=True)
        first = jnp.min(jnp.where(s == m, row, float(n)), axis=0, keepdims=True)
        vals.append(m)
        s = jnp.where(row == first, -jnp.inf, s)
    return vals, jnp.zeros_like(vals[0])


def _top_values_distinct(s, k):
    gone_before = jnp.sum(jnp.where(s == -jnp.inf, 1.0, 0.0), axis=0, keepdims=True)
    vals = []
    for _ in range(k):
        m = jnp.max(s, axis=0, keepdims=True)
        vals.append(m)
        s = jnp.where(s >= m, -jnp.inf, s)
    gone = jnp.sum(jnp.where(s == -jnp.inf, 1.0, 0.0), axis=0, keepdims=True)
    return vals, gone - gone_before - float(k)


def _peer_ranks(s1, s2, top_fn):
    n_rank = PEER_TOPK + 1
    top1, extra1 = top_fn(s1, n_rank)
    top2, extra2 = top_fn(s2, n_rank)
    t1 = jnp.concatenate(top1[:PEER_TOPK], axis=0)
    t2 = jnp.concatenate(top2[:PEER_TOPK], axis=0)
    row8 = lax.broadcasted_iota(jnp.int32, (SUBLANES, 1), 0)
    ends = jnp.where(row8 == 0, top1[0] + top2[PEER_TOPK],
                     jnp.where(row8 == 1, top1[PEER_TOPK] + top2[0], -jnp.inf))
    pairs = [top1[0] + t2, top1[1] + t2[:SUBLANES], t1[SUBLANES:] + top2[0], ends]
    for i in range(2, SUBLANES):
        pairs.append(jnp.where(row8 < n_rank // (i + 1), top1[i] + t2[:SUBLANES], -jnp.inf))
    best, extra3 = top_fn(jnp.concatenate(pairs, axis=0), n_rank)
    z = best[0] * 0.0
    for v in best[:PEER_TOPK]:
        z = z + jnp.exp(v - best[0])
    cut = 0.5 * (best[PEER_TOPK - 1] + best[PEER_TOPK])
    return jnp.exp(s1 - top1[0]), jnp.exp(s2 - top2[0]) / z, cut, extra1 + extra2 + extra3


def _peer_route_kernel(qv_ref, keys_ref, s1_ref, s2_ref, e1_ref, e2_ref, th_ref):
    nc = qv_ref.shape[0] // LANES

    def store(h, e1, e2, cut):
        for c in range(nc):
            cols = slice(c * LANES, (c + 1) * LANES)
            e1_ref[h, c] = e1[:, cols]
            e2_ref[h, c] = e2[:, cols]
            th_ref[h, c] = cut[:, cols]

    repeats = None
    for h in range(PEER_HEADS):
        q1 = qv_ref[:, (2 * h) * PEER_DKEY:(2 * h + 1) * PEER_DKEY]
        q2 = qv_ref[:, (2 * h + 1) * PEER_DKEY:(2 * h + 2) * PEER_DKEY]
        s1 = lax.dot_general(keys_ref[h, 0], q1, NT, preferred_element_type=F32)
        s2 = lax.dot_general(keys_ref[h, 1], q2, NT, preferred_element_type=F32)
        for c in range(nc):
            s1_ref[h, c] = s1[:, c * LANES:(c + 1) * LANES]
            s2_ref[h, c] = s2[:, c * LANES:(c + 1) * LANES]
        e1, e2, cut, rep = _peer_ranks(s1, s2, _top_values_distinct)
        store(h, e1, e2, cut)
        repeats = rep if repeats is None else jnp.maximum(repeats, rep)

    @pl.when(jnp.max(repeats) > 0.5)
    def _():
        for h in range(PEER_HEADS):
            s1 = jnp.concatenate([s1_ref[h, c] for c in range(nc)], axis=1)
            s2 = jnp.concatenate([s2_ref[h, c] for c in range(nc)], axis=1)
            store(h, *_peer_ranks(s1, s2, _top_values)[:3])


def peer_route(qv, keys, *, tm):
    t = qv.shape[0]
    nc = tm // LANES
    big = jax.ShapeDtypeStruct((PEER_HEADS, t // LANES, PEER_NKEYS, LANES), F32)
    big_spec = pl.BlockSpec((PEER_HEADS, nc, PEER_NKEYS, LANES), lambda i: (0, i, 0, 0))
    return pl.pallas_call(
        _peer_route_kernel,
        out_shape=(big, big, big, big, jax.ShapeDtypeStruct((PEER_HEADS, t // LANES, 1, LANES), F32)),
        grid=(t // tm,),
        in_specs=[
            pl.BlockSpec((tm, 2 * PEER_HEADS * PEER_DKEY), lambda i: (i, 0)),
            pl.BlockSpec(keys.shape, lambda i: (0, 0, 0, 0)),
        ],
        out_specs=(big_spec, big_spec, big_spec, big_spec,
                   pl.BlockSpec((PEER_HEADS, nc, 1, LANES), lambda i: (0, i, 0, 0))),
        compiler_params=_params("parallel"),
        name="peer_route",
    )(qv, keys)


PEER_A_PER_STEP = 2 * SUBLANES
PEER_ROUTE_TILE = 256


def _gelu_tanh(x):
    return 0.5 * x * (1.0 + jnp.tanh(math.sqrt(2.0 / math.pi) * (x + 0.044715 * (x * x * x))))


def _peer_dense_kernel(h_ref, s1_ref, e1_ref, s2_ref, e2_ref, th_ref, u_ref, vt_ref, o_ref,
                       act_ref, g_ref, acc_ref):
    j = pl.program_id(1)
    tm = h_ref.shape[0]

    @pl.when(j == 0)
    def _():
        acc_ref[...] = jnp.zeros_like(acc_ref)

    nc = tm // LANES
    act = lax.dot_general(u_ref[...], h_ref[...], NT, preferred_element_type=F32)
    for c in range(nc):
        act_ref[c] = act[:, c * LANES:(c + 1) * LANES]

    def tile(idx, carry):
        a, c = idx // nc, idx % nc
        rows = pl.ds(pl.multiple_of(a * PEER_NKEYS, PEER_NKEYS), PEER_NKEYS)
        w = jnp.zeros((PEER_NKEYS, LANES), F32)
        for h in range(PEER_HEADS):
            need = th_ref[h, c] - s1_ref[h, c, pl.ds(a, 1), :]
            w = w + e1_ref[h, c, pl.ds(a, 1), :] * jnp.where(s2_ref[h, c] >= need, e2_ref[h, c], 0.0)
        g_ref[c, rows, :] = (w * _gelu_tanh(act_ref[c, rows, :])).astype(BF16)
        return carry

    lax.fori_loop(0, PEER_A_PER_STEP * nc, tile, 0)
    g = jnp.concatenate([g_ref[c] for c in range(nc)], axis=1)
    acc_ref[...] += jnp.dot(vt_ref[...], g, preferred_element_type=F32)

    @pl.when(j == pl.num_programs(1) - 1)
    def _():
        o_ref[...] = acc_ref[...].T


def peer_dense(h, s1, s2, e1, e2, th, u, vt, *, tm):
    t, d = h.shape
    ne = PEER_A_PER_STEP * PEER_NKEYS
    nc = tm // LANES
    tok = pl.BlockSpec((PEER_HEADS, nc, PEER_NKEYS, LANES), lambda i, j: (0, i, 0, 0))
    arow = pl.BlockSpec((PEER_HEADS, nc, PEER_A_PER_STEP, LANES), lambda i, j: (0, i, j, 0))
    return pl.pallas_call(
        _peer_dense_kernel,
        out_shape=jax.ShapeDtypeStruct((t, d), F32),
        grid=(t // tm, PEER_NKEYS // PEER_A_PER_STEP),
        in_specs=[
            pl.BlockSpec((tm, d), lambda i, j: (i, 0)),
            arow, arow, tok, tok,
            pl.BlockSpec((PEER_HEADS, nc, 1, LANES), lambda i, j: (0, i, 0, 0)),
            pl.BlockSpec((ne, d), lambda i, j: (j, 0)),
            pl.BlockSpec((d, ne), lambda i, j: (0, j)),
        ],
        out_specs=pl.BlockSpec((tm, d), lambda i, j: (i, 0)),
        scratch_shapes=[pltpu.VMEM((nc, ne, LANES), F32), pltpu.VMEM((nc, ne, LANES), BF16),
                        pltpu.VMEM((d, tm), F32)],
        compiler_params=_params("parallel", "arbitrary"),
        name="peer_dense",
    )(h, s1, e1, s2, e2, th, u, vt)


def _peer_tables(u, v):
    return u.astype(BF16), v.astype(BF16).T


def peer_ffn(x, scale, shift, wq, keys, u_bf, vt_bf, b, s):
    qv, h = project(x, scale, shift, wq, b, s)
    tm = _row_tile(b * s)
    s1, s2, e1, e2, th = peer_route(qv, keys, tm=min(tm, PEER_ROUTE_TILE))
    return peer_dense(h, s1, s2, e1, e2, th, u_bf, vt_bf, tm=tm)


def kernel(x_prompt, x_sample, state_gla, cache_cmp_k, cache_cmp_v, cache_sel_k, cache_sel_v, cache_win_k, cache_win_v, page_table, c_prompt, c_sample, ada_w, ada_b, ln_g, ln_b, gla_w_in, gla_w_a2, gla_b_a2, gla_gn, gla_w_out, nsa_w_in, nsa_pe_k, nsa_pe_v, nsa_w_out, peer_wq, peer_keys, peer_u, peer_v):
    bp, sp, d = x_prompt.shape
    bs, ss, _ = x_sample.shape
    groups = ((bp, sp), (bs, ss))
    c_all = jnp.concatenate([c_prompt, c_sample], axis=0)
    c_rows = _round_up(bp + bs, SUBLANES)
    mod = ada_mod(jnp.pad(c_all, ((0, c_rows - bp - bs), (0, 0))), ada_w, ada_b)
    ys = [x_prompt.reshape(bp * sp, d), x_sample.reshape(bs * ss, d)]
    gla_states, nsa_rows = ([], []), ([], [])
    for i in range(DEPTH):
        row0 = (0, bp)
        mods = [[mod[i, row0[n]:row0[n] + b, k * d:(k + 1) * d][:, None, :] for k in range(6)]
                for n, (b, _) in enumerate(groups)]
        j = i // N_MIXERS
        if i % N_MIXERS == 0:
            for n, (b, s) in enumerate(groups):
                shift, scale, gate = mods[n][:3]
                proj, _ = project(ys[n], scale, shift, gla_w_in[j], b, s)
                o, st = gla_recurrence(proj, gla_w_a2[j], gla_b_a2[j], None if n == 0 else state_gla[j], b, s)
                gla_states[n].append(st.astype(state_gla.dtype))
                ys[n] = post(ys[n], gate, ln_g[i, 0], ln_b[i, 0], b, s, sub=o, w_out=gla_w_out[j],
                             gla_proj=proj, gla_gn=gla_gn[j])
        else:
            w_in_p, w_out_p = _nsa_weights(nsa_w_in[j], nsa_w_out[j])
            for n, (b, s) in enumerate(groups):
                shift, scale, gate = mods[n][:3]
                if n == 0:
                    o, rows = nsa_prompt(ys[n], scale, shift, w_in_p, nsa_pe_k[j], nsa_pe_v[j], b, s)
                else:
                    o, rows = nsa_sample(ys[n], scale, shift, cache_cmp_k[j], cache_cmp_v[j], cache_sel_k[j],
                                         cache_sel_v[j], cache_win_k[j], cache_win_v[j], page_table,
                                         w_in_p, nsa_pe_k[j], nsa_pe_v[j], b, s)
                nsa_rows[n].append(rows)
                ys[n] = post(ys[n], gate, ln_g[i, 0], ln_b[i, 0], b, s, sub=o, w_out=w_out_p)
        u4, vt4 = _peer_tables(peer_u[i], peer_v[i])
        for n, (b, s) in enumerate(groups):
            shift, scale, gate = mods[n][3:]
            f = peer_ffn(ys[n], scale, shift, peer_wq[i], peer_keys[i], u4, vt4, b, s)
            ys[n] = post(ys[n], gate, ln_g[i, 1], ln_b[i, 1], b, s, sub=f)

    st = lambda ts, k: jnp.stack([t[k] for t in ts])
    return (ys[0].reshape(bp, sp, d), ys[1].reshape(bs, ss, d),
            jnp.stack(gla_states[0]), jnp.stack(gla_states[1]),
            *(st(nsa_rows[0], k) for k in range(6)), *(st(nsa_rows[1], k) for k in range(6)))
```
